```python
import math
import jax, jax.numpy as jnp
from jax import lax
import numpy as np

D_MODEL = 1024
BATCH = 16
SEQ = 256
DEPTH = 4
DEC_BATCH = 2
DEC_SEQ = 1024
PAST_LEN = 256

GRID_W = 64
CHUNK = 128
N_GROUPS_A = 4
D_A = D_MODEL
N_HEADS = 8
HEAD_DIM = D_MODEL // (2 * N_HEADS)
V_DIM = 2 * HEAD_DIM
ATTN_W = N_HEADS * V_DIM
D_FF = ((8 * D_MODEL // 3 + 127) // 128) * 128
IN_W = 2 * D_A + 3 * ATTN_W + 2 * D_MODEL
N_MOD = 9
ROPE_THETA = 10000.0
Q_BLOCK = 128
EPS = 1e-6

kernel_name = 'hybrid_dit_gmlp_diffattn_step'


def rmsnorm(x, g):
    xf = x.astype(jnp.float32)
    r = lax.rsqrt(jnp.mean(xf * xf, axis=-1, keepdims=True) + EPS)
    return (xf * r).astype(x.dtype) * g


def swiglu(h, w13, w2):
    gate, up = jnp.split(h @ w13, 2, axis=-1)
    return (jax.nn.silu(gate) * up) @ w2


def spatial_gating(z, v_gain, w_s, b_s):
    u, v = jnp.split(z, 2, axis=-1)
    v = rmsnorm(v, v_gain)
    B, N, _ = v.shape
    vc = v.reshape(B, N // CHUNK, CHUNK, N_GROUPS_A, D_A // N_GROUPS_A)
    mixed = jnp.einsum('gpq,bnqgc->bnpgc', w_s, vc) + jnp.swapaxes(b_s, 0, 1)[:, :, None]
    return u * mixed.reshape(B, N, D_A)


def rope_half(x, ang):
    cos = jnp.cos(ang)[None, :, None, None, :].astype(x.dtype)
    sin = jnp.sin(ang)[None, :, None, None, :].astype(x.dtype)
    x1, x2 = jnp.split(x, 2, axis=-1)
    return jnp.concatenate([x1 * cos - x2 * sin, x2 * cos + x1 * sin], axis=-1)


def axial_rope(x):
    n = x.shape[1]
    n_rows = n // GRID_W
    row = jnp.repeat(jnp.arange(n_rows, dtype=jnp.float32), GRID_W)
    col = jnp.tile(jnp.arange(GRID_W, dtype=jnp.float32), n_rows)
    half = HEAD_DIM // 2
    freqs = ROPE_THETA ** (-jnp.arange(0, half, 2, dtype=jnp.float32) / half)
    xr = rope_half(x[..., :half], row[:, None] * freqs[None, :])
    xc = rope_half(x[..., half:], col[:, None] * freqs[None, :])
    return jnp.concatenate([xr, xc], axis=-1)


def diff_attention(q, k, v, lam):
    B, Nq = q.shape[0], q.shape[1]
    nb = Nq // Q_BLOCK
    qb = jnp.swapaxes(q.reshape(B, nb, Q_BLOCK, N_HEADS, 2, HEAD_DIM), 0, 1)
    scale = HEAD_DIM ** -0.5

    def block(qi):
        s = jnp.einsum('bqhid,bkhid->bihqk', qi, k).astype(jnp.float32) * scale
        p = jax.nn.softmax(s, axis=-1)
        a = p[:, 0] - lam * p[:, 1]
        return jnp.einsum('bhqk,bkhe->bqhe', a.astype(v.dtype), v)

    out = lax.map(block, qb)
    return jnp.swapaxes(out, 0, 1).reshape(B, Nq, N_HEADS, V_DIM)


def setup_inputs(seed: int = 0) -> dict:
    key = jax.random.key(seed)
    ks = jax.random.split(key, 32)
    f32 = jnp.float32

    def nrm(k, shape, scale):
        return jax.random.normal(k, shape, f32) * scale

    def gain(k, shape):
        return 1.0 + 0.01 * jax.random.normal(k, shape, f32)

    return {
        'x_prompt': nrm(ks[0], (BATCH, SEQ, D_MODEL), 1.0),
        'x_sample': nrm(ks[1], (DEC_BATCH, DEC_SEQ, D_MODEL), 1.0),
        'cache_k': nrm(ks[2], (DEC_BATCH, DEPTH, PAST_LEN, N_HEADS, 2 * HEAD_DIM), 1.0),
        'cache_v': nrm(ks[3], (DEC_BATCH, DEPTH, PAST_LEN, N_HEADS, V_DIM), 1.0),
        'c': nrm(ks[4], (DEC_BATCH, D_MODEL), 1.0),
        'c_ctx': nrm(ks[5], (D_MODEL,), 1.0),
        'w_mod': nrm(ks[6], (DEPTH, D_MODEL, N_MOD * D_MODEL), 0.5 * D_MODEL ** -0.5),
        'b_mod': nrm(ks[7], (DEPTH, N_MOD * D_MODEL), 0.01),
        'g_norm': gain(ks[8], (DEPTH, 3, D_MODEL)),
        'ffn1_w13': nrm(ks[9], (DEPTH, D_MODEL, 2 * D_FF), D_MODEL ** -0.5),
        'ffn1_w2': nrm(ks[10], (DEPTH, D_FF, D_MODEL), D_FF ** -0.5),
        'w_in': nrm(ks[11], (DEPTH, D_MODEL, IN_W), D_MODEL ** -0.5),
        'sgu_gain': gain(ks[12], (DEPTH, D_A)),
        'w_spatial': nrm(ks[13], (DEPTH, N_GROUPS_A, CHUNK, CHUNK), CHUNK ** -0.5),
        'b_spatial': gain(ks[14], (DEPTH, N_GROUPS_A, CHUNK)),
        'lam': nrm(ks[15], (DEPTH, 4, HEAD_DIM), 0.1),
        'subln_gain': gain(ks[16], (DEPTH, V_DIM)),
        'w_branch_a': nrm(ks[17], (DEPTH, D_A, D_MODEL), D_A ** -0.5),
        'w_branch_b': nrm(ks[18], (DEPTH, ATTN_W, D_MODEL), ATTN_W ** -0.5),
        'w_out': nrm(ks[19], (DEPTH, D_MODEL, D_MODEL), D_MODEL ** -0.5),
        'ffn2_w13': nrm(ks[20], (DEPTH, D_MODEL, 2 * D_FF), D_MODEL ** -0.5),
        'ffn2_w2': nrm(ks[21], (DEPTH, D_FF, D_MODEL), D_FF ** -0.5),
        'g_final': gain(ks[22], (D_MODEL,)),
    }


def reference(x_prompt, x_sample, cache_k, cache_v, c, c_ctx, w_mod, b_mod, g_norm,
              ffn1_w13, ffn1_w2, w_in, sgu_gain, w_spatial, b_spatial, lam, subln_gain,
              w_branch_a, w_branch_b, w_out, ffn2_w13, ffn2_w2, g_final):

    def mixer(h, l, ctx_kv, use_rope):
        B, N, _ = h.shape
        proj = h @ w_in[l]
        z, q, k, v, gates = jnp.split(
            proj, [2 * D_A, 2 * D_A + ATTN_W, 2 * D_A + 2 * ATTN_W, 2 * D_A + 3 * ATTN_W], axis=-1)
        a = spatial_gating(jax.nn.gelu(z), sgu_gain[l], w_spatial[l], b_spatial[l])
        q = q.reshape(B, N, N_HEADS, 2, HEAD_DIM)
        k = k.reshape(B, N, N_HEADS, 2, HEAD_DIM)
        v = v.reshape(B, N, N_HEADS, V_DIM)
        k_state = k.reshape(B, N, N_HEADS, 2 * HEAD_DIM)
        v_state = v
        if use_rope:
            q = axial_rope(q)
            k = axial_rope(k)
        if ctx_kv is not None:
            ck, cv = ctx_kv
            k = jnp.concatenate([ck.reshape(B, ck.shape[1], N_HEADS, 2, HEAD_DIM), k], axis=1)
            v = jnp.concatenate([cv, v], axis=1)
        lam_init = 0.8 - 0.6 * math.exp(-0.3 * l)
        lp = lam[l].astype(jnp.float32)
        lam_full = jnp.exp(jnp.sum(lp[0] * lp[1])) - jnp.exp(jnp.sum(lp[2] * lp[3])) + lam_init
        o = diff_attention(q, k, v, lam_full)
        o = rmsnorm(o, subln_gain[l]) * (1.0 - lam_init)
        b = o.reshape(B, N, ATTN_W)
        g_a, g_b = jnp.split(gates, 2, axis=-1)
        merged = jax.nn.sigmoid(g_a) * (a @ w_branch_a[l]) + jax.nn.sigmoid(g_b) * (b @ w_branch_b[l])
        return merged @ w_out[l], k_state, v_state

    def run_layer(x, cond, l, ctx_kv, use_rope):
        mod = jax.nn.silu(cond) @ w_mod[l] + b_mod[l]
        sh1, sc1, gt1, sh2, sc2, gt2, sh3, sc3, gt3 = jnp.split(mod, N_MOD, axis=-1)
        h = rmsnorm(x, g_norm[l, 0]) * (1.0 + sc1) + sh1
        x = x + 0.5 * gt1 * swiglu(h, ffn1_w13[l], ffn1_w2[l])
        h = rmsnorm(x, g_norm[l, 1]) * (1.0 + sc2) + sh2
        m, k_state, v_state = mixer(h, l, ctx_kv, use_rope)
        x = x + gt2 * m
        h = rmsnorm(x, g_norm[l, 2]) * (1.0 + sc3) + sh3
        x = x + 0.5 * gt3 * swiglu(h, ffn2_w13[l], ffn2_w2[l])
        return x, k_state, v_state

    xp = x_prompt
    k_list, v_list = [], []
    for l in range(DEPTH):
        xp, k_l, v_l = run_layer(xp, c_ctx, l, None, False)
        k_list.append(k_l)
        v_list.append(v_l)
    y_prompt = rmsnorm(xp, g_final)
    new_cache_k = jnp.stack(k_list, axis=1)
    new_cache_v = jnp.stack(v_list, axis=1)

    xs = x_sample
    cond = c[:, None, :]
    for l in range(DEPTH):
        xs, _, _ = run_layer(xs, cond, l, (cache_k[:, l], cache_v[:, l]), True)
    y_sample = rmsnorm(xs, g_final)

    return (y_prompt, y_sample, new_cache_k, new_cache_v)
```

```python
import functools
import math

import numpy as np
import jax
import jax.numpy as jnp
from jax import lax
from jax.experimental import pallas as pl
from jax.experimental.pallas import tpu as pltpu

D_MODEL = 1024
BATCH = 16
SEQ = 256
DEPTH = 4
DEC_BATCH = 2
DEC_SEQ = 1024
PAST_LEN = 256
GRID_W = 64
CHUNK = 128
N_GROUPS_A = 4
GROUP_W = D_MODEL // N_GROUPS_A
N_HEADS = 8
HEAD_DIM = D_MODEL // (2 * N_HEADS)
V_DIM = 2 * HEAD_DIM
D_FF = ((8 * D_MODEL // 3 + 127) // 128) * 128
IN_W = 7 * D_MODEL
N_MOD = 9
ROPE_THETA = 10000.0
EPS = 1e-6

N_CTX = BATCH * SEQ
N_LAT = DEC_BATCH * DEC_SEQ
N_TOK = N_CTX + N_LAT
COND_ROWS = 8

ROW_TILE = 256
FF_CHUNK = 256
MOD_TILE = 2304
Q_TILE = 256
V7X_VMEM_BYTES = 64 * 1024 * 1024

F32 = jnp.float32
BF16 = jnp.bfloat16


def _params(vmem_mib):
    assert vmem_mib * 1024 * 1024 < V7X_VMEM_BYTES
    return pltpu.CompilerParams(dimension_semantics=None, vmem_limit_bytes=vmem_mib * 1024 * 1024)


def _resident(block_shape, index_map):
    return pl.BlockSpec(block_shape, index_map, pipeline_mode=pl.Buffered(1))


def _dot(a, b):
    return jnp.dot(a, b, preferred_element_type=F32)


def _dot_nt(a, b):
    return lax.dot_general(a, b, (((1,), (1,)), ((), ())), preferred_element_type=F32)


def _rms(x):
    return x * lax.rsqrt(jnp.mean(x * x, axis=-1, keepdims=True) + EPS)


def _norm_mod(x, g, shift, scale):
    return _rms(x) * g * (1.0 + scale) + shift


def _stacked_mod_row(i):
    n_ctx_tiles = N_CTX // ROW_TILE
    return jnp.where(i < n_ctx_tiles, 0, 1 + (i - n_ctx_tiles) // (DEC_SEQ // ROW_TILE))


def _mod_chunk(mod_ref, row, k):
    return mod_ref[pl.ds(row, 1), k * D_MODEL:(k + 1) * D_MODEL]


def _mod_kernel(c_ref, w_ref, b_ref, o_ref):
    c = c_ref[...]
    o_ref[...] = _dot(c * jax.nn.sigmoid(c), w_ref[...]) + b_ref[...]


def _modulation(cond, w_mod, b_mod):
    n = N_MOD * D_MODEL
    return pl.pallas_call(
        _mod_kernel,
        grid=(DEPTH, n // MOD_TILE),
        in_specs=[
            pl.BlockSpec((COND_ROWS, D_MODEL), lambda l, j: (0, 0)),
            pl.BlockSpec((None, D_MODEL, MOD_TILE), lambda l, j: (l, 0, j)),
            pl.BlockSpec((None, 1, MOD_TILE), lambda l, j: (l, 0, j)),
        ],
        out_specs=pl.BlockSpec((None, COND_ROWS, MOD_TILE), lambda l, j: (l, 0, j)),
        out_shape=jax.ShapeDtypeStruct((DEPTH, COND_ROWS, n), F32),
        compiler_params=_params(32),
        name="modulation",
    )(cond, w_mod, b_mod.reshape(DEPTH, 1, n))


def _ffn_kernel(x_ref, mod_ref, g_ref, w13_ref, w2_ref, *rest, sub, final):
    o_ref = rest[-1]
    row = _stacked_mod_row(pl.program_id(0))
    x = x_ref[...]
    h = _norm_mod(x, g_ref[sub:sub + 1, :], _mod_chunk(mod_ref, row, 3 * sub),
                  _mod_chunk(mod_ref, row, 3 * sub + 1))
    acc = jnp.zeros((ROW_TILE, D_MODEL), F32)
    for lo in range(0, D_FF, FF_CHUNK):
        gate = _dot(h, w13_ref[:, lo:lo + FF_CHUNK])
        up = _dot(h, w13_ref[:, D_FF + lo:D_FF + lo + FF_CHUNK])
        acc = acc + _dot(gate * jax.nn.sigmoid(gate) * up, w2_ref[lo:lo + FF_CHUNK, :])
    y = x + 0.5 * _mod_chunk(mod_ref, row, 3 * sub + 2) * acc
    if final:
        y = _rms(y) * rest[0][...]
    o_ref[...] = y


def _ffn(x, mod, g_norm, w13, w2, g_final, l, sub):
    final = g_final is not None
    in_specs = [
        pl.BlockSpec((ROW_TILE, D_MODEL), lambda i: (i, 0)),
        _resident((None, COND_ROWS, N_MOD * D_MODEL), lambda i: (l, 0, 0)),
        _resident((None, 3, D_MODEL), lambda i: (l, 0, 0)),
        _resident((None, D_MODEL, 2 * D_FF), lambda i: (l, 0, 0)),
        _resident((None, D_FF, D_MODEL), lambda i: (l, 0, 0)),
    ]
    args = [x, mod, g_norm, w13, w2]
    if final:
        in_specs.append(_resident((1, D_MODEL), lambda i: (0, 0)))
        args.append(g_final.reshape(1, D_MODEL))
    return pl.pallas_call(
        functools.partial(_ffn_kernel, sub=sub, final=final),
        grid=(N_TOK // ROW_TILE,),
        in_specs=in_specs,
        out_specs=pl.BlockSpec((ROW_TILE, D_MODEL), lambda i: (i, 0)),
        out_shape=jax.ShapeDtypeStruct((N_TOK, D_MODEL), F32),
        input_output_aliases={0: 0},
        compiler_params=_params(56),
        name="ffn",
    )(*args)


def _gelu_tanh(x):
    return x * (0.5 * (1.0 + jnp.tanh(math.sqrt(2.0 / math.pi) * (x + 0.044715 * (x * x * x)))))


def _sgu_kernel(x_ref, mod_ref, g_ref, wz_ref, gain_ref, ws_ref, bias_ref, o_ref, *, l):
    row = _stacked_mod_row(pl.program_id(0))
    h = _norm_mod(x_ref[...], g_ref[1:2, :], _mod_chunk(mod_ref, row, 3), _mod_chunk(mod_ref, row, 4))
    z = _gelu_tanh(_dot(h, wz_ref[...]))
    u = z[:, :D_MODEL]
    v = _rms(z[:, D_MODEL:]) * gain_ref[l:l + 1, :]
    for r0 in range(0, ROW_TILE, CHUNK):
        for g in range(N_GROUPS_A):
            c0 = g * GROUP_W
            mixed = _dot(ws_ref[l, g], v[r0:r0 + CHUNK, c0:c0 + GROUP_W]) + bias_ref[:, c0:c0 + GROUP_W]
            o_ref[r0:r0 + CHUNK, c0:c0 + GROUP_W] = (u[r0:r0 + CHUNK, c0:c0 + GROUP_W] * mixed).astype(BF16)


def _sgu(x, mod, g_norm, w_in, sgu_gain, w_spatial, bias_tile, l):
    return pl.pallas_call(
        functools.partial(_sgu_kernel, l=l),
        grid=(N_TOK // ROW_TILE,),
        in_specs=[
            pl.BlockSpec((ROW_TILE, D_MODEL), lambda i: (i, 0)),
            _resident((None, COND_ROWS, N_MOD * D_MODEL), lambda i: (l, 0, 0)),
            _resident((None, 3, D_MODEL), lambda i: (l, 0, 0)),
            _resident((None, D_MODEL, 2 * D_MODEL), lambda i: (l, 0, 0)),
            _resident((DEPTH, D_MODEL), lambda i: (0, 0)),
            _resident((DEPTH, N_GROUPS_A, CHUNK, CHUNK), lambda i: (0, 0, 0, 0)),
            _resident((None, CHUNK, D_MODEL), lambda i: (l, 0, 0)),
        ],
        out_specs=pl.BlockSpec((ROW_TILE, D_MODEL), lambda i: (i, 0)),
        out_shape=jax.ShapeDtypeStruct((N_TOK, D_MODEL), BF16),
        compiler_params=_params(40),
        name="sgu",
    )(x, mod, g_norm, w_in, sgu_gain, w_spatial, bias_tile)


def _rope_tables():
    pos = np.arange(DEC_SEQ)
    lane = np.arange(V_DIM)
    half = HEAD_DIM // 2
    within = lane % half
    coord = np.where((lane % HEAD_DIM) < half, pos[:, None] // GRID_W, pos[:, None] % GRID_W)
    freqs = ROPE_THETA ** (-np.arange(0, half, 2, dtype=np.float64) / half)
    ang = coord * freqs[within % (half // 2)][None, :]
    first = (within < half // 2)[None, :]
    cos = np.cos(ang)
    sin_lo = np.where(first, -np.sin(ang), 0.0)
    sin_hi = np.where(first, 0.0, np.sin(ang))
    return tuple(jnp.asarray(t, dtype=F32) for t in (cos, sin_lo, sin_hi))


def _qkv_kernel(x_ref, mod_ref, g_ref, wq_ref, wk_ref, wv_ref, *rest, latent, n_extra):
    extra, outs = rest[:n_extra], rest[n_extra:]
    q_ref, k_ref, v_ref = outs[:3]
    row = 1 + pl.program_id(0) // (DEC_SEQ // ROW_TILE) if latent else 0
    h =_norm_mod(x_ref[...], g_ref[1:2, :], _mod_chunk(mod_ref, row, 3), _mod_chunk(mod_ref, row, 4))
    q = _dot(h, wq_ref[...]) * (HEAD_DIM ** -0.5)
    k = _dot(h, wk_ref[...])
    v = _dot(h, wv_ref[...])
    v_ref[...] = v.astype(BF16)
    if not latent:
        kc_ref, vc_ref = outs[3:]
        kc_ref[...] = k
        vc_ref[...] = v
        q_ref[...] = q.astype(BF16)
        k_ref[...] = k.astype(BF16)
        return
    cos, sin_lo, sin_hi = (r[...] for r in extra)
    for c0 in range(0, D_MODEL, V_DIM):
        for src, dst in ((q, q_ref), (k, k_ref)):
            t = src[:, c0:c0 + V_DIM]
            t = (t * cos + pltpu.roll(t, V_DIM - HEAD_DIM // 4, 1) * sin_lo
                 + pltpu.roll(t, HEAD_DIM // 4, 1) * sin_hi)
            dst[:, c0:c0 + V_DIM] = t.astype(BF16)


def _qkv(x, mod, g_norm, w_in, l, latent, rope=None, caches=None):
    rows, tile0 = (N_LAT, N_CTX // ROW_TILE) if latent else (N_CTX, 0)
    col0 = 2
    in_specs = [
        pl.BlockSpec((ROW_TILE, D_MODEL), lambda i: (tile0 + i, 0)),
        _resident((None, COND_ROWS, N_MOD * D_MODEL), lambda i: (l, 0, 0)),
        _resident((None, 3, D_MODEL), lambda i: (l, 0, 0)),
    ] + [_resident((None, D_MODEL, D_MODEL), functools.partial(lambda i, c: (l, 0, c), c=col0 + j))
         for j in range(3)]
    args = [x, mod, g_norm, w_in, w_in, w_in]
    act = jax.ShapeDtypeStruct((rows, D_MODEL), BF16)
    out_shape = [act, act, act]
    out_specs = [pl.BlockSpec((ROW_TILE, D_MODEL), lambda i: (i, 0))] * 3
    aliases = {}
    if latent:
        tiles = DEC_SEQ // ROW_TILE
        in_specs += [pl.BlockSpec((ROW_TILE, V_DIM), lambda i: (i % tiles, 0))] * 3
        args += list(rope)
    else:
        assert ROW_TILE == SEQ
        cache = jax.ShapeDtypeStruct((BATCH, DEPTH, SEQ, D_MODEL), F32)
        out_shape += [cache, cache]
        out_specs += [pl.BlockSpec((None, None, SEQ, D_MODEL), lambda i: (i, l, 0, 0))] * 2
        if caches is not None:
            in_specs += [pl.BlockSpec(memory_space=pl.ANY)] * 2
            args += list(caches)
            aliases = {len(args) - 2: 3, len(args) - 1: 4}
    return pl.pallas_call(
        functools.partial(_qkv_kernel, latent=latent, n_extra=len(args) - 6),
        grid=(rows // ROW_TILE,),
        in_specs=in_specs,
        out_specs=out_specs,
        out_shape=out_shape,
        input_output_aliases=aliases,
        compiler_params=_params(40),
        name="qkv_latent" if latent else "qkv_context",
    )(*args)


def _lam_full(lam_ref, l):
    lp = lam_ref[l]
    s01 = jnp.sum(lp[0:1, :] * lp[1:2, :], axis=-1, keepdims=True)
    s23 = jnp.sum(lp[2:3, :] * lp[3:4, :], axis=-1, keepdims=True)
    return jnp.exp(s01) - jnp.exp(s23) + _lam_init(l)


def _lam_init(l):
    return 0.8 - 0.6 * math.exp(-0.3 * l)


def _attn_head(q, ks, vs, lam_full, gain, l):
    lane = lax.broadcasted_iota(jnp.int32, (1, V_DIM), 1)
    halves = []
    for keep in ((lane < HEAD_DIM), (lane >= HEAD_DIM)):
        qi = q * keep.astype(BF16)
        s = [_dot_nt(qi, k) for k in ks]
        m = functools.reduce(jnp.maximum, [jnp.max(t, axis=-1, keepdims=True) for t in s])
        e = [jnp.exp(t - m) for t in s]
        denom = functools.reduce(jnp.add, [jnp.sum(t, axis=-1, keepdims=True) for t in e])
        pv = functools.reduce(jnp.add, [_dot(t.astype(BF16), v) for t, v in zip(e, vs)])
        halves.append(pv / denom)
    o = halves[0] - lam_full * halves[1]
    return _rms(o) * gain * (1.0 - _lam_init(l))


def _attn_ctx_kernel(q_ref, k_ref, v_ref, lam_ref, gain_ref, o_ref, *, l):
    lam_full = _lam_full(lam_ref, l)
    gain = gain_ref[l:l + 1, :]
    for c0 in range(0, D_MODEL, V_DIM):
        cs = slice(c0, c0 + V_DIM)
        o = _attn_head(q_ref[:, cs], [k_ref[:, cs]], [v_ref[:, cs]], lam_full, gain, l)
        o_ref[:, cs] = o.astype(BF16)


def _attn_lat_kernel(q_ref, k_ref, v_ref, kc_ref, vc_ref, lam_ref, gain_ref, _, o_ref, *, l):
    lam_full = _lam_full(lam_ref, l)
    gain = gain_ref[l:l + 1, :]
    for c0 in range(0, D_MODEL, V_DIM):
        cs = slice(c0, c0 + V_DIM)
        ks = [kc_ref[:, cs].astype(BF16), k_ref[:, cs]]
        vs = [vc_ref[:, cs].astype(BF16), v_ref[:, cs]]
        o = _attn_head(q_ref[:, cs], ks, vs, lam_full, gain, l)
        o_ref[:, cs] = o.astype(BF16)


def _attn_ctx(q, k, v, lam, subln_gain, l):
    seq = pl.BlockSpec((SEQ, D_MODEL), lambda i: (i, 0))
    return pl.pallas_call(
        functools.partial(_attn_ctx_kernel, l=l),
        grid=(BATCH,),
        in_specs=[seq, seq, seq,
                  _resident((DEPTH, 4, HEAD_DIM), lambda i: (0, 0, 0)),
                  _resident((DEPTH, V_DIM), lambda i: (0, 0))],
        out_specs=seq,
        out_shape=jax.ShapeDtypeStruct((N_TOK, D_MODEL), BF16),
        compiler_params=_params(32),
        name="attn_context",
    )(q, k, v, lam, subln_gain)


def _attn_lat(q, k, v, cache_k, cache_v, lam, subln_gain, b_all, l):
    tiles = DEC_SEQ // Q_TILE
    qspec = pl.BlockSpec((Q_TILE, D_MODEL), lambda b, j: (b * tiles + j, 0))
    kv = pl.BlockSpec((DEC_SEQ, D_MODEL), lambda b, j: (b, 0))
    past = pl.BlockSpec((None, None, PAST_LEN, D_MODEL), lambda b, j: (b, l, 0, 0))
    return pl.pallas_call(
        functools.partial(_attn_lat_kernel, l=l),
        grid=(DEC_BATCH, tiles),
        in_specs=[qspec, kv, kv, past, past,
                  _resident((DEPTH, 4, HEAD_DIM), lambda b, j: (0, 0, 0)),
                  _resident((DEPTH, V_DIM), lambda b, j: (0, 0)),
                  pl.BlockSpec(memory_space=pl.ANY)],
        out_specs=pl.BlockSpec((Q_TILE, D_MODEL), lambda b, j: (N_CTX // Q_TILE + b * tiles + j, 0)),
        out_shape=jax.ShapeDtypeStruct((N_TOK, D_MODEL), BF16),
        input_output_aliases={7: 0},
        compiler_params=_params(40),
        name="attn_latent",
    )(q, k, v, cache_k, cache_v, lam, subln_gain, b_all)


def _merge_kernel(x_ref, a_ref, b_ref, mod_ref, g_ref, wga_ref, wgb_ref, wa_ref, wb_ref, wo_ref, o_ref):
    row = _stacked_mod_row(pl.program_id(0))
    x = x_ref[...]
    h = _norm_mod(x, g_ref[1:2, :], _mod_chunk(mod_ref, row, 3), _mod_chunk(mod_ref, row, 4))
    merged = (jax.nn.sigmoid(_dot(h, wga_ref[...])) * _dot(a_ref[...].astype(F32), wa_ref[...])
              + jax.nn.sigmoid(_dot(h, wgb_ref[...])) * _dot(b_ref[...].astype(F32), wb_ref[...]))
    o_ref[...] = x + _mod_chunk(mod_ref, row, 5) * _dot(merged, wo_ref[...])


def _merge(x, a, b, mod, g_norm, w_in, w_branch_a, w_branch_b, w_out, l):
    tile = pl.BlockSpec((ROW_TILE, D_MODEL), lambda i: (i, 0))
    square = _resident((None, D_MODEL, D_MODEL), lambda i: (l, 0, 0))
    return pl.pallas_call(
        _merge_kernel,
        grid=(N_TOK // ROW_TILE,),
        in_specs=[tile, tile, tile,
                  _resident((None, COND_ROWS, N_MOD * D_MODEL), lambda i: (l, 0, 0)),
                  _resident((None, 3, D_MODEL), lambda i: (l, 0, 0)),
                  _resident((None, D_MODEL, D_MODEL), lambda i: (l, 0, 5)),
                  _resident((None, D_MODEL, D_MODEL), lambda i: (l, 0, 6)),
                  square, square, square],
        out_specs=tile,
        out_shape=jax.ShapeDtypeStruct((N_TOK, D_MODEL), F32),
        input_output_aliases={0: 0},
        compiler_params=_params(48),
        name="merge",
    )(x, a, b, mod, g_norm, w_in, w_in, w_branch_a, w_branch_b, w_out)


def kernel(x_prompt, x_sample, cache_k, cache_v, c, c_ctx, w_mod, b_mod, g_norm, ffn1_w13, ffn1_w2, w_in,
           sgu_gain, w_spatial, b_spatial, lam, subln_gain, w_branch_a, w_branch_b, w_out, ffn2_w13,
           ffn2_w2, g_final):
    x = jnp.concatenate([x_prompt.reshape(N_CTX, D_MODEL), x_sample.reshape(N_LAT, D_MODEL)], axis=0)
    cond = jnp.concatenate(
        [c_ctx[None, :], c, jnp.zeros((COND_ROWS - 1 - DEC_BATCH, D_MODEL), F32)], axis=0)
    mod = _modulation(cond, w_mod, b_mod)
    bias_tile = jnp.repeat(jnp.swapaxes(b_spatial, 1, 2), GROUP_W, axis=2)
    rope = _rope_tables()
    past_k = cache_k.reshape(DEC_BATCH, DEPTH, PAST_LEN, D_MODEL)
    past_v = cache_v.reshape(DEC_BATCH, DEPTH, PAST_LEN, D_MODEL)

    caches = None
    for l in range(DEPTH):
        x = _ffn(x, mod, g_norm, ffn1_w13, ffn1_w2, None, l, 0)
        a = _sgu(x, mod, g_norm, w_in, sgu_gain, w_spatial, bias_tile, l)
        qc, kc, vc, *caches = _qkv(x, mod, g_norm, w_in, l, latent=False, caches=caches)
        ql, kl, vl = _qkv(x, mod, g_norm, w_in, l, latent=True, rope=rope)
        b = _attn_ctx(qc, kc, vc, lam, subln_gain, l)
        b = _attn_lat(ql, kl, vl, past_k, past_v, lam, subln_gain, b, l)
        x = _merge(x, a, b, mod, g_norm, w_in, w_branch_a, w_branch_b, w_out, l)
        x = _ffn(x, mod, g_norm, ffn2_w13, ffn2_w2, g_final if l == DEPTH - 1 else None, l, 2)

    new_k, new_v = caches
    return (x[:N_CTX].reshape(BATCH, SEQ, D_MODEL),
            x[N_CTX:].reshape(DEC_BATCH, DEC_SEQ, D_MODEL),
            new_k.reshape(BATCH, DEPTH, SEQ, N_HEADS, 2 * HEAD_DIM),
            new_v.reshape(BATCH, DEPTH, SEQ, N_HEADS, V_DIM))
```

```python
import functools
import math

import numpy as np
import jax
import jax.numpy as jnp
from jax import lax
from jax.experimental import pallas as pl
from jax.experimental.pallas import tpu as pltpu

D_MODEL = 1024
BATCH = 16
SEQ = 256
DEPTH = 4
DEC_BATCH = 2
DEC_SEQ = 1024
PAST_LEN = 256
GRID_W = 64
CHUNK = 128
N_GROUPS_A = 4
GROUP_W = D_MODEL // N_GROUPS_A
N_HEADS = 8
HEAD_DIM = D_MODEL // (2 * N_HEADS)
V_DIM = 2 * HEAD_DIM
D_FF = ((8 * D_MODEL // 3 + 127) // 128) * 128
IN_W = 7 * D_MODEL
N_MOD = 9
ROPE_THETA = 10000.0
EPS = 1e-6

N_CTX = BATCH * SEQ
N_LAT = DEC_BATCH * DEC_SEQ
N_TOK = N_CTX + N_LAT
COND_ROWS = 8

ROW_TILE = 256
FFN_TILE = 512
FF_CHUNK = 256
MOD_TILE = 2304
Q_TILE = 256
V7X_VMEM_BYTES = 64 * 1024 * 1024
VMEM_LIMIT_BYTES = V7X_VMEM_BYTES - 4 * 1024 * 1024

F32 = jnp.float32
BF16 = jnp.bfloat16


def _params():
    return pltpu.CompilerParams(vmem_limit_bytes=VMEM_LIMIT_BYTES)


def _resident(block_shape, index_map):
    return pl.BlockSpec(block_shape, index_map, pipeline_mode=pl.Buffered(1))


def _dot(a, b):
    return jnp.dot(a, b, preferred_element_type=F32)


def _dot_nt(a, b):
    return lax.dot_general(a, b, (((1,), (1,)), ((), ())), preferred_element_type=F32)


def _rms(x):
    return x * lax.rsqrt(jnp.mean(x * x, axis=-1, keepdims=True) + EPS)


def _norm_mod(x, g, shift, scale):
    return _rms(x) * g * (1.0 + scale) + shift


def _stacked_mod_row(i, tile=ROW_TILE):
    n_ctx_tiles = N_CTX // tile
    return jnp.where(i < n_ctx_tiles, 0, 1 + (i - n_ctx_tiles) // (DEC_SEQ // tile))


def _mod_chunk(mod_ref, row, k):
    return mod_ref[pl.ds(row, 1), k * D_MODEL:(k + 1) * D_MODEL]


def _mod_kernel(c_ref, w_ref, b_ref, o_ref):
    c = c_ref[...]
    o_ref[...] = _dot(c * jax.nn.sigmoid(c), w_ref[...]) + b_ref[...]


def _modulation(cond, w_mod, b_mod):
    n = N_MOD * D_MODEL
    return pl.pallas_call(
        _mod_kernel,
        grid=(DEPTH, n // MOD_TILE),
        in_specs=[
            pl.BlockSpec((COND_ROWS, D_MODEL), lambda l, j: (0, 0)),
            pl.BlockSpec((None, D_MODEL, MOD_TILE), lambda l, j: (l, 0, j)),
            pl.BlockSpec((None, 1, MOD_TILE), lambda l, j: (l, 0, j)),
        ],
        out_specs=pl.BlockSpec((None, COND_ROWS, MOD_TILE), lambda l, j: (l, 0, j)),
        out_shape=jax.ShapeDtypeStruct((DEPTH, COND_ROWS, n), F32),
        compiler_params=_params(),
        name="modulation",
    )(cond, w_mod, b_mod.reshape(DEPTH, 1, n))


def _ffn_kernel(*refs, sub, split_in, final):
    n_x = 2 if split_in else 1
    x_refs, (mod_ref, g_ref, w13_ref, w2_ref), rest = refs[:n_x], refs[n_x:n_x + 4], refs[n_x + 4:]
    i = pl.program_id(0)
    is_ctx = i < N_CTX // FFN_TILE
    row = _stacked_mod_row(i, FFN_TILE)
    x = jnp.where(is_ctx, x_refs[0][...], x_refs[1][...]) if split_in else x_refs[0][...]
    h = _norm_mod(x, g_ref[sub:sub + 1, :], _mod_chunk(mod_ref, row, 3 * sub),
                  _mod_chunk(mod_ref, row, 3 * sub + 1))

    def gate_up(lo):
        return _dot(h, w13_ref[:, lo:lo + FF_CHUNK]), _dot(h, w13_ref[:, D_FF + lo:D_FF + lo + FF_CHUNK])

    acc = jnp.zeros((FFN_TILE, D_MODEL), F32)
    nxt = gate_up(0)
    for lo in range(0, D_FF, FF_CHUNK):
        gate, up = nxt
        if lo + FF_CHUNK < D_FF:
            nxt = gate_up(lo + FF_CHUNK)
        acc = acc + _dot(gate * jax.nn.sigmoid(gate) * up, w2_ref[lo:lo + FF_CHUNK, :])
    y = x + 0.5 * _mod_chunk(mod_ref, row, 3 * sub + 2) * acc
    if not final:
        rest[0][...] = y
        return
    gf_ref, ctx_ref, lat_ref = rest
    y = _rms(y) * gf_ref[...]

    @pl.when(is_ctx)
    def _():
        ctx_ref[...] = y

    @pl.when(jnp.logical_not(is_ctx))
    def _():
        lat_ref[...] = y


def _ffn(xs, mod, g_norm, w13, w2, g_final, l, sub):
    split_in, final = isinstance(xs, tuple), g_final is not None
    n_ctx_tiles = N_CTX // FFN_TILE
    tile = pl.BlockSpec((FFN_TILE, D_MODEL), lambda i: (i, 0))
    ctx_tile = pl.BlockSpec((FFN_TILE, D_MODEL), lambda i: (jnp.minimum(i, n_ctx_tiles - 1), 0))
    lat_tile = pl.BlockSpec((FFN_TILE, D_MODEL), lambda i: (jnp.maximum(i - n_ctx_tiles, 0), 0))
    in_specs = ([ctx_tile, lat_tile] if split_in else [tile]) + [
        _resident((None, COND_ROWS, N_MOD * D_MODEL), lambda i: (l, 0, 0)),
        _resident((None, 3, D_MODEL), lambda i: (l, 0, 0)),
        _resident((None, D_MODEL, 2 * D_FF), lambda i: (l, 0, 0)),
        _resident((None, D_FF, D_MODEL), lambda i: (l, 0, 0)),
    ]
    args = (list(xs) if split_in else [xs]) + [mod, g_norm, w13, w2]
    if final:
        in_specs.append(_resident((1, D_MODEL), lambda i: (0, 0)))
        args.append(g_final.reshape(1, D_MODEL))
        out_specs = [ctx_tile, lat_tile]
        out_shape = [jax.ShapeDtypeStruct((N_CTX, D_MODEL), F32), jax.ShapeDtypeStruct((N_LAT, D_MODEL), F32)]
    else:
        out_specs, out_shape = tile, jax.ShapeDtypeStruct((N_TOK, D_MODEL), F32)
    return pl.pallas_call(
        functools.partial(_ffn_kernel, sub=sub, split_in=split_in, final=final),
        grid=(N_TOK // FFN_TILE,),
        in_specs=in_specs,
        out_specs=out_specs,
        out_shape=out_shape,
        input_output_aliases={} if split_in or final else {0: 0},
        compiler_params=_params(),
        name="ffn",
    )(*args)


def _gelu_tanh(x):
    return x * (0.5 * (1.0 + jnp.tanh(math.sqrt(2.0 / math.pi) * (x + 0.044715 * (x * x * x)))))


def _sgu_kernel(x_ref, mod_ref, g_ref, wz_ref, gain_ref, ws_ref, bias_ref, o_ref, *, l):
    row = _stacked_mod_row(pl.program_id(0))
    h = _norm_mod(x_ref[...], g_ref[1:2, :], _mod_chunk(mod_ref, row, 3), _mod_chunk(mod_ref, row, 4))
    z = _gelu_tanh(_dot(h, wz_ref[...]))
    u = z[:, :D_MODEL]
    v = _rms(z[:, D_MODEL:]) * gain_ref[l:l + 1, :]
    for r0 in range(0, ROW_TILE, CHUNK):
        for g in range(N_GROUPS_A):
            c0 = g * GROUP_W
            mixed = _dot(ws_ref[l, g], v[r0:r0 + CHUNK, c0:c0 + GROUP_W]) + bias_ref[:, c0:c0 + GROUP_W]
            o_ref[r0:r0 + CHUNK, c0:c0 + GROUP_W] = (u[r0:r0 + CHUNK, c0:c0 + GROUP_W] * mixed).astype(BF16)


def _sgu(x, mod, g_norm, w_in, sgu_gain, w_spatial, bias_tile, l):
    return pl.pallas_call(
        functools.partial(_sgu_kernel, l=l),
        grid=(N_TOK // ROW_TILE,),
        in_specs=[
            pl.BlockSpec((ROW_TILE, D_MODEL), lambda i: (i, 0)),
            _resident((None, COND_ROWS, N_MOD * D_MODEL), lambda i: (l, 0, 0)),
            _resident((None, 3, D_MODEL), lambda i: (l, 0, 0)),
            _resident((None, D_MODEL, 2 * D_MODEL), lambda i: (l, 0, 0)),
            _resident((DEPTH, D_MODEL), lambda i: (0, 0)),
            _resident((DEPTH, N_GROUPS_A, CHUNK, CHUNK), lambda i: (0, 0, 0, 0)),
            _resident((None, CHUNK, D_MODEL), lambda i: (l, 0, 0)),
        ],
        out_specs=pl.BlockSpec((ROW_TILE, D_MODEL), lambda i: (i, 0)),
        out_shape=jax.ShapeDtypeStruct((N_TOK, D_MODEL), BF16),
        compiler_params=_params(),
        name="sgu",
    )(x, mod, g_norm, w_in, sgu_gain, w_spatial, bias_tile)


def _rope_tables():
    pos = np.arange(DEC_SEQ)
    lane = np.arange(V_DIM)
    half = HEAD_DIM // 2
    within = lane % half
    coord = np.where((lane % HEAD_DIM) < half, pos[:, None] // GRID_W, pos[:, None] % GRID_W)
    freqs = ROPE_THETA ** (-np.arange(0, half, 2, dtype=np.float64) / half)
    ang = coord * freqs[within % (half // 2)][None, :]
    first = (within < half // 2)[None, :]
    cos = np.cos(ang)
    sin_lo = np.where(first, -np.sin(ang), 0.0)
    sin_hi = np.where(first, 0.0, np.sin(ang))
    return tuple(jnp.asarray(t, dtype=F32) for t in (cos, sin_lo, sin_hi))


def _qkv_kernel(x_ref, mod_ref, g_ref, wq_ref, wk_ref, wv_ref, *rest, latent, n_extra):
    extra, outs = rest[:n_extra], rest[n_extra:]
    qkv_ref = outs[0]
    row = 1 + pl.program_id(0) // (DEC_SEQ // ROW_TILE) if latent else 0
    h = _norm_mod(x_ref[...], g_ref[1:2, :], _mod_chunk(mod_ref, row, 3), _mod_chunk(mod_ref, row, 4))
    q = _dot(h, wq_ref[...]) * (HEAD_DIM ** -0.5)
    k = _dot(h, wk_ref[...])
    v = _dot(h, wv_ref[...])
    qkv_ref[:, 2 * D_MODEL:] = v.astype(BF16)
    if not latent:
        kc_ref, vc_ref = outs[1:]
        kc_ref[...] = k
        vc_ref[...] = v
        qkv_ref[:, :D_MODEL] = q.astype(BF16)
        qkv_ref[:, D_MODEL:2 * D_MODEL] = k.astype(BF16)
        return
    cos, sin_lo, sin_hi = (r[...] for r in extra)
    for c0 in range(0, D_MODEL, V_DIM):
        for src, base in ((q, 0), (k, D_MODEL)):
            t = src[:, c0:c0 + V_DIM]
            t = (t * cos + pltpu.roll(t, V_DIM - HEAD_DIM // 4, 1) * sin_lo
                 + pltpu.roll(t, HEAD_DIM // 4, 1) * sin_hi)
            qkv_ref[:, base + c0:base + c0 + V_DIM] = t.astype(BF16)


def _qkv(x, mod, g_norm, w_in, l, latent, rope=None, caches=None):
    rows, tile0 = (N_LAT, N_CTX // ROW_TILE) if latent else (N_CTX, 0)
    col0 = 2
    in_specs = [
        pl.BlockSpec((ROW_TILE, D_MODEL), lambda i: (tile0 + i, 0)),
        _resident((None, COND_ROWS, N_MOD * D_MODEL), lambda i: (l, 0, 0)),
        _resident((None, 3, D_MODEL), lambda i: (l, 0, 0)),
    ] + [_resident((None, D_MODEL, D_MODEL), functools.partial(lambda i, c: (l, 0, c), c=col0 + j))
         for j in range(3)]
    args = [x, mod, g_norm, w_in, w_in, w_in]
    out_shape = [jax.ShapeDtypeStruct((rows, 3 * D_MODEL), BF16)]
    out_specs = [pl.BlockSpec((ROW_TILE, 3 * D_MODEL), lambda i: (i, 0))]
    aliases = {}
    if latent:
        tiles = DEC_SEQ // ROW_TILE
        in_specs += [pl.BlockSpec((ROW_TILE, V_DIM), lambda i: (i % tiles, 0))] * 3
        args += list(rope)
    else:
        assert ROW_TILE == SEQ
        cache = jax.ShapeDtypeStruct((BATCH, DEPTH, SEQ, D_MODEL), F32)
        out_shape += [cache, cache]
        out_specs += [pl.BlockSpec((None, None, SEQ, D_MODEL), lambda i: (i, l, 0, 0))] * 2
        if caches is not None:
            in_specs += [pl.BlockSpec(memory_space=pl.ANY)] * 2
            args += list(caches)
            aliases = {len(args) - 2: 1, len(args) - 1: 2}
    return pl.pallas_call(
        functools.partial(_qkv_kernel, latent=latent, n_extra=len(args) - 6),
        grid=(rows // ROW_TILE,),
        in_specs=in_specs,
        out_specs=out_specs,
        out_shape=out_shape,
        input_output_aliases=aliases,
        compiler_params=_params(),
        name="qkv_latent" if latent else "qkv_context",
    )(*args)


def _lam_full(lam_ref, l):
    lp = lam_ref[l]
    s01 = jnp.sum(lp[0:1, :] * lp[1:2, :], axis=-1, keepdims=True)
    s23 = jnp.sum(lp[2:3, :] * lp[3:4, :], axis=-1, keepdims=True)
    return jnp.exp(s01) - jnp.exp(s23) + _lam_init(l)


def _lam_init(l):
    return 0.8 - 0.6 * math.exp(-0.3 * l)


def _attn_head(q, ks, vs, lam_full, gain, l):
    lane = lax.broadcasted_iota(jnp.int32, (1, V_DIM), 1)
    halves = []
    for keep in ((lane < HEAD_DIM), (lane >= HEAD_DIM)):
        qi = q * keep.astype(BF16)
        s = [_dot_nt(qi, k) for k in ks]
        m = functools.reduce(jnp.maximum, [jnp.max(t, axis=-1, keepdims=True) for t in s])
        e = [jnp.exp(t - m) for t in s]
        denom = functools.reduce(jnp.add, [jnp.sum(t, axis=-1, keepdims=True) for t in e])
        pv = functools.reduce(jnp.add, [_dot(t.astype(BF16), v) for t, v in zip(e, vs)])
        halves.append(pv / denom)
    o = halves[0] - lam_full * halves[1]
    return _rms(o) * gain * (1.0 - _lam_init(l))


def _attn_ctx_kernel(q_ref, k_ref, v_ref, lam_ref, gain_ref, o_ref, *, l):
    lam_full = _lam_full(lam_ref, l)
    gain = gain_ref[l:l + 1, :]
    for c0 in range(0, D_MODEL, V_DIM):
        cs = slice(c0, c0 + V_DIM)
        o = _attn_head(q_ref[:, cs], [k_ref[:, cs]], [v_ref[:, cs]], lam_full, gain, l)
        o_ref[:, cs] = o.astype(BF16)


def _attn_lat_kernel(q_ref, k_ref, v_ref, past_k, past_v, lam_ref, gain_ref, _, o_ref, *, l):
    lam_full = _lam_full(lam_ref, l)
    gain = gain_ref[l:l + 1, :]
    for c0 in range(0, D_MODEL, V_DIM):
        cs = slice(c0, c0 + V_DIM)
        ks = [past_k[:, cs].astype(BF16), k_ref[:, cs]]
        vs = [past_v[:, cs].astype(BF16), v_ref[:, cs]]
        o = _attn_head(q_ref[:, cs], ks, vs, lam_full, gain, l)
        o_ref[:, cs] = o.astype(BF16)


def _attn_ctx(qkv, lam, subln_gain, l):
    part = [pl.BlockSpec((SEQ, D_MODEL), functools.partial(lambda i, c: (i, c), c=c)) for c in range(3)]
    return pl.pallas_call(
        functools.partial(_attn_ctx_kernel, l=l),
        grid=(BATCH,),
        in_specs=part + [_resident((DEPTH, 4, HEAD_DIM), lambda i: (0, 0, 0)),
                         _resident((DEPTH, V_DIM), lambda i: (0, 0))],
        out_specs=pl.BlockSpec((SEQ, D_MODEL), lambda i: (i, 0)),
        out_shape=jax.ShapeDtypeStruct((N_TOK, D_MODEL), BF16),
        compiler_params=_params(),
        name="attn_context",
    )(qkv, qkv, qkv, lam, subln_gain)


def _attn_lat(qkv, past_k, past_v, lam, subln_gain, b_all, l):
    tiles = DEC_SEQ // Q_TILE
    qspec = pl.BlockSpec((Q_TILE, D_MODEL), lambda b, j: (b * tiles + j, 0))
    kspec = pl.BlockSpec((DEC_SEQ, D_MODEL), lambda b, j: (b, 1))
    vspec = pl.BlockSpec((DEC_SEQ, D_MODEL), lambda b, j: (b, 2))
    past = pl.BlockSpec((None, None, PAST_LEN, D_MODEL), lambda b, j: (b, l, 0, 0))
    return pl.pallas_call(
        functools.partial(_attn_lat_kernel, l=l),
        grid=(DEC_BATCH, tiles),
        in_specs=[qspec, kspec, vspec, past, past,
                  _resident((DEPTH, 4, HEAD_DIM), lambda b, j: (0, 0, 0)),
                  _resident((DEPTH, V_DIM), lambda b, j: (0, 0)),
                  pl.BlockSpec(memory_space=pl.ANY)],
        out_specs=pl.BlockSpec((Q_TILE, D_MODEL), lambda b, j: (N_CTX // Q_TILE + b * tiles + j, 0)),
        out_shape=jax.ShapeDtypeStruct((N_TOK, D_MODEL), BF16),
        input_output_aliases={7: 0},
        compiler_params=_params(),
        name="attn_latent",
    )(qkv, qkv, qkv, past_k, past_v, lam, subln_gain, b_all)


def _merge_kernel(x_ref, a_ref, b_ref, mod_ref, g_ref, wga_ref, wgb_ref, wa_ref, wb_ref, wo_ref, o_ref):
    row = _stacked_mod_row(pl.program_id(0))
    x = x_ref[...]
    h = _norm_mod(x, g_ref[1:2, :], _mod_chunk(mod_ref, row, 3), _mod_chunk(mod_ref, row, 4))
    merged = (jax.nn.sigmoid(_dot(h, wga_ref[...])) * _dot(a_ref[...].astype(F32), wa_ref[...])
              + jax.nn.sigmoid(_dot(h, wgb_ref[...])) * _dot(b_ref[...].astype(F32), wb_ref[...]))
    o_ref[...] = x + _mod_chunk(mod_ref, row, 5) * _dot(merged, wo_ref[...])


def _merge(x, a, b, mod, g_norm, w_in, w_branch_a, w_branch_b, w_out, l):
    tile = pl.BlockSpec((ROW_TILE, D_MODEL), lambda i: (i, 0))
    square = _resident((None, D_MODEL, D_MODEL), lambda i: (l, 0, 0))
    return pl.pallas_call(
        _merge_kernel,
        grid=(N_TOK // ROW_TILE,),
        in_specs=[tile, tile, tile,
                  _resident((None, COND_ROWS, N_MOD * D_MODEL), lambda i: (l, 0, 0)),
                  _resident((None, 3, D_MODEL), lambda i: (l, 0, 0)),
                  _resident((None, D_MODEL, D_MODEL), lambda i: (l, 0, 5)),
                  _resident((None, D_MODEL, D_MODEL), lambda i: (l, 0, 6)),
                  square, square, square],
        out_specs=tile,
        out_shape=jax.ShapeDtypeStruct((N_TOK, D_MODEL), F32),
        input_output_aliases={0: 0},
        compiler_params=_params(),
        name="merge",
    )(x, a, b, mod, g_norm, w_in, w_in, w_branch_a, w_branch_b, w_out)


def kernel(x_prompt, x_sample, cache_k, cache_v, c, c_ctx, w_mod, b_mod, g_norm, ffn1_w13, ffn1_w2, w_in,
           sgu_gain, w_spatial, b_spatial, lam, subln_gain, w_branch_a, w_branch_b, w_out, ffn2_w13,
           ffn2_w2, g_final):
    cond = jnp.concatenate(
        [c_ctx[None, :], c, jnp.zeros((COND_ROWS - 1 - DEC_BATCH, D_MODEL), F32)], axis=0)
    mod = _modulation(cond, w_mod, b_mod)
    bias_tile = jnp.repeat(jnp.swapaxes(b_spatial, 1, 2), GROUP_W, axis=2)
    rope = _rope_tables()
    past_k = cache_k.reshape(DEC_BATCH, DEPTH, PAST_LEN, D_MODEL)
    past_v = cache_v.reshape(DEC_BATCH, DEPTH, PAST_LEN, D_MODEL)

    x = (x_prompt.reshape(N_CTX, D_MODEL), x_sample.reshape(N_LAT, D_MODEL))
    caches = None
    for l in range(DEPTH):
        x = _ffn(x, mod, g_norm, ffn1_w13, ffn1_w2, None, l, 0)
        a = _sgu(x, mod, g_norm, w_in, sgu_gain, w_spatial, bias_tile, l)
        qkv_ctx, *caches = _qkv(x, mod, g_norm, w_in, l, latent=False, caches=caches)
        qkv_lat, = _qkv(x, mod, g_norm, w_in, l, latent=True, rope=rope)
        b = _attn_ctx(qkv_ctx, lam, subln_gain, l)
        b = _attn_lat(qkv_lat, past_k, past_v, lam, subln_gain, b, l)
        x = _merge(x, a, b, mod, g_norm, w_in, w_branch_a, w_branch_b, w_out, l)
        x = _ffn(x, mod, g_norm, ffn2_w13, ffn2_w2, g_final if l == DEPTH - 1 else None, l, 2)

    y_ctx, y_lat = x
    new_k, new_v = caches
    return (y_ctx.reshape(BATCH, SEQ, D_MODEL),
            y_lat.reshape(DEC_BATCH, DEC_SEQ, D_MODEL),
            new_k.reshape(BATCH, DEPTH, SEQ, N_HEADS, 2 * HEAD_DIM),
            new_v.reshape(BATCH, DEPTH, SEQ, N_HEADS, V_DIM))
```

```python
import functools
import math

import numpy as np
import jax
import jax.numpy as jnp
from jax import lax
from jax.experimental import pallas as pl
from jax.experimental.pallas import tpu as pltpu

D_MODEL = 1024
BATCH = 16
SEQ = 256
DEPTH = 4
DEC_BATCH = 2
DEC_SEQ = 1024
PAST_LEN = 256
GRID_W = 64
CHUNK = 128
N_GROUPS_A = 4
GROUP_W = D_MODEL // N_GROUPS_A
N_HEADS = 8
HEAD_DIM = D_MODEL // (2 * N_HEADS)
V_DIM = 2 * HEAD_DIM
D_FF = ((8 * D_MODEL // 3 + 127) // 128) * 128
IN_W = 7 * D_MODEL
N_MOD = 9
ROPE_THETA = 10000.0
EPS = 1e-6

N_CTX = BATCH * SEQ
N_LAT = DEC_BATCH * DEC_SEQ
N_TOK = N_CTX + N_LAT
COND_ROWS = 8

ROW_TILE = 512
SGU_SUB = 256
FFN_TILE = 512
FF_CHUNK = 256
MOD_TILE = 2304
Q_TILE = 512
V7X_VMEM_BYTES = 64 * 1024 * 1024
VMEM_LIMIT_BYTES = V7X_VMEM_BYTES - 4 * 1024 * 1024

F32 = jnp.float32
BF16 = jnp.bfloat16


def _params():
    return pltpu.CompilerParams(vmem_limit_bytes=VMEM_LIMIT_BYTES)


def _resident(block_shape, index_map):
    return pl.BlockSpec(block_shape, index_map, pipeline_mode=pl.Buffered(1))


def _dot(a, b):
    return jnp.dot(a, b, preferred_element_type=F32)


def _dot_nt(a, b):
    return lax.dot_general(a, b, (((1,), (1,)), ((), ())), preferred_element_type=F32)


def _rms(x):
    return x * lax.rsqrt(jnp.mean(x * x, axis=-1, keepdims=True) + EPS)


def _norm_mod(x, g, shift, scale):
    return _rms(x) * g * (1.0 + scale) + shift


def _stacked_mod_row(i, tile=ROW_TILE):
    n_ctx_tiles = N_CTX // tile
    return jnp.where(i < n_ctx_tiles, 0, 1 + (i - n_ctx_tiles) // (DEC_SEQ // tile))


def _mod_chunk(mod_ref, row, k):
    return mod_ref[pl.ds(row, 1), k * D_MODEL:(k + 1) * D_MODEL]


def _mod_kernel(c_ref, w_ref, b_ref, o_ref):
    c = c_ref[...]
    o_ref[...] = _dot(c * jax.nn.sigmoid(c), w_ref[...]) + b_ref[...]


def _modulation(cond, w_mod, b_mod):
    n = N_MOD * D_MODEL
    return pl.pallas_call(
        _mod_kernel,
        grid=(DEPTH, n // MOD_TILE),
        in_specs=[
            pl.BlockSpec((COND_ROWS, D_MODEL), lambda l, j: (0, 0)),
            pl.BlockSpec((None, D_MODEL, MOD_TILE), lambda l, j: (l, 0, j)),
            pl.BlockSpec((None, 1, MOD_TILE), lambda l, j: (l, 0, j)),
        ],
        out_specs=pl.BlockSpec((None, COND_ROWS, MOD_TILE), lambda l, j: (l, 0, j)),
        out_shape=jax.ShapeDtypeStruct((DEPTH, COND_ROWS, n), F32),
        compiler_params=_params(),
        name="modulation",
    )(cond, w_mod, b_mod.reshape(DEPTH, 1, n))


def _ffn_kernel(*refs, sub, split_in, final):
    n_x = 2 if split_in else 1
    x_refs, (mod_ref, g_ref, w13_ref, w2_ref), rest = refs[:n_x], refs[n_x:n_x + 4], refs[n_x + 4:]
    i = pl.program_id(0)
    is_ctx = i < N_CTX // FFN_TILE
    row = _stacked_mod_row(i, FFN_TILE)
    x = jnp.where(is_ctx, x_refs[0][...], x_refs[1][...]) if split_in else x_refs[0][...]
    h = _norm_mod(x, g_ref[sub:sub + 1, :], _mod_chunk(mod_ref, row, 3 * sub),
                  _mod_chunk(mod_ref, row, 3 * sub + 1))

    def gate_up(lo):
        return _dot(h, w13_ref[:, lo:lo + FF_CHUNK]), _dot(h, w13_ref[:, D_FF + lo:D_FF + lo + FF_CHUNK])

    acc = jnp.zeros((FFN_TILE, D_MODEL), F32)
    nxt = gate_up(0)
    for lo in range(0, D_FF, FF_CHUNK):
        gate, up = nxt
        if lo + FF_CHUNK < D_FF:
            nxt = gate_up(lo + FF_CHUNK)
        acc = acc + _dot(gate * jax.nn.sigmoid(gate) * up, w2_ref[lo:lo + FF_CHUNK, :])
    y = x + 0.5 * _mod_chunk(mod_ref, row, 3 * sub + 2) * acc
    if not final:
        rest[0][...] = y
        return
    gf_ref, ctx_ref, lat_ref = rest
    y = _rms(y) * gf_ref[...]

    @pl.when(is_ctx)
    def _():
        ctx_ref[...] = y

    @pl.when(jnp.logical_not(is_ctx))
    def _():
        lat_ref[...] = y


def _ffn(xs, mod, g_norm, w13, w2, g_final, l, sub):
    split_in, final = isinstance(xs, tuple), g_final is not None
    n_ctx_tiles = N_CTX // FFN_TILE
    tile = pl.BlockSpec((FFN_TILE, D_MODEL), lambda i: (i, 0))
    ctx_tile = pl.BlockSpec((FFN_TILE, D_MODEL), lambda i: (jnp.minimum(i, n_ctx_tiles - 1), 0))
    lat_tile = pl.BlockSpec((FFN_TILE, D_MODEL), lambda i: (jnp.maximum(i - n_ctx_tiles, 0), 0))
    in_specs = ([ctx_tile, lat_tile] if split_in else [tile]) + [
        _resident((None, COND_ROWS, N_MOD * D_MODEL), lambda i: (l, 0, 0)),
        _resident((None, 3, D_MODEL), lambda i: (l, 0, 0)),
        _resident((None, D_MODEL, 2 * D_FF), lambda i: (l, 0, 0)),
        _resident((None, D_FF, D_MODEL), lambda i: (l, 0, 0)),
    ]
    args = (list(xs) if split_in else [xs]) + [mod, g_norm, w13, w2]
    if final:
        in_specs.append(_resident((1, D_MODEL), lambda i: (0, 0)))
        args.append(g_final.reshape(1, D_MODEL))
        out_specs = [ctx_tile, lat_tile]
        out_shape = [jax.ShapeDtypeStruct((N_CTX, D_MODEL), F32), jax.ShapeDtypeStruct((N_LAT, D_MODEL), F32)]
    else:
        out_specs, out_shape = tile, jax.ShapeDtypeStruct((N_TOK, D_MODEL), F32)
    return pl.pallas_call(
        functools.partial(_ffn_kernel, sub=sub, split_in=split_in, final=final),
        grid=(N_TOK // FFN_TILE,),
        in_specs=in_specs,
        out_specs=out_specs,
        out_shape=out_shape,
        input_output_aliases={} if split_in or final else {0: 0},
        compiler_params=_params(),
        name="ffn",
    )(*args)


def _gelu_tanh(x):
    return x * (0.5 * (1.0 + jnp.tanh(math.sqrt(2.0 / math.pi) * (x + 0.044715 * (x * x * x)))))


def _sgu_kernel(x_ref, mod_ref, g_ref, wz_ref, gain_ref, ws_ref, bias_ref, o_ref, *, l):
    row = _stacked_mod_row(pl.program_id(0))

    def gated(r0):
        h = _norm_mod(x_ref[r0:r0 + SGU_SUB, :], g_ref[1:2, :], _mod_chunk(mod_ref, row, 3),
                      _mod_chunk(mod_ref, row, 4))
        return _gelu_tanh(_dot(h, wz_ref[...]))

    def mix(r0, z):
        u = z[:, :D_MODEL]
        v = _rms(z[:, D_MODEL:]) * gain_ref[l:l + 1, :]
        for c in range(0, SGU_SUB, CHUNK):
            for g in range(N_GROUPS_A):
                c0 = g * GROUP_W
                mixed = _dot(ws_ref[l, g], v[c:c + CHUNK, c0:c0 + GROUP_W]) + bias_ref[:, c0:c0 + GROUP_W]
                o_ref[r0 + c:r0 + c + CHUNK, c0:c0 + GROUP_W] = (
                    u[c:c + CHUNK, c0:c0 + GROUP_W] * mixed).astype(BF16)

    starts = list(range(0, ROW_TILE, SGU_SUB))
    nxt = gated(starts[0])
    for i, r0 in enumerate(starts):
        z = nxt
        if i + 1 < len(starts):
            nxt = gated(starts[i + 1])
        mix(r0, z)


def _sgu(x, mod, g_norm, w_in, sgu_gain, w_spatial, bias_tile, l):
    return pl.pallas_call(
        functools.partial(_sgu_kernel, l=l),
        grid=(N_TOK // ROW_TILE,),
        in_specs=[
            pl.BlockSpec((ROW_TILE, D_MODEL), lambda i: (i, 0)),
            _resident((None, COND_ROWS, N_MOD * D_MODEL), lambda i: (l, 0, 0)),
            _resident((None, 3, D_MODEL), lambda i: (l, 0, 0)),
            _resident((None, D_MODEL, 2 * D_MODEL), lambda i: (l, 0, 0)),
            _resident((DEPTH, D_MODEL), lambda i: (0, 0)),
            _resident((DEPTH, N_GROUPS_A, CHUNK, CHUNK), lambda i: (0, 0, 0, 0)),
            _resident((None, CHUNK, D_MODEL), lambda i: (l, 0, 0)),
        ],
        out_specs=pl.BlockSpec((ROW_TILE, D_MODEL), lambda i: (i, 0)),
        out_shape=jax.ShapeDtypeStruct((N_TOK, D_MODEL), BF16),
        compiler_params=_params(),
        name="sgu",
    )(x, mod, g_norm, w_in, sgu_gain, w_spatial, bias_tile)


def _rope_tables():
    pos = np.arange(DEC_SEQ)
    lane = np.arange(V_DIM)
    half = HEAD_DIM // 2
    within = lane % half
    coord = np.where((lane % HEAD_DIM) < half, pos[:, None] // GRID_W, pos[:, None] % GRID_W)
    freqs = ROPE_THETA ** (-np.arange(0, half, 2, dtype=np.float64) / half)
    ang = coord * freqs[within % (half // 2)][None, :]
    first = (within < half // 2)[None, :]
    cos = np.cos(ang)
    sin_lo = np.where(first, -np.sin(ang), 0.0)
    sin_hi = np.where(first, 0.0, np.sin(ang))
    return tuple(jnp.asarray(t, dtype=F32) for t in (cos, sin_lo, sin_hi))


def _qkv_kernel(x_ref, mod_ref, g_ref, wq_ref, wk_ref, wv_ref, *rest, latent, n_extra):
    extra, outs = rest[:n_extra], rest[n_extra:]
    qkv_ref = outs[0]
    row = 1 + pl.program_id(0) // (DEC_SEQ // ROW_TILE) if latent else 0
    h = _norm_mod(x_ref[...], g_ref[1:2, :], _mod_chunk(mod_ref, row, 3), _mod_chunk(mod_ref, row, 4))
    q = _dot(h, wq_ref[...]) * (HEAD_DIM ** -0.5)
    k = _dot(h, wk_ref[...])
    v = _dot(h, wv_ref[...])
    qkv_ref[:, 2 * D_MODEL:] = v.astype(BF16)
    if not latent:
        kc_ref, vc_ref = outs[1:]
        kc_ref[...] = k.reshape(kc_ref.shape)
        vc_ref[...] = v.reshape(vc_ref.shape)
        qkv_ref[:, :D_MODEL] = q.astype(BF16)
        qkv_ref[:, D_MODEL:2 * D_MODEL] = k.astype(BF16)
        return
    cos, sin_lo, sin_hi = (r[...] for r in extra)
    for c0 in range(0, D_MODEL, V_DIM):
        for src, base in ((q, 0), (k, D_MODEL)):
            t = src[:, c0:c0 + V_DIM]
            t = (t * cos + pltpu.roll(t, V_DIM - HEAD_DIM // 4, 1) * sin_lo
                 + pltpu.roll(t, HEAD_DIM // 4, 1) * sin_hi)
            qkv_ref[:, base + c0:base + c0 + V_DIM] = t.astype(BF16)


def _qkv(x, mod, g_norm, w_in, l, latent, rope=None, caches=None):
    rows, tile0 = (N_LAT, N_CTX // ROW_TILE) if latent else (N_CTX, 0)
    col0 = 2
    in_specs = [
        pl.BlockSpec((ROW_TILE, D_MODEL), lambda i: (tile0 + i, 0)),
        _resident((None, COND_ROWS, N_MOD * D_MODEL), lambda i: (l, 0, 0)),
        _resident((None, 3, D_MODEL), lambda i: (l, 0, 0)),
    ] + [_resident((None, D_MODEL, D_MODEL), functools.partial(lambda i, c: (l, 0, c), c=col0 + j))
         for j in range(3)]
    args = [x, mod, g_norm, w_in, w_in, w_in]
    out_shape = [jax.ShapeDtypeStruct((rows, 3 * D_MODEL), BF16)]
    out_specs = [pl.BlockSpec((ROW_TILE, 3 * D_MODEL), lambda i: (i, 0))]
    aliases = {}
    if latent:
        tiles = DEC_SEQ // ROW_TILE
        in_specs += [pl.BlockSpec((ROW_TILE, V_DIM), lambda i: (i % tiles, 0))] * 3
        args += list(rope)
    else:
        seqs = ROW_TILE // SEQ
        cache = jax.ShapeDtypeStruct((BATCH, DEPTH, SEQ, D_MODEL), F32)
        out_shape += [cache, cache]
        out_specs += [pl.BlockSpec((seqs, None, SEQ, D_MODEL), lambda i: (i, l, 0, 0))] * 2
        if caches is not None:
            in_specs += [pl.BlockSpec(memory_space=pl.ANY)] * 2
            args += list(caches)
            aliases = {len(args) - 2: 1, len(args) - 1: 2}
    return pl.pallas_call(
        functools.partial(_qkv_kernel, latent=latent, n_extra=len(args) - 6),
        grid=(rows // ROW_TILE,),
        in_specs=in_specs,
        out_specs=out_specs,
        out_shape=out_shape,
        input_output_aliases=aliases,
        compiler_params=_params(),
        name="qkv_latent" if latent else "qkv_context",
    )(*args)


def _lam_full(lam_ref, l):
    lp = lam_ref[l]
    s01 = jnp.sum(lp[0:1, :] * lp[1:2, :], axis=-1, keepdims=True)
    s23 = jnp.sum(lp[2:3, :] * lp[3:4, :], axis=-1, keepdims=True)
    return jnp.exp(s01) - jnp.exp(s23) + _lam_init(l)


def _lam_init(l):
    return 0.8 - 0.6 * math.exp(-0.3 * l)


def _attn_scores(q, ks):
    lane = lax.broadcasted_iota(jnp.int32, (1, V_DIM), 1)
    lo = (lane < HEAD_DIM).astype(BF16)
    qs = jnp.concatenate([q * lo, q * (1 - lo)], axis=0)
    return [_dot_nt(qs, k) for k in ks]


def _attn_combine(s, vs, lam_full, gain, l):
    n_q = s[0].shape[0] // 2
    m = functools.reduce(jnp.maximum, [jnp.max(t, axis=-1, keepdims=True) for t in s])
    pv = functools.reduce(jnp.add, [
        _dot(jnp.exp(t - m).astype(BF16), jnp.concatenate([v, jnp.ones_like(v)], axis=1))
        for t, v in zip(s, vs)])
    pv = pv[:, :V_DIM] / pv[:, V_DIM:]
    o = pv[:n_q] - lam_full * pv[n_q:]
    return _rms(o) * gain * (1.0 - _lam_init(l))


def _attn_heads(q_ref, k_refs, v_refs, lam_ref, gain_ref, o_ref, l):
    lam_full = _lam_full(lam_ref, l)
    gain = gain_ref[l:l + 1, :]

    def scores(hd):
        cs = slice(hd * V_DIM, (hd + 1) * V_DIM)
        return _attn_scores(q_ref[:, cs], [r[:, cs].astype(BF16) for r in k_refs])

    nxt = scores(0)
    for hd in range(N_HEADS):
        cs = slice(hd * V_DIM, (hd + 1) * V_DIM)
        s = nxt
        if hd + 1 < N_HEADS:
            nxt = scores(hd + 1)
        o = _attn_combine(s, [r[:, cs].astype(BF16) for r in v_refs], lam_full, gain, l)
        o_ref[:, cs] = o.astype(BF16)


def _attn_ctx_kernel(q_ref, k_ref, v_ref, lam_ref, gain_ref, o_ref, *, l):
    _attn_heads(q_ref, [k_ref], [v_ref], lam_ref, gain_ref, o_ref, l)


def _attn_lat_kernel(q_ref, k_ref, v_ref, past_k, past_v, lam_ref, gain_ref, _, o_ref, *, l):
    _attn_heads(q_ref, [past_k, k_ref], [past_v, v_ref], lam_ref, gain_ref, o_ref, l)


def _attn_ctx(qkv, lam, subln_gain, l):
    part = [pl.BlockSpec((SEQ, D_MODEL), functools.partial(lambda i, c: (i, c), c=c)) for c in range(3)]
    return pl.pallas_call(
        functools.partial(_attn_ctx_kernel, l=l),
        grid=(BATCH,),
        in_specs=part + [_resident((DEPTH, 4, HEAD_DIM), lambda i: (0, 0, 0)),
                         _resident((DEPTH, V_DIM), lambda i: (0, 0))],
        out_specs=pl.BlockSpec((SEQ, D_MODEL), lambda i: (i, 0)),
        out_shape=jax.ShapeDtypeStruct((N_TOK, D_MODEL), BF16),
        compiler_params=_params(),
        name="attn_context",
    )(qkv, qkv, qkv, lam, subln_gain)


def _attn_lat(qkv, past_k, past_v, lam, subln_gain, b_all, l):
    tiles = DEC_SEQ // Q_TILE
    qspec = pl.BlockSpec((Q_TILE, D_MODEL), lambda b, j: (b * tiles + j, 0))
    kspec = pl.BlockSpec((DEC_SEQ, D_MODEL), lambda b, j: (b, 1))
    vspec = pl.BlockSpec((DEC_SEQ, D_MODEL), lambda b, j: (b, 2))
    past = pl.BlockSpec((None, None, PAST_LEN, D_MODEL), lambda b, j: (b, l, 0, 0))
    return pl.pallas_call(
        functools.partial(_attn_lat_kernel, l=l),
        grid=(DEC_BATCH, tiles),
        in_specs=[qspec, kspec, vspec, past, past,
                  _resident((DEPTH, 4, HEAD_DIM), lambda b, j: (0, 0, 0)),
                  _resident((DEPTH, V_DIM), lambda b, j: (0, 0)),
                  pl.BlockSpec(memory_space=pl.ANY)],
        out_specs=pl.BlockSpec((Q_TILE, D_MODEL), lambda b, j: (N_CTX // Q_TILE + b * tiles + j, 0)),
        out_shape=jax.ShapeDtypeStruct((N_TOK, D_MODEL), BF16),
        input_output_aliases={7: 0},
        compiler_params=_params(),
        name="attn_latent",
    )(qkv, qkv, qkv, past_k, past_v, lam, subln_gain, b_all)


def _merge_kernel(x_ref, a_ref, b_ref, mod_ref, g_ref, wga_ref, wgb_ref, wa_ref, wb_ref, wo_ref, o_ref):
    row = _stacked_mod_row(pl.program_id(0))
    x = x_ref[...]
    h = _norm_mod(x, g_ref[1:2, :], _mod_chunk(mod_ref, row, 3), _mod_chunk(mod_ref, row, 4))
    merged = (jax.nn.sigmoid(_dot(h, wga_ref[...])) * _dot(a_ref[...].astype(F32), wa_ref[...])
              + jax.nn.sigmoid(_dot(h, wgb_ref[...])) * _dot(b_ref[...].astype(F32), wb_ref[...]))
    o_ref[...] = x + _mod_chunk(mod_ref, row, 5) * _dot(merged, wo_ref[...])


def _merge(x, a, b, mod, g_norm, w_in, w_branch_a, w_branch_b, w_out, l):
    tile = pl.BlockSpec((ROW_TILE, D_MODEL), lambda i: (i, 0))
    square = _resident((None, D_MODEL, D_MODEL), lambda i: (l, 0, 0))
    return pl.pallas_call(
        _merge_kernel,
        grid=(N_TOK // ROW_TILE,),
        in_specs=[tile, tile, tile,
                  _resident((None, COND_ROWS, N_MOD * D_MODEL), lambda i: (l, 0, 0)),
                  _resident((None, 3, D_MODEL), lambda i: (l, 0, 0)),
                  _resident((None, D_MODEL, D_MODEL), lambda i: (l, 0, 5)),
                  _resident((None, D_MODEL, D_MODEL), lambda i: (l, 0, 6)),
                  square, square, square],
        out_specs=tile,
        out_shape=jax.ShapeDtypeStruct((N_TOK, D_MODEL), F32),
        input_output_aliases={0: 0},
        compiler_params=_params(),
        name="merge",
    )(x, a, b, mod, g_norm, w_in, w_in, w_branch_a, w_branch_b, w_out)


def kernel(x_prompt, x_sample, cache_k, cache_v, c, c_ctx, w_mod, b_mod, g_norm, ffn1_w13, ffn1_w2, w_in,
           sgu_gain, w_spatial, b_spatial, lam, subln_gain, w_branch_a, w_branch_b, w_out, ffn2_w13,
           ffn2_w2, g_final):
    cond = jnp.concatenate(
        [c_ctx[None, :], c, jnp.zeros((COND_ROWS - 1 - DEC_BATCH, D_MODEL), F32)], axis=0)
    mod = _modulation(cond, w_mod, b_mod)
    bias_tile = jnp.repeat(jnp.swapaxes(b_spatial, 1, 2), GROUP_W, axis=2)
    rope = _rope_tables()
    past_k = cache_k.reshape(DEC_BATCH, DEPTH, PAST_LEN, D_MODEL)
    past_v = cache_v.reshape(DEC_BATCH, DEPTH, PAST_LEN, D_MODEL)

    x = (x_prompt.reshape(N_CTX, D_MODEL), x_sample.reshape(N_LAT, D_MODEL))
    caches = None
    for l in range(DEPTH):
        x = _ffn(x, mod, g_norm, ffn1_w13, ffn1_w2, None, l, 0)
        a = _sgu(x, mod, g_norm, w_in, sgu_gain, w_spatial, bias_tile, l)
        qkv_ctx, *caches = _qkv(x, mod, g_norm, w_in, l, latent=False, caches=caches)
        qkv_lat, = _qkv(x, mod, g_norm, w_in, l, latent=True, rope=rope)
        b = _attn_ctx(qkv_ctx, lam, subln_gain, l)
        b = _attn_lat(qkv_lat, past_k, past_v, lam, subln_gain, b, l)
        x = _merge(x, a, b, mod, g_norm, w_in, w_branch_a, w_branch_b, w_out, l)
        x = _ffn(x, mod, g_norm, ffn2_w13, ffn2_w2, g_final if l == DEPTH - 1 else None, l, 2)

    y_ctx, y_lat = x
    new_k, new_v = caches
    return (y_ctx.reshape(BATCH, SEQ, D_MODEL),
            y_lat.reshape(DEC_BATCH, DEC_SEQ, D_MODEL),
            new_k.reshape(BATCH, DEPTH, SEQ, N_HEADS, 2 * HEAD_DIM),
            new_v.reshape(BATCH, DEPTH, SEQ, N_HEADS, V_DIM))
```

```python
import functools
import math

import numpy as np
import jax
import jax.numpy as jnp
from jax import lax
from jax.experimental import pallas as pl
from jax.experimental.pallas import tpu as pltpu

D_MODEL = 1024
BATCH = 16
SEQ = 256
DEPTH = 4
DEC_BATCH = 2
DEC_SEQ = 1024
PAST_LEN = 256
GRID_W = 64
CHUNK = 128
N_GROUPS_A = 4
GROUP_W = D_MODEL // N_GROUPS_A
N_HEADS = 8
HEAD_DIM = D_MODEL // (2 * N_HEADS)
V_DIM = 2 * HEAD_DIM
D_FF = ((8 * D_MODEL // 3 + 127) // 128) * 128
IN_W = 7 * D_MODEL
N_MOD = 9
ROPE_THETA = 10000.0
EPS = 1e-6

N_CTX = BATCH * SEQ
N_LAT = DEC_BATCH * DEC_SEQ
N_TOK = N_CTX + N_LAT
COND_ROWS = 8

ROW_TILE = 512
SGU_SUB = 256
FFN_TILE = 512
FF_CHUNK = 256
MOD_TILE = 2304
Q_TILE = 512
V7X_VMEM_BYTES = 64 * 1024 * 1024
VMEM_LIMIT_BYTES = V7X_VMEM_BYTES - 4 * 1024 * 1024

F32 = jnp.float32
BF16 = jnp.bfloat16


def _params():
    return pltpu.CompilerParams(vmem_limit_bytes=VMEM_LIMIT_BYTES)


def _resident(block_shape, index_map):
    return pl.BlockSpec(block_shape, index_map, pipeline_mode=pl.Buffered(1))


def _dot(a, b):
    return jnp.dot(a, b, preferred_element_type=F32)


def _dot_nt(a, b):
    return lax.dot_general(a, b, (((1,), (1,)), ((), ())), preferred_element_type=F32)


def _rms(x):
    return x * lax.rsqrt(jnp.mean(x * x, axis=-1, keepdims=True) + EPS)


def _norm_mod(x, g, shift, scale):
    return _rms(x) * g * (1.0 + scale) + shift


def _stacked_mod_row(i, tile=ROW_TILE):
    n_ctx_tiles = N_CTX // tile
    return jnp.where(i < n_ctx_tiles, 0, 1 + (i - n_ctx_tiles) // (DEC_SEQ // tile))


def _mod_chunk(mod_ref, row, k):
    return mod_ref[pl.ds(row, 1), k * D_MODEL:(k + 1) * D_MODEL]


def _mod_kernel(c_ref, w_ref, b_ref, o_ref):
    c = c_ref[...]
    o_ref[...] = _dot(c * jax.nn.sigmoid(c), w_ref[...]) + b_ref[...]


def _modulation(cond, w_mod, b_mod):
    n = N_MOD * D_MODEL
    return pl.pallas_call(
        _mod_kernel,
        grid=(DEPTH, n // MOD_TILE),
        in_specs=[
            pl.BlockSpec((COND_ROWS, D_MODEL), lambda l, j: (0, 0)),
            pl.BlockSpec((None, D_MODEL, MOD_TILE), lambda l, j: (l, 0, j)),
            pl.BlockSpec((None, 1, MOD_TILE), lambda l, j: (l, 0, j)),
        ],
        out_specs=pl.BlockSpec((None, COND_ROWS, MOD_TILE), lambda l, j: (l, 0, j)),
        out_shape=jax.ShapeDtypeStruct((DEPTH, COND_ROWS, n), F32),
        compiler_params=_params(),
        name="modulation",
    )(cond, w_mod, b_mod.reshape(DEPTH, 1, n))


def _ffn_weight_copies(w13_hbm, w2_hbm, w13_ref, w2_ref, sems, l, lo):
    c = lo // FF_CHUNK
    cols = pl.ds(lo, FF_CHUNK)
    up_cols = pl.ds(D_FF + lo, FF_CHUNK)
    return (pltpu.make_async_copy(w13_hbm.at[l, :, cols], w13_ref.at[:, cols], sems.at[c, 0]),
            pltpu.make_async_copy(w13_hbm.at[l, :, up_cols], w13_ref.at[:, up_cols], sems.at[c, 1]),
            pltpu.make_async_copy(w2_hbm.at[l, cols, :], w2_ref.at[cols, :], sems.at[c, 2]))


def _ffn_kernel(*refs, l, sub, split_in, final):
    n_x = 2 if split_in else 1
    x_refs, (mod_ref, g_ref, w13_hbm, w2_hbm) = refs[:n_x], refs[n_x:n_x + 4]
    *rest, w13_ref, w2_ref, sems = refs[n_x + 4:]
    i = pl.program_id(0)
    is_ctx = i < N_CTX // FFN_TILE
    row = _stacked_mod_row(i, FFN_TILE)
    chunks = range(0, D_FF, FF_CHUNK)

    def body(first_step):
        def arrive(lo):
            if first_step:
                for cp in _ffn_weight_copies(w13_hbm, w2_hbm, w13_ref, w2_ref, sems, l, lo):
                    cp.wait()

        if first_step:
            for lo in chunks:
                for cp in _ffn_weight_copies(w13_hbm, w2_hbm, w13_ref, w2_ref, sems, l, lo):
                    cp.start()
        x = jnp.where(is_ctx, x_refs[0][...], x_refs[1][...]) if split_in else x_refs[0][...]
        h = _norm_mod(x, g_ref[sub:sub + 1, :], _mod_chunk(mod_ref, row, 3 * sub),
                      _mod_chunk(mod_ref, row, 3 * sub + 1))

        def gate_up(lo):
            arrive(lo)
            return (_dot(h, w13_ref[:, lo:lo + FF_CHUNK]),
                    _dot(h, w13_ref[:, D_FF + lo:D_FF + lo + FF_CHUNK]))

        acc = jnp.zeros((FFN_TILE, D_MODEL), F32)
        nxt = gate_up(0)
        for lo in chunks:
            gate, up = nxt
            if lo + FF_CHUNK < D_FF:
                nxt = gate_up(lo + FF_CHUNK)
            acc = acc + _dot(gate * jax.nn.sigmoid(gate) * up, w2_ref[lo:lo + FF_CHUNK, :])
        y = x + 0.5 * _mod_chunk(mod_ref, row, 3 * sub + 2) * acc
        if not final:
            rest[0][...] = y
            return
        gf_ref, ctx_ref, lat_ref = rest
        y = _rms(y) * gf_ref[...]

        @pl.when(is_ctx)
        def _():
            ctx_ref[...] = y

        @pl.when(jnp.logical_not(is_ctx))
        def _():
            lat_ref[...] = y

    lax.cond(i == 0, lambda: body(True), lambda: body(False))


def _ffn(xs, mod, g_norm, w13, w2, g_final, l, sub):
    split_in, final = isinstance(xs, tuple), g_final is not None
    n_ctx_tiles = N_CTX // FFN_TILE
    tile = pl.BlockSpec((FFN_TILE, D_MODEL), lambda i: (i, 0))
    ctx_tile = pl.BlockSpec((FFN_TILE, D_MODEL), lambda i: (jnp.minimum(i, n_ctx_tiles - 1), 0))
    lat_tile = pl.BlockSpec((FFN_TILE, D_MODEL), lambda i: (jnp.maximum(i - n_ctx_tiles, 0), 0))
    in_specs = ([ctx_tile, lat_tile] if split_in else [tile]) + [
        _resident((None, COND_ROWS, N_MOD * D_MODEL), lambda i: (l, 0, 0)),
        _resident((None, 3, D_MODEL), lambda i: (l, 0, 0)),
        pl.BlockSpec(memory_space=pl.ANY),
        pl.BlockSpec(memory_space=pl.ANY),
    ]
    args = (list(xs) if split_in else [xs]) + [mod, g_norm, w13, w2]
    if final:
        in_specs.append(_resident((1, D_MODEL), lambda i: (0, 0)))
        args.append(g_final.reshape(1, D_MODEL))
        out_specs = [ctx_tile, lat_tile]
        out_shape = [jax.ShapeDtypeStruct((N_CTX, D_MODEL), F32), jax.ShapeDtypeStruct((N_LAT, D_MODEL), F32)]
    else:
        out_specs, out_shape = tile, jax.ShapeDtypeStruct((N_TOK, D_MODEL), F32)
    return pl.pallas_call(
        functools.partial(_ffn_kernel, l=l, sub=sub, split_in=split_in, final=final),
        grid=(N_TOK // FFN_TILE,),
        in_specs=in_specs,
        out_specs=out_specs,
        out_shape=out_shape,
        scratch_shapes=[pltpu.VMEM((D_MODEL, 2 * D_FF), F32), pltpu.VMEM((D_FF, D_MODEL), F32),
                        pltpu.SemaphoreType.DMA((D_FF // FF_CHUNK, 3))],
        input_output_aliases={} if split_in or final else {0: 0},
        compiler_params=_params(),
        name="ffn",
    )(*args)


def _gelu_tanh(x):
    return x * (0.5 * (1.0 + jnp.tanh(math.sqrt(2.0 / math.pi) * (x + 0.044715 * (x * x * x)))))


def _sgu_kernel(x_ref, mod_ref, g_ref, wz_ref, gain_ref, ws_ref, bias_ref, o_ref, *, l):
    row = _stacked_mod_row(pl.program_id(0))

    def gated(r0):
        h = _norm_mod(x_ref[r0:r0 + SGU_SUB, :], g_ref[1:2, :], _mod_chunk(mod_ref, row, 3),
                      _mod_chunk(mod_ref, row, 4))
        return _gelu_tanh(_dot(h, wz_ref[...]))

    def mix(r0, z):
        u = z[:, :D_MODEL]
        v = _rms(z[:, D_MODEL:]) * gain_ref[l:l + 1, :]
        for c in range(0, SGU_SUB, CHUNK):
            for g in range(N_GROUPS_A):
                c0 = g * GROUP_W
                mixed = _dot(ws_ref[l, g], v[c:c + CHUNK, c0:c0 + GROUP_W]) + bias_ref[:, c0:c0 + GROUP_W]
                o_ref[r0 + c:r0 + c + CHUNK, c0:c0 + GROUP_W] = (
                    u[c:c + CHUNK, c0:c0 + GROUP_W] * mixed).astype(BF16)

    starts = list(range(0, ROW_TILE, SGU_SUB))
    nxt = gated(starts[0])
    for i, r0 in enumerate(starts):
        z = nxt
        if i + 1 < len(starts):
            nxt = gated(starts[i + 1])
        mix(r0, z)


def _sgu(x, mod, g_norm, w_in, sgu_gain, w_spatial, bias_tile, l):
    return pl.pallas_call(
        functools.partial(_sgu_kernel, l=l),
        grid=(N_TOK // ROW_TILE,),
        in_specs=[
            pl.BlockSpec((ROW_TILE, D_MODEL), lambda i: (i, 0)),
            _resident((None, COND_ROWS, N_MOD * D_MODEL), lambda i: (l, 0, 0)),
            _resident((None, 3, D_MODEL), lambda i: (l, 0, 0)),
            _resident((None, D_MODEL, 2 * D_MODEL), lambda i: (l, 0, 0)),
            _resident((DEPTH, D_MODEL), lambda i: (0, 0)),
            _resident((DEPTH, N_GROUPS_A, CHUNK, CHUNK), lambda i: (0, 0, 0, 0)),
            _resident((None, CHUNK, D_MODEL), lambda i: (l, 0, 0)),
        ],
        out_specs=pl.BlockSpec((ROW_TILE, D_MODEL), lambda i: (i, 0)),
        out_shape=jax.ShapeDtypeStruct((N_TOK, D_MODEL), BF16),
        compiler_params=_params(),
        name="sgu",
    )(x, mod, g_norm, w_in, sgu_gain, w_spatial, bias_tile)


def _rope_tables():
    pos = np.arange(DEC_SEQ)
    lane = np.arange(V_DIM)
    half = HEAD_DIM // 2
    within = lane % half
    coord = np.where((lane % HEAD_DIM) < half, pos[:, None] // GRID_W, pos[:, None] % GRID_W)
    freqs = ROPE_THETA ** (-np.arange(0, half, 2, dtype=np.float64) / half)
    ang = coord * freqs[within % (half // 2)][None, :]
    first = (within < half // 2)[None, :]
    cos = np.cos(ang)
    sin_lo = np.where(first, -np.sin(ang), 0.0)
    sin_hi = np.where(first, 0.0, np.sin(ang))
    return tuple(jnp.asarray(t, dtype=F32) for t in (cos, sin_lo, sin_hi))


def _qkv_kernel(x_ref, mod_ref, g_ref, wq_ref, wk_ref, wv_ref, *rest, latent, n_extra):
    extra, outs = rest[:n_extra], rest[n_extra:]
    qkv_ref = outs[0]
    row = 1 + pl.program_id(0) // (DEC_SEQ // ROW_TILE) if latent else 0
    h = _norm_mod(x_ref[...], g_ref[1:2, :], _mod_chunk(mod_ref, row, 3), _mod_chunk(mod_ref, row, 4))
    q = _dot(h, wq_ref[...]) * (HEAD_DIM ** -0.5)
    k = _dot(h, wk_ref[...])
    v = _dot(h, wv_ref[...])
    qkv_ref[:, 2 * D_MODEL:] = v.astype(BF16)
    if not latent:
        kc_ref, vc_ref = outs[1:]
        kc_ref[...] = k.reshape(kc_ref.shape)
        vc_ref[...] = v.reshape(vc_ref.shape)
        qkv_ref[:, :D_MODEL] = q.astype(BF16)
        qkv_ref[:, D_MODEL:2 * D_MODEL] = k.astype(BF16)
        return
    cos, sin_lo, sin_hi = (r[...] for r in extra)
    for c0 in range(0, D_MODEL, V_DIM):
        for src, base in ((q, 0), (k, D_MODEL)):
            t = src[:, c0:c0 + V_DIM]
            t = (t * cos + pltpu.roll(t, V_DIM - HEAD_DIM // 4, 1) * sin_lo
                 + pltpu.roll(t, HEAD_DIM // 4, 1) * sin_hi)
            qkv_ref[:, base + c0:base + c0 + V_DIM] = t.astype(BF16)


def _qkv(x, mod, g_norm, w_in, l, latent, rope=None, caches=None):
    rows, tile0 = (N_LAT, N_CTX // ROW_TILE) if latent else (N_CTX, 0)
    col0 = 2
    in_specs = [
        pl.BlockSpec((ROW_TILE, D_MODEL), lambda i: (tile0 + i, 0)),
        _resident((None, COND_ROWS, N_MOD * D_MODEL), lambda i: (l, 0, 0)),
        _resident((None, 3, D_MODEL), lambda i: (l, 0, 0)),
    ] + [_resident((None, D_MODEL, D_MODEL), functools.partial(lambda i, c: (l, 0, c), c=col0 + j))
         for j in range(3)]
    args = [x, mod, g_norm, w_in, w_in, w_in]
    out_shape = [jax.ShapeDtypeStruct((rows, 3 * D_MODEL), BF16)]
    out_specs = [pl.BlockSpec((ROW_TILE, 3 * D_MODEL), lambda i: (i, 0))]
    aliases = {}
    if latent:
        tiles = DEC_SEQ // ROW_TILE
        in_specs += [pl.BlockSpec((ROW_TILE, V_DIM), lambda i: (i % tiles, 0))] * 3
        args += list(rope)
    else:
        seqs = ROW_TILE // SEQ
        cache = jax.ShapeDtypeStruct((BATCH, DEPTH, SEQ, D_MODEL), F32)
        out_shape += [cache, cache]
        out_specs += [pl.BlockSpec((seqs, None, SEQ, D_MODEL), lambda i: (i, l, 0, 0))] * 2
        if caches is not None:
            in_specs += [pl.BlockSpec(memory_space=pl.ANY)] * 2
            args += list(caches)
            aliases = {len(args) - 2: 1, len(args) - 1: 2}
    return pl.pallas_call(
        functools.partial(_qkv_kernel, latent=latent, n_extra=len(args) - 6),
        grid=(rows // ROW_TILE,),
        in_specs=in_specs,
        out_specs=out_specs,
        out_shape=out_shape,
        input_output_aliases=aliases,
        compiler_params=_params(),
        name="qkv_latent" if latent else "qkv_context",
    )(*args)


def _lam_full(lam_ref, l):
    lp = lam_ref[l]
    s01 = jnp.sum(lp[0:1, :] * lp[1:2, :], axis=-1, keepdims=True)
    s23 = jnp.sum(lp[2:3, :] * lp[3:4, :], axis=-1, keepdims=True)
    return jnp.exp(s01) - jnp.exp(s23) + _lam_init(l)


def _lam_init(l):
    return 0.8 - 0.6 * math.exp(-0.3 * l)


def _attn_scores(q, ks):
    lane = lax.broadcasted_iota(jnp.int32, (1, V_DIM), 1)
    lo = (lane < HEAD_DIM).astype(BF16)
    qs = jnp.concatenate([q * lo, q * (1 - lo)], axis=0)
    return [_dot_nt(qs, k) for k in ks]


def _attn_combine(s, vs, lam_full, gain, l):
    n_q = s[0].shape[0] // 2
    m = functools.reduce(jnp.maximum, [jnp.max(t, axis=-1, keepdims=True) for t in s])
    pv = functools.reduce(jnp.add, [
        _dot(jnp.exp(t - m).astype(BF16), jnp.concatenate([v, jnp.ones_like(v)], axis=1))
        for t, v in zip(s, vs)])
    pv = pv[:, :V_DIM] / pv[:, V_DIM:]
    o = pv[:n_q] - lam_full * pv[n_q:]
    return _rms(o) * gain * (1.0 - _lam_init(l))


def _attn_heads(q_ref, k_refs, v_refs, lam_ref, gain_ref, o_ref, l):
    lam_full = _lam_full(lam_ref, l)
    gain = gain_ref[l:l + 1, :]

    def scores(hd):
        cs = slice(hd * V_DIM, (hd + 1) * V_DIM)
        return _attn_scores(q_ref[:, cs], [r[:, cs].astype(BF16) for r in k_refs])

    nxt = scores(0)
    for hd in range(N_HEADS):
        cs = slice(hd * V_DIM, (hd + 1) * V_DIM)
        s = nxt
        if hd + 1 < N_HEADS:
            nxt = scores(hd + 1)
        o = _attn_combine(s, [r[:, cs].astype(BF16) for r in v_refs], lam_full, gain, l)
        o_ref[:, cs] = o.astype(BF16)


def _attn_ctx_kernel(q_ref, k_ref, v_ref, lam_ref, gain_ref, o_ref, *, l):
    _attn_heads(q_ref, [k_ref], [v_ref], lam_ref, gain_ref, o_ref, l)


def _attn_lat_kernel(q_ref, k_ref, v_ref, past_k, past_v, lam_ref, gain_ref, _, o_ref, *, l):
    _attn_heads(q_ref, [past_k, k_ref], [past_v, v_ref], lam_ref, gain_ref, o_ref, l)


def _attn_ctx(qkv, lam, subln_gain, l):
    part = [pl.BlockSpec((SEQ, D_MODEL), functools.partial(lambda i, c: (i, c), c=c)) for c in range(3)]
    return pl.pallas_call(
        functools.partial(_attn_ctx_kernel, l=l),
        grid=(BATCH,),
        in_specs=part + [_resident((DEPTH, 4, HEAD_DIM), lambda i: (0, 0, 0)),
                         _resident((DEPTH, V_DIM), lambda i: (0, 0))],
        out_specs=pl.BlockSpec((SEQ, D_MODEL), lambda i: (i, 0)),
        out_shape=jax.ShapeDtypeStruct((N_TOK, D_MODEL), BF16),
        compiler_params=_params(),
        name="attn_context",
    )(qkv, qkv, qkv, lam, subln_gain)


def _attn_lat(qkv, past_k, past_v, lam, subln_gain, b_all, l):
    tiles = DEC_SEQ // Q_TILE
    qspec = pl.BlockSpec((Q_TILE, D_MODEL), lambda b, j: (b * tiles + j, 0))
    kspec = pl.BlockSpec((DEC_SEQ, D_MODEL), lambda b, j: (b, 1))
    vspec = pl.BlockSpec((DEC_SEQ, D_MODEL), lambda b, j: (b, 2))
    past = pl.BlockSpec((None, None, PAST_LEN, D_MODEL), lambda b, j: (b, l, 0, 0))
    return pl.pallas_call(
        functools.partial(_attn_lat_kernel, l=l),
        grid=(DEC_BATCH, tiles),
        in_specs=[qspec, kspec, vspec, past, past,
                  _resident((DEPTH, 4, HEAD_DIM), lambda b, j: (0, 0, 0)),
                  _resident((DEPTH, V_DIM), lambda b, j: (0, 0)),
                  pl.BlockSpec(memory_space=pl.ANY)],
        out_specs=pl.BlockSpec((Q_TILE, D_MODEL), lambda b, j: (N_CTX // Q_TILE + b * tiles + j, 0)),
        out_shape=jax.ShapeDtypeStruct((N_TOK, D_MODEL), BF16),
        input_output_aliases={7: 0},
        compiler_params=_params(),
        name="attn_latent",
    )(qkv, qkv, qkv, past_k, past_v, lam, subln_gain, b_all)


def _merge_kernel(x_ref, a_ref, b_ref, mod_ref, g_ref, wga_ref, wgb_ref, wa_ref, wb_ref, wo_ref, o_ref):
    row = _stacked_mod_row(pl.program_id(0))
    x = x_ref[...]
    h = _norm_mod(x, g_ref[1:2, :], _mod_chunk(mod_ref, row, 3), _mod_chunk(mod_ref, row, 4))
    merged = (jax.nn.sigmoid(_dot(h, wga_ref[...])) * _dot(a_ref[...].astype(F32), wa_ref[...])
              + jax.nn.sigmoid(_dot(h, wgb_ref[...])) * _dot(b_ref[...].astype(F32), wb_ref[...]))
    o_ref[...] = x + _mod_chunk(mod_ref, row, 5) * _dot(merged, wo_ref[...])


def _merge(x, a, b, mod, g_norm, w_in, w_branch_a, w_branch_b, w_out, l):
    tile = pl.BlockSpec((ROW_TILE, D_MODEL), lambda i: (i, 0))
    square = _resident((None, D_MODEL, D_MODEL), lambda i: (l, 0, 0))
    return pl.pallas_call(
        _merge_kernel,
        grid=(N_TOK // ROW_TILE,),
        in_specs=[tile, tile, tile,
                  _resident((None, COND_ROWS, N_MOD * D_MODEL), lambda i: (l, 0, 0)),
                  _resident((None, 3, D_MODEL), lambda i: (l, 0, 0)),
                  _resident((None, D_MODEL, D_MODEL), lambda i: (l, 0, 5)),
                  _resident((None, D_MODEL, D_MODEL), lambda i: (l, 0, 6)),
                  square, square, square],
        out_specs=tile,
        out_shape=jax.ShapeDtypeStruct((N_TOK, D_MODEL), F32),
        input_output_aliases={0: 0},
        compiler_params=_params(),
        name="merge",
    )(x, a, b, mod, g_norm, w_in, w_in, w_branch_a, w_branch_b, w_out)


def kernel(x_prompt, x_sample, cache_k, cache_v, c, c_ctx, w_mod, b_mod, g_norm, ffn1_w13, ffn1_w2, w_in,
           sgu_gain, w_spatial, b_spatial, lam, subln_gain, w_branch_a, w_branch_b, w_out, ffn2_w13,
           ffn2_w2, g_final):
    cond = jnp.concatenate(
        [c_ctx[None, :], c, jnp.zeros((COND_ROWS - 1 - DEC_BATCH, D_MODEL), F32)], axis=0)
    mod = _modulation(cond, w_mod, b_mod)
    bias_tile = jnp.repeat(jnp.swapaxes(b_spatial, 1, 2), GROUP_W, axis=2)
    rope = _rope_tables()
    past_k = cache_k.reshape(DEC_BATCH, DEPTH, PAST_LEN, D_MODEL)
    past_v = cache_v.reshape(DEC_BATCH, DEPTH, PAST_LEN, D_MODEL)

    x = (x_prompt.reshape(N_CTX, D_MODEL), x_sample.reshape(N_LAT, D_MODEL))
    caches = None
    for l in range(DEPTH):
        x = _ffn(x, mod, g_norm, ffn1_w13, ffn1_w2, None, l, 0)
        a = _sgu(x, mod, g_norm, w_in, sgu_gain, w_spatial, bias_tile, l)
        qkv_ctx, *caches = _qkv(x, mod, g_norm, w_in, l, latent=False, caches=caches)
        qkv_lat, = _qkv(x, mod, g_norm, w_in, l, latent=True, rope=rope)
        b = _attn_ctx(qkv_ctx, lam, subln_gain, l)
        b = _attn_lat(qkv_lat, past_k, past_v, lam, subln_gain, b, l)
        x = _merge(x, a, b, mod, g_norm, w_in, w_branch_a, w_branch_b, w_out, l)
        x = _ffn(x, mod, g_norm, ffn2_w13, ffn2_w2, g_final if l == DEPTH - 1 else None, l, 2)

    y_ctx, y_lat = x
    new_k, new_v = caches
    return (y_ctx.reshape(BATCH, SEQ, D_MODEL),
            y_lat.reshape(DEC_BATCH, DEC_SEQ, D_MODEL),
            new_k.reshape(BATCH, DEPTH, SEQ, N_HEADS, 2 * HEAD_DIM),
            new_v.reshape(BATCH, DEPTH, SEQ, N_HEADS, V_DIM))
```

```python
import functools
import math

import numpy as np
import jax
import jax.numpy as jnp
from jax import lax
from jax.experimental import pallas as pl
from jax.experimental.pallas import tpu as pltpu

D_MODEL = 1024
BATCH = 16
SEQ = 256
DEPTH = 4
DEC_BATCH = 2
DEC_SEQ = 1024
PAST_LEN = 256
GRID_W = 64
CHUNK = 128
N_GROUPS_A = 4
GROUP_W = D_MODEL // N_GROUPS_A
N_HEADS = 8
HEAD_DIM = D_MODEL // (2 * N_HEADS)
V_DIM = 2 * HEAD_DIM
D_FF = ((8 * D_MODEL // 3 + 127) // 128) * 128
IN_W = 7 * D_MODEL
N_MOD = 9
ROPE_THETA = 10000.0
EPS = 1e-6

N_CTX = BATCH * SEQ
N_LAT = DEC_BATCH * DEC_SEQ
N_TOK = N_CTX + N_LAT
COND_ROWS = 8

ROW_TILE = 512
SGU_SUB = 256
FFN_TILE = 512
FF_CHUNK = 256
FFN_DMA_AHEAD = 3
MOD_TILE = 2304
Q_TILE = 512
V7X_VMEM_BYTES = 64 * 1024 * 1024
VMEM_LIMIT_BYTES = V7X_VMEM_BYTES - 4 * 1024 * 1024

F32 = jnp.float32
BF16 = jnp.bfloat16


def _params():
    return pltpu.CompilerParams(vmem_limit_bytes=VMEM_LIMIT_BYTES)


def _resident(block_shape, index_map):
    return pl.BlockSpec(block_shape, index_map, pipeline_mode=pl.Buffered(1))


def _dot(a, b):
    return jnp.dot(a, b, preferred_element_type=F32)


def _dot_nt(a, b):
    return lax.dot_general(a, b, (((1,), (1,)), ((), ())), preferred_element_type=F32)


def _rms(x):
    return x * lax.rsqrt(jnp.mean(x * x, axis=-1, keepdims=True) + EPS)


def _norm_mod(x, g, shift, scale):
    return _rms(x) * g * (1.0 + scale) + shift


def _stacked_mod_row(i, tile=ROW_TILE):
    n_ctx_tiles = N_CTX // tile
    return jnp.where(i < n_ctx_tiles, 0, 1 + (i - n_ctx_tiles) // (DEC_SEQ // tile))


def _mod_chunk(mod_ref, row, k):
    return mod_ref[pl.ds(row, 1), k * D_MODEL:(k + 1) * D_MODEL]


def _mod_kernel(c_ref, w_ref, b_ref, o_ref):
    c = c_ref[...]
    o_ref[...] = _dot(c * jax.nn.sigmoid(c), w_ref[...]) + b_ref[...]


def _modulation(cond, w_mod, b_mod):
    n = N_MOD * D_MODEL
    return pl.pallas_call(
        _mod_kernel,
        grid=(DEPTH, n // MOD_TILE),
        in_specs=[
            pl.BlockSpec((COND_ROWS, D_MODEL), lambda l, j: (0, 0)),
            pl.BlockSpec((None, D_MODEL, MOD_TILE), lambda l, j: (l, 0, j)),
            pl.BlockSpec((None, 1, MOD_TILE), lambda l, j: (l, 0, j)),
        ],
        out_specs=pl.BlockSpec((None, COND_ROWS, MOD_TILE), lambda l, j: (l, 0, j)),
        out_shape=jax.ShapeDtypeStruct((DEPTH, COND_ROWS, n), F32),
        compiler_params=_params(),
        name="modulation",
    )(cond, w_mod, b_mod.reshape(DEPTH, 1, n))


def _ffn_weight_copies(w13_hbm, w2_hbm, w13_ref, w2_ref, sems, l, lo):
    c = lo // FF_CHUNK
    cols = pl.ds(lo, FF_CHUNK)
    up_cols = pl.ds(D_FF + lo, FF_CHUNK)
    return (pltpu.make_async_copy(w13_hbm.at[l, :, cols], w13_ref.at[:, cols], sems.at[c, 0]),
            pltpu.make_async_copy(w13_hbm.at[l, :, up_cols], w13_ref.at[:, up_cols], sems.at[c, 1]),
            pltpu.make_async_copy(w2_hbm.at[l, cols, :], w2_ref.at[cols, :], sems.at[c, 2]))


def _ffn_kernel(*refs, l, sub, split_in, final):
    n_x = 2 if split_in else 1
    x_refs, (mod_ref, g_ref, w13_hbm, w2_hbm) = refs[:n_x], refs[n_x:n_x + 4]
    *rest, w13_ref, w2_ref, sems = refs[n_x + 4:]
    i = pl.program_id(0)
    is_ctx = i < N_CTX // FFN_TILE
    row = _stacked_mod_row(i, FFN_TILE)
    chunks = range(0, D_FF, FF_CHUNK)

    def body(first_step):
        def start(lo):
            if first_step and lo < D_FF:
                for cp in _ffn_weight_copies(w13_hbm, w2_hbm, w13_ref, w2_ref, sems, l, lo):
                    cp.start()

        def arrive(lo):
            if first_step:
                for cp in _ffn_weight_copies(w13_hbm, w2_hbm, w13_ref, w2_ref, sems, l, lo):
                    cp.wait()
                start(lo + FFN_DMA_AHEAD * FF_CHUNK)

        for lo in chunks[:FFN_DMA_AHEAD]:
            start(lo)
        x = jnp.where(is_ctx, x_refs[0][...], x_refs[1][...]) if split_in else x_refs[0][...]
        h = _norm_mod(x, g_ref[sub:sub + 1, :], _mod_chunk(mod_ref, row, 3 * sub),
                      _mod_chunk(mod_ref, row, 3 * sub + 1))

        def gate_up(lo):
            arrive(lo)
            return (_dot(h, w13_ref[:, lo:lo + FF_CHUNK]),
                    _dot(h, w13_ref[:, D_FF + lo:D_FF + lo + FF_CHUNK]))

        acc = jnp.zeros((FFN_TILE, D_MODEL), F32)
        nxt = gate_up(0)
        for lo in chunks:
            gate, up = nxt
            if lo + FF_CHUNK < D_FF:
                nxt = gate_up(lo + FF_CHUNK)
            acc = acc + _dot(gate * jax.nn.sigmoid(gate) * up, w2_ref[lo:lo + FF_CHUNK, :])
        y = x + 0.5 * _mod_chunk(mod_ref, row, 3 * sub + 2) * acc
        if not final:
            rest[0][...] = y
            return
        gf_ref, ctx_ref, lat_ref = rest
        y = _rms(y) * gf_ref[...]

        @pl.when(is_ctx)
        def _():
            ctx_ref[...] = y

        @pl.when(jnp.logical_not(is_ctx))
        def _():
            lat_ref[...] = y

    lax.cond(i == 0, lambda: body(True), lambda: body(False))


def _ffn(xs, mod, g_norm, w13, w2, g_final, l, sub):
    split_in, final = isinstance(xs, tuple), g_final is not None
    n_ctx_tiles = N_CTX // FFN_TILE
    tile = pl.BlockSpec((FFN_TILE, D_MODEL), lambda i: (i, 0))
    ctx_tile = pl.BlockSpec((FFN_TILE, D_MODEL), lambda i: (jnp.minimum(i, n_ctx_tiles - 1), 0))
    lat_tile = pl.BlockSpec((FFN_TILE, D_MODEL), lambda i: (jnp.maximum(i - n_ctx_tiles, 0), 0))
    in_specs = ([ctx_tile, lat_tile] if split_in else [tile]) + [
        _resident((None, COND_ROWS, N_MOD * D_MODEL), lambda i: (l, 0, 0)),
        _resident((None, 3, D_MODEL), lambda i: (l, 0, 0)),
        pl.BlockSpec(memory_space=pl.ANY),
        pl.BlockSpec(memory_space=pl.ANY),
    ]
    args = (list(xs) if split_in else [xs]) + [mod, g_norm, w13, w2]
    if final:
        in_specs.append(_resident((1, D_MODEL), lambda i: (0, 0)))
        args.append(g_final.reshape(1, D_MODEL))
        out_specs = [ctx_tile, lat_tile]
        out_shape = [jax.ShapeDtypeStruct((N_CTX, D_MODEL), F32), jax.ShapeDtypeStruct((N_LAT, D_MODEL), F32)]
    else:
        out_specs, out_shape = tile, jax.ShapeDtypeStruct((N_TOK, D_MODEL), F32)
    return pl.pallas_call(
        functools.partial(_ffn_kernel, l=l, sub=sub, split_in=split_in, final=final),
        grid=(N_TOK // FFN_TILE,),
        in_specs=in_specs,
        out_specs=out_specs,
        out_shape=out_shape,
        scratch_shapes=[pltpu.VMEM((D_MODEL, 2 * D_FF), F32), pltpu.VMEM((D_FF, D_MODEL), F32),
                        pltpu.SemaphoreType.DMA((D_FF // FF_CHUNK, 3))],
        input_output_aliases={} if split_in or final else {0: 0},
        compiler_params=_params(),
        name="ffn",
    )(*args)


def _gelu_tanh(x):
    return x * (0.5 * (1.0 + jnp.tanh(math.sqrt(2.0 / math.pi) * (x + 0.044715 * (x * x * x)))))


def _sgu_kernel(x_ref, mod_ref, g_ref, wz_ref, gain_ref, ws_ref, bias_ref, o_ref, *, l):
    row = _stacked_mod_row(pl.program_id(0))

    def gated(r0):
        h = _norm_mod(x_ref[r0:r0 + SGU_SUB, :], g_ref[1:2, :], _mod_chunk(mod_ref, row, 3),
                      _mod_chunk(mod_ref, row, 4))
        return _gelu_tanh(_dot(h, wz_ref[...]))

    def mix(r0, z):
        u = z[:, :D_MODEL]
        v = _rms(z[:, D_MODEL:]) * gain_ref[l:l + 1, :]
        for c in range(0, SGU_SUB, CHUNK):
            for g in range(N_GROUPS_A):
                c0 = g * GROUP_W
                mixed = _dot(ws_ref[l, g], v[c:c + CHUNK, c0:c0 + GROUP_W]) + bias_ref[:, c0:c0 + GROUP_W]
                o_ref[r0 + c:r0 + c + CHUNK, c0:c0 + GROUP_W] = (
                    u[c:c + CHUNK, c0:c0 + GROUP_W] * mixed).astype(BF16)

    starts = list(range(0, ROW_TILE, SGU_SUB))
    nxt = gated(starts[0])
    for i, r0 in enumerate(starts):
        z = nxt
        if i + 1 < len(starts):
            nxt = gated(starts[i + 1])
        mix(r0, z)


def _sgu(x, mod, g_norm, w_in, sgu_gain, w_spatial, bias_tile, l):
    return pl.pallas_call(
        functools.partial(_sgu_kernel, l=l),
        grid=(N_TOK // ROW_TILE,),
        in_specs=[
            pl.BlockSpec((ROW_TILE, D_MODEL), lambda i: (i, 0)),
            _resident((None, COND_ROWS, N_MOD * D_MODEL), lambda i: (l, 0, 0)),
            _resident((None, 3, D_MODEL), lambda i: (l, 0, 0)),
            _resident((None, D_MODEL, 2 * D_MODEL), lambda i: (l, 0, 0)),
            _resident((DEPTH, D_MODEL), lambda i: (0, 0)),
            _resident((DEPTH, N_GROUPS_A, CHUNK, CHUNK), lambda i: (0, 0, 0, 0)),
            _resident((None, CHUNK, D_MODEL), lambda i: (l, 0, 0)),
        ],
        out_specs=pl.BlockSpec((ROW_TILE, D_MODEL), lambda i: (i, 0)),
        out_shape=jax.ShapeDtypeStruct((N_TOK, D_MODEL), BF16),
        compiler_params=_params(),
        name="sgu",
    )(x, mod, g_norm, w_in, sgu_gain, w_spatial, bias_tile)


def _rope_tables():
    pos = np.arange(DEC_SEQ)
    lane = np.arange(V_DIM)
    half = HEAD_DIM // 2
    within = lane % half
    coord = np.where((lane % HEAD_DIM) < half, pos[:, None] // GRID_W, pos[:, None] % GRID_W)
    freqs = ROPE_THETA ** (-np.arange(0, half, 2, dtype=np.float64) / half)
    ang = coord * freqs[within % (half // 2)][None, :]
    first = (within < half // 2)[None, :]
    cos = np.cos(ang)
    sin_lo = np.where(first, -np.sin(ang), 0.0)
    sin_hi = np.where(first, 0.0, np.sin(ang))
    return tuple(jnp.asarray(t, dtype=F32) for t in (cos, sin_lo, sin_hi))


def _qkv_kernel(x_ref, mod_ref, g_ref, wq_ref, wk_ref, wv_ref, *rest, latent, n_extra):
    extra, outs = rest[:n_extra], rest[n_extra:]
    qkv_ref = outs[0]
    row = 1 + pl.program_id(0) // (DEC_SEQ // ROW_TILE) if latent else 0
    h = _norm_mod(x_ref[...], g_ref[1:2, :], _mod_chunk(mod_ref, row, 3), _mod_chunk(mod_ref, row, 4))
    q = _dot(h, wq_ref[...]) * (HEAD_DIM ** -0.5)
    k = _dot(h, wk_ref[...])
    v = _dot(h, wv_ref[...])
    qkv_ref[:, 2 * D_MODEL:] = v.astype(BF16)
    if not latent:
        kc_ref, vc_ref = outs[1:]
        kc_ref[...] = k.reshape(kc_ref.shape)
        vc_ref[...] = v.reshape(vc_ref.shape)
        qkv_ref[:, :D_MODEL] = q.astype(BF16)
        qkv_ref[:, D_MODEL:2 * D_MODEL] = k.astype(BF16)
        return
    cos, sin_lo, sin_hi = (r[...] for r in extra)
    for c0 in range(0, D_MODEL, V_DIM):
        for src, base in ((q, 0), (k, D_MODEL)):
            t = src[:, c0:c0 + V_DIM]
            t = (t * cos + pltpu.roll(t, V_DIM - HEAD_DIM // 4, 1) * sin_lo
                 + pltpu.roll(t, HEAD_DIM // 4, 1) * sin_hi)
            qkv_ref[:, base + c0:base + c0 + V_DIM] = t.astype(BF16)


def _qkv(x, mod, g_norm, w_in, l, latent, rope=None, caches=None):
    rows, tile0 = (N_LAT, N_CTX // ROW_TILE) if latent else (N_CTX, 0)
    col0 = 2
    in_specs = [
        pl.BlockSpec((ROW_TILE, D_MODEL), lambda i: (tile0 + i, 0)),
        _resident((None, COND_ROWS, N_MOD * D_MODEL), lambda i: (l, 0, 0)),
        _resident((None, 3, D_MODEL), lambda i: (l, 0, 0)),
    ] + [_resident((None, D_MODEL, D_MODEL), functools.partial(lambda i, c: (l, 0, c), c=col0 + j))
         for j in range(3)]
    args = [x, mod, g_norm, w_in, w_in, w_in]
    out_shape = [jax.ShapeDtypeStruct((rows, 3 * D_MODEL), BF16)]
    out_specs = [pl.BlockSpec((ROW_TILE, 3 * D_MODEL), lambda i: (i, 0))]
    aliases = {}
    if latent:
        tiles = DEC_SEQ // ROW_TILE
        in_specs += [pl.BlockSpec((ROW_TILE, V_DIM), lambda i: (i % tiles, 0))] * 3
        args += list(rope)
    else:
        seqs = ROW_TILE // SEQ
        cache = jax.ShapeDtypeStruct((BATCH, DEPTH, SEQ, D_MODEL), F32)
        out_shape += [cache, cache]
        out_specs += [pl.BlockSpec((seqs, None, SEQ, D_MODEL), lambda i: (i, l, 0, 0))] * 2
        if caches is not None:
            in_specs += [pl.BlockSpec(memory_space=pl.ANY)] * 2
            args += list(caches)
            aliases = {len(args) - 2: 1, len(args) - 1: 2}
    return pl.pallas_call(
        functools.partial(_qkv_kernel, latent=latent, n_extra=len(args) - 6),
        grid=(rows // ROW_TILE,),
        in_specs=in_specs,
        out_specs=out_specs,
        out_shape=out_shape,
        input_output_aliases=aliases,
        compiler_params=_params(),
        name="qkv_latent" if latent else "qkv_context",
    )(*args)


def _lam_full(lam_ref, l):
    lp = lam_ref[l]
    s01 = jnp.sum(lp[0:1, :] * lp[1:2, :], axis=-1, keepdims=True)
    s23 = jnp.sum(lp[2:3, :] * lp[3:4, :], axis=-1, keepdims=True)
    return jnp.exp(s01) - jnp.exp(s23) + _lam_init(l)


def _lam_init(l):
    return 0.8 - 0.6 * math.exp(-0.3 * l)


def _attn_scores(q, ks):
    lane = lax.broadcasted_iota(jnp.int32, (1, V_DIM), 1)
    lo = (lane < HEAD_DIM).astype(BF16)
    qs = jnp.concatenate([q * lo, q * (1 - lo)], axis=0)
    return [_dot_nt(qs, k) for k in ks]


def _attn_combine(s, vs, lam_full, gain, l):
    n_q = s[0].shape[0] // 2
    m = functools.reduce(jnp.maximum, [jnp.max(t, axis=-1, keepdims=True) for t in s])
    pv = functools.reduce(jnp.add, [
        _dot(jnp.exp(t - m).astype(BF16), jnp.concatenate([v, jnp.ones_like(v)], axis=1))
        for t, v in zip(s, vs)])
    pv = pv[:, :V_DIM] / pv[:, V_DIM:]
    o = pv[:n_q] - lam_full * pv[n_q:]
    return _rms(o) * gain * (1.0 - _lam_init(l))


def _attn_heads(q_ref, k_refs, v_refs, lam_ref, gain_ref, o_ref, l):
    lam_full = _lam_full(lam_ref, l)
    gain = gain_ref[l:l + 1, :]

    def scores(hd):
        cs = slice(hd * V_DIM, (hd + 1) * V_DIM)
        return _attn_scores(q_ref[:, cs], [r[:, cs].astype(BF16) for r in k_refs])

    nxt = scores(0)
    for hd in range(N_HEADS):
        cs = slice(hd * V_DIM, (hd + 1) * V_DIM)
        s = nxt
        if hd + 1 < N_HEADS:
            nxt = scores(hd + 1)
        o = _attn_combine(s, [r[:, cs].astype(BF16) for r in v_refs], lam_full, gain, l)
        o_ref[:, cs] = o.astype(BF16)


def _attn_ctx_kernel(q_ref, k_ref, v_ref, lam_ref, gain_ref, o_ref, *, l):
    _attn_heads(q_ref, [k_ref], [v_ref], lam_ref, gain_ref, o_ref, l)


def _attn_lat_kernel(q_ref, k_ref, v_ref, past_k, past_v, lam_ref, gain_ref, _, o_ref, *, l):
    _attn_heads(q_ref, [past_k, k_ref], [past_v, v_ref], lam_ref, gain_ref, o_ref, l)


def _attn_ctx(qkv, lam, subln_gain, l):
    part = [pl.BlockSpec((SEQ, D_MODEL), functools.partial(lambda i, c: (i, c), c=c)) for c in range(3)]
    return pl.pallas_call(
        functools.partial(_attn_ctx_kernel, l=l),
        grid=(BATCH,),
        in_specs=part + [_resident((DEPTH, 4, HEAD_DIM), lambda i: (0, 0, 0)),
                         _resident((DEPTH, V_DIM), lambda i: (0, 0))],
        out_specs=pl.BlockSpec((SEQ, D_MODEL), lambda i: (i, 0)),
        out_shape=jax.ShapeDtypeStruct((N_TOK, D_MODEL), BF16),
        compiler_params=_params(),
        name="attn_context",
    )(qkv, qkv, qkv, lam, subln_gain)


def _attn_lat(qkv, past_k, past_v, lam, subln_gain, b_all, l):
    tiles = DEC_SEQ // Q_TILE
    qspec = pl.BlockSpec((Q_TILE, D_MODEL), lambda b, j: (b * tiles + j, 0))
    kspec = pl.BlockSpec((DEC_SEQ, D_MODEL), lambda b, j: (b, 1))
    vspec = pl.BlockSpec((DEC_SEQ, D_MODEL), lambda b, j: (b, 2))
    past = pl.BlockSpec((None, None, PAST_LEN, D_MODEL), lambda b, j: (b, l, 0, 0))
    return pl.pallas_call(
        functools.partial(_attn_lat_kernel, l=l),
        grid=(DEC_BATCH, tiles),
        in_specs=[qspec, kspec, vspec, past, past,
                  _resident((DEPTH, 4, HEAD_DIM), lambda b, j: (0, 0, 0)),
                  _resident((DEPTH, V_DIM), lambda b, j: (0, 0)),
                  pl.BlockSpec(memory_space=pl.ANY)],
        out_specs=pl.BlockSpec((Q_TILE, D_MODEL), lambda b, j: (N_CTX // Q_TILE + b * tiles + j, 0)),
        out_shape=jax.ShapeDtypeStruct((N_TOK, D_MODEL), BF16),
        input_output_aliases={7: 0},
        compiler_params=_params(),
        name="attn_latent",
    )(qkv, qkv, qkv, past_k, past_v, lam, subln_gain, b_all)


def _merge_kernel(x_ref, a_ref, b_ref, mod_ref, g_ref, wga_ref, wgb_ref, wa_ref, wb_ref, wo_ref, o_ref):
    row = _stacked_mod_row(pl.program_id(0))
    x = x_ref[...]
    h = _norm_mod(x, g_ref[1:2, :], _mod_chunk(mod_ref, row, 3), _mod_chunk(mod_ref, row, 4))
    merged = (jax.nn.sigmoid(_dot(h, wga_ref[...])) * _dot(a_ref[...].astype(F32), wa_ref[...])
              + jax.nn.sigmoid(_dot(h, wgb_ref[...])) * _dot(b_ref[...].astype(F32), wb_ref[...]))
    o_ref[...] = x + _mod_chunk(mod_ref, row, 5) * _dot(merged, wo_ref[...])


def _merge(x, a, b, mod, g_norm, w_in, w_branch_a, w_branch_b, w_out, l):
    tile = pl.BlockSpec((ROW_TILE, D_MODEL), lambda i: (i, 0))
    square = _resident((None, D_MODEL, D_MODEL), lambda i: (l, 0, 0))
    return pl.pallas_call(
        _merge_kernel,
        grid=(N_TOK // ROW_TILE,),
        in_specs=[tile, tile, tile,
                  _resident((None, COND_ROWS, N_MOD * D_MODEL), lambda i: (l, 0, 0)),
                  _resident((None, 3, D_MODEL), lambda i: (l, 0, 0)),
                  _resident((None, D_MODEL, D_MODEL), lambda i: (l, 0, 5)),
                  _resident((None, D_MODEL, D_MODEL), lambda i: (l, 0, 6)),
                  square, square, square],
        out_specs=tile,
        out_shape=jax.ShapeDtypeStruct((N_TOK, D_MODEL), F32),
        input_output_aliases={0: 0},
        compiler_params=_params(),
        name="merge",
    )(x, a, b, mod, g_norm, w_in, w_in, w_branch_a, w_branch_b, w_out)


def kernel(x_prompt, x_sample, cache_k, cache_v, c, c_ctx, w_mod, b_mod, g_norm, ffn1_w13, ffn1_w2, w_in,
           sgu_gain, w_spatial, b_spatial, lam, subln_gain, w_branch_a, w_branch_b, w_out, ffn2_w13,
           ffn2_w2, g_final):
    cond = jnp.concatenate(
        [c_ctx[None, :], c, jnp.zeros((COND_ROWS - 1 - DEC_BATCH, D_MODEL), F32)], axis=0)
    mod = _modulation(cond, w_mod, b_mod)
    bias_tile = jnp.repeat(jnp.swapaxes(b_spatial, 1, 2), GROUP_W, axis=2)
    rope = _rope_tables()
    past_k = cache_k.reshape(DEC_BATCH, DEPTH, PAST_LEN, D_MODEL)
    past_v = cache_v.reshape(DEC_BATCH, DEPTH, PAST_LEN, D_MODEL)

    x = (x_prompt.reshape(N_CTX, D_MODEL), x_sample.reshape(N_LAT, D_MODEL))
    caches = None
    for l in range(DEPTH):
        x = _ffn(x, mod, g_norm, ffn1_w13, ffn1_w2, None, l, 0)
        a = _sgu(x, mod, g_norm, w_in, sgu_gain, w_spatial, bias_tile, l)
        qkv_ctx, *caches = _qkv(x, mod, g_norm, w_in, l, latent=False, caches=caches)
        qkv_lat, = _qkv(x, mod, g_norm, w_in, l, latent=True, rope=rope)
        b = _attn_ctx(qkv_ctx, lam, subln_gain, l)
        b = _attn_lat(qkv_lat, past_k, past_v, lam, subln_gain, b, l)
        x = _merge(x, a, b, mod, g_norm, w_in, w_branch_a, w_branch_b, w_out, l)
        x = _ffn(x, mod, g_norm, ffn2_w13, ffn2_w2, g_final if l == DEPTH - 1 else None, l, 2)

    y_ctx, y_lat = x
    new_k, new_v = caches
    return (y_ctx.reshape(BATCH, SEQ, D_MODEL),
            y_lat.reshape(DEC_BATCH, DEC_SEQ, D_MODEL),
            new_k.reshape(BATCH, DEPTH, SEQ, N_HEADS, 2 * HEAD_DIM),
            new_v.reshape(BATCH, DEPTH, SEQ, N_HEADS, V_DIM))
```

```python
import functools
import math

import numpy as np
import jax
import jax.numpy as jnp
from jax import lax
from jax.experimental import pallas as pl
from jax.experimental.pallas import tpu as pltpu

D_MODEL = 1024
BATCH = 16
SEQ = 256
DEPTH = 4
DEC_BATCH = 2
DEC_SEQ = 1024
PAST_LEN = 256
GRID_W = 64
CHUNK = 128
N_GROUPS_A = 4
GROUP_W = D_MODEL // N_GROUPS_A
N_HEADS = 8
HEAD_DIM = D_MODEL // (2 * N_HEADS)
V_DIM = 2 * HEAD_DIM
D_FF = ((8 * D_MODEL // 3 + 127) // 128) * 128
IN_W = 7 * D_MODEL
N_MOD = 9
ROPE_THETA = 10000.0
EPS = 1e-6

N_CTX = BATCH * SEQ
N_LAT = DEC_BATCH * DEC_SEQ
N_TOK = N_CTX + N_LAT
COND_ROWS = 8

ROW_TILE = 512
SGU_SUB = 256
FFN_TILE = 512
FF_CHUNK = 256
MOD_TILE = 2304
Q_TILE = 512
V7X_VMEM_BYTES = 64 * 1024 * 1024
VMEM_LIMIT_BYTES = V7X_VMEM_BYTES - 4 * 1024 * 1024

F32 = jnp.float32
BF16 = jnp.bfloat16


def _params():
    return pltpu.CompilerParams(vmem_limit_bytes=VMEM_LIMIT_BYTES)


def _resident(block_shape, index_map):
    return pl.BlockSpec(block_shape, index_map, pipeline_mode=pl.Buffered(1))


def _dot(a, b):
    return jnp.dot(a, b, preferred_element_type=F32)


def _dot_nt(a, b):
    return lax.dot_general(a, b, (((1,), (1,)), ((), ())), preferred_element_type=F32)


def _rms(x):
    return x * lax.rsqrt(jnp.mean(x * x, axis=-1, keepdims=True) + EPS)


def _norm_mod(x, g, shift, scale):
    return _rms(x) * g * (1.0 + scale) + shift


def _stacked_mod_row(i, tile=ROW_TILE):
    n_ctx_tiles = N_CTX // tile
    return jnp.where(i < n_ctx_tiles, 0, 1 + (i - n_ctx_tiles) // (DEC_SEQ // tile))


def _mod_chunk(mod_ref, row, k):
    return mod_ref[pl.ds(row, 1), k * D_MODEL:(k + 1) * D_MODEL]


def _mod_kernel(c_ref, w_ref, b_ref, o_ref):
    c = c_ref[...]
    o_ref[...] = _dot(c * jax.nn.sigmoid(c), w_ref[...]) + b_ref[...]


def _modulation(cond, w_mod, b_mod):
    n = N_MOD * D_MODEL
    return pl.pallas_call(
        _mod_kernel,
        grid=(DEPTH, n // MOD_TILE),
        in_specs=[
            pl.BlockSpec((COND_ROWS, D_MODEL), lambda l, j: (0, 0)),
            pl.BlockSpec((None, D_MODEL, MOD_TILE), lambda l, j: (l, 0, j)),
            pl.BlockSpec((None, 1, MOD_TILE), lambda l, j: (l, 0, j)),
        ],
        out_specs=pl.BlockSpec((None, COND_ROWS, MOD_TILE), lambda l, j: (l, 0, j)),
        out_shape=jax.ShapeDtypeStruct((DEPTH, COND_ROWS, n), F32),
        compiler_params=_params(),
        name="modulation",
    )(cond, w_mod, b_mod.reshape(DEPTH, 1, n))


def _ffn_kernel(*refs, sub, split_in, final):
    n_x = 2 if split_in else 1
    x_refs, (mod_ref, g_ref, w13_ref, w2_ref), rest = refs[:n_x], refs[n_x:n_x + 4], refs[n_x + 4:]
    i = pl.program_id(0)
    is_ctx = i < N_CTX // FFN_TILE
    row = _stacked_mod_row(i, FFN_TILE)
    x = jnp.where(is_ctx, x_refs[0][...], x_refs[1][...]) if split_in else x_refs[0][...]
    h = _norm_mod(x, g_ref[sub:sub + 1, :], _mod_chunk(mod_ref, row, 3 * sub),
                  _mod_chunk(mod_ref, row, 3 * sub + 1))

    def gate_up(lo):
        return _dot(h, w13_ref[:, lo:lo + FF_CHUNK]), _dot(h, w13_ref[:, D_FF + lo:D_FF + lo + FF_CHUNK])

    acc = jnp.zeros((FFN_TILE, D_MODEL), F32)
    nxt = gate_up(0)
    for lo in range(0, D_FF, FF_CHUNK):
        gate, up = nxt
        if lo + FF_CHUNK < D_FF:
            nxt = gate_up(lo + FF_CHUNK)
        acc = acc + _dot(gate * jax.nn.sigmoid(gate) * up, w2_ref[lo:lo + FF_CHUNK, :])
    y = x + 0.5 * _mod_chunk(mod_ref, row, 3 * sub + 2) * acc
    if not final:
        rest[0][...] = y
        return
    gf_ref, ctx_ref, lat_ref = rest
    y = _rms(y) * gf_ref[...]

    @pl.when(is_ctx)
    def _():
        ctx_ref[...] = y

    @pl.when(jnp.logical_not(is_ctx))
    def _():
        lat_ref[...] = y


def _ffn(xs, mod, g_norm, w13, w2, g_final, l, sub):
    split_in, final = isinstance(xs, tuple), g_final is not None
    n_ctx_tiles = N_CTX // FFN_TILE
    tile = pl.BlockSpec((FFN_TILE, D_MODEL), lambda i: (i, 0))
    ctx_tile = pl.BlockSpec((FFN_TILE, D_MODEL), lambda i: (jnp.minimum(i, n_ctx_tiles - 1), 0))
    lat_tile = pl.BlockSpec((FFN_TILE, D_MODEL), lambda i: (jnp.maximum(i - n_ctx_tiles, 0), 0))
    in_specs = ([ctx_tile, lat_tile] if split_in else [tile]) + [
        _resident((None, COND_ROWS, N_MOD * D_MODEL), lambda i: (l, 0, 0)),
        _resident((None, 3, D_MODEL), lambda i: (l, 0, 0)),
        _resident((None, D_MODEL, 2 * D_FF), lambda i: (l, 0, 0)),
        _resident((None, D_FF, D_MODEL), lambda i: (l, 0, 0)),
    ]
    args = (list(xs) if split_in else [xs]) + [mod, g_norm, w13, w2]
    if final:
        in_specs.append(_resident((1, D_MODEL), lambda i: (0, 0)))
        args.append(g_final.reshape(1, D_MODEL))
        out_specs = [ctx_tile, lat_tile]
        out_shape = [jax.ShapeDtypeStruct((N_CTX, D_MODEL), F32), jax.ShapeDtypeStruct((N_LAT, D_MODEL), F32)]
    else:
        out_specs, out_shape = tile, jax.ShapeDtypeStruct((N_TOK, D_MODEL), F32)
    return pl.pallas_call(
        functools.partial(_ffn_kernel, sub=sub, split_in=split_in, final=final),
        grid=(N_TOK // FFN_TILE,),
        in_specs=in_specs,
        out_specs=out_specs,
        out_shape=out_shape,
        input_output_aliases={} if split_in or final else {0: 0},
        compiler_params=_params(),
        name="ffn",
    )(*args)


def _gelu_tanh(x):
    return x * (0.5 * (1.0 + jnp.tanh(math.sqrt(2.0 / math.pi) * (x + 0.044715 * (x * x * x)))))


def _sgu_kernel(x_ref, mod_ref, g_ref, wz_ref, gain_ref, ws_ref, bias_ref, o_ref, *, l):
    row = _stacked_mod_row(pl.program_id(0))

    def gated(r0):
        h = _norm_mod(x_ref[r0:r0 + SGU_SUB, :], g_ref[1:2, :], _mod_chunk(mod_ref, row, 3),
                      _mod_chunk(mod_ref, row, 4))
        return _gelu_tanh(_dot(h, wz_ref[...]))

    def mix(r0, z):
        u = z[:, :D_MODEL]
        v = _rms(z[:, D_MODEL:]) * gain_ref[l:l + 1, :]
        for c in range(0, SGU_SUB, CHUNK):
            for g in range(N_GROUPS_A):
                c0 = g * GROUP_W
                mixed = _dot(ws_ref[l, g], v[c:c + CHUNK, c0:c0 + GROUP_W]) + bias_ref[:, c0:c0 + GROUP_W]
                o_ref[r0 + c:r0 + c + CHUNK, c0:c0 + GROUP_W] = (
                    u[c:c + CHUNK, c0:c0 + GROUP_W] * mixed).astype(BF16)

    starts = list(range(0, ROW_TILE, SGU_SUB))
    nxt = gated(starts[0])
    for i, r0 in enumerate(starts):
        z = nxt
        if i + 1 < len(starts):
            nxt = gated(starts[i + 1])
        mix(r0, z)


def _sgu(x, mod, g_norm, w_in, sgu_gain, w_spatial, bias_tile, l):
    return pl.pallas_call(
        functools.partial(_sgu_kernel, l=l),
        grid=(N_TOK // ROW_TILE,),
        in_specs=[
            pl.BlockSpec((ROW_TILE, D_MODEL), lambda i: (i, 0)),
            _resident((None, COND_ROWS, N_MOD * D_MODEL), lambda i: (l, 0, 0)),
            _resident((None, 3, D_MODEL), lambda i: (l, 0, 0)),
            _resident((None, D_MODEL, 2 * D_MODEL), lambda i: (l, 0, 0)),
            _resident((DEPTH, D_MODEL), lambda i: (0, 0)),
            _resident((DEPTH, N_GROUPS_A, CHUNK, CHUNK), lambda i: (0, 0, 0, 0)),
            _resident((None, CHUNK, D_MODEL), lambda i: (l, 0, 0)),
        ],
        out_specs=pl.BlockSpec((ROW_TILE, D_MODEL), lambda i: (i, 0)),
        out_shape=jax.ShapeDtypeStruct((N_TOK, D_MODEL), BF16),
        compiler_params=_params(),
        name="sgu",
    )(x, mod, g_norm, w_in, sgu_gain, w_spatial, bias_tile)


def _rope_tables():
    pos = np.arange(DEC_SEQ)
    lane = np.arange(V_DIM)
    half = HEAD_DIM // 2
    within = lane % half
    coord = np.where((lane % HEAD_DIM) < half, pos[:, None] // GRID_W, pos[:, None] % GRID_W)
    freqs = ROPE_THETA ** (-np.arange(0, half, 2, dtype=np.float64) / half)
    ang = coord * freqs[within % (half // 2)][None, :]
    first = (within < half // 2)[None, :]
    cos = np.cos(ang)
    sin_lo = np.where(first, -np.sin(ang), 0.0)
    sin_hi = np.where(first, 0.0, np.sin(ang))
    return tuple(jnp.asarray(t, dtype=F32) for t in (cos, sin_lo, sin_hi))


def _qkv_kernel(x_ref, mod_ref, g_ref, wq_ref, wk_ref, wv_ref, *rest, latent, n_extra):
    extra, outs = rest[:n_extra], rest[n_extra:]
    qkv_ref = outs[0]
    row = 1 + pl.program_id(0) // (DEC_SEQ // ROW_TILE) if latent else 0
    h = _norm_mod(x_ref[...], g_ref[1:2, :], _mod_chunk(mod_ref, row, 3), _mod_chunk(mod_ref, row, 4))
    q = _dot(h, wq_ref[...]) * (HEAD_DIM ** -0.5)
    k = _dot(h, wk_ref[...])
    v = _dot(h, wv_ref[...])
    qkv_ref[:, 2 * D_MODEL:] = v.astype(BF16)
    if not latent:
        kc_ref, vc_ref = outs[1:]
        kc_ref[...] = k.reshape(kc_ref.shape)
        vc_ref[...] = v.reshape(vc_ref.shape)
        qkv_ref[:, :D_MODEL] = q.astype(BF16)
        qkv_ref[:, D_MODEL:2 * D_MODEL] = k.astype(BF16)
        return
    cos, sin_lo, sin_hi = (r[...] for r in extra)
    for c0 in range(0, D_MODEL, V_DIM):
        for src, base in ((q, 0), (k, D_MODEL)):
            t = src[:, c0:c0 + V_DIM]
            t = (t * cos + pltpu.roll(t, V_DIM - HEAD_DIM // 4, 1) * sin_lo
                 + pltpu.roll(t, HEAD_DIM // 4, 1) * sin_hi)
            qkv_ref[:, base + c0:base + c0 + V_DIM] = t.astype(BF16)


def _qkv(x, mod, g_norm, w_in, l, latent, rope=None, caches=None):
    rows, tile0 = (N_LAT, N_CTX // ROW_TILE) if latent else (N_CTX, 0)
    col0 = 2
    in_specs = [
        pl.BlockSpec((ROW_TILE, D_MODEL), lambda i: (tile0 + i, 0)),
        _resident((None, COND_ROWS, N_MOD * D_MODEL), lambda i: (l, 0, 0)),
        _resident((None, 3, D_MODEL), lambda i: (l, 0, 0)),
    ] + [_resident((None, D_MODEL, D_MODEL), functools.partial(lambda i, c: (l, 0, c), c=col0 + j))
         for j in range(3)]
    args = [x, mod, g_norm, w_in, w_in, w_in]
    out_shape = [jax.ShapeDtypeStruct((rows, 3 * D_MODEL), BF16)]
    out_specs = [pl.BlockSpec((ROW_TILE, 3 * D_MODEL), lambda i: (i, 0))]
    aliases = {}
    if latent:
        tiles = DEC_SEQ // ROW_TILE
        in_specs += [pl.BlockSpec((ROW_TILE, V_DIM), lambda i: (i % tiles, 0))] * 3
        args += list(rope)
    else:
        seqs = ROW_TILE // SEQ
        cache = jax.ShapeDtypeStruct((BATCH, DEPTH, SEQ, N_HEADS, V_DIM), F32)
        out_shape += [cache, cache]
        out_specs += [pl.BlockSpec((seqs, None, SEQ, N_HEADS, V_DIM), lambda i: (i, l, 0, 0, 0))] * 2
        if caches is not None:
            in_specs += [pl.BlockSpec(memory_space=pl.ANY)] * 2
            args += list(caches)
            aliases = {len(args) - 2: 1, len(args) - 1: 2}
    return pl.pallas_call(
        functools.partial(_qkv_kernel, latent=latent, n_extra=len(args) - 6),
        grid=(rows // ROW_TILE,),
        in_specs=in_specs,
        out_specs=out_specs,
        out_shape=out_shape,
        input_output_aliases=aliases,
        compiler_params=_params(),
        name="qkv_latent" if latent else "qkv_context",
    )(*args)


def _lam_full(lam_ref, l):
    lp = lam_ref[l]
    s01 = jnp.sum(lp[0:1, :] * lp[1:2, :], axis=-1, keepdims=True)
    s23 = jnp.sum(lp[2:3, :] * lp[3:4, :], axis=-1, keepdims=True)
    return jnp.exp(s01) - jnp.exp(s23) + _lam_init(l)


def _lam_init(l):
    return 0.8 - 0.6 * math.exp(-0.3 * l)


def _attn_scores(q, ks):
    lane = lax.broadcasted_iota(jnp.int32, (1, V_DIM), 1)
    lo = (lane < HEAD_DIM).astype(BF16)
    qs = jnp.concatenate([q * lo, q * (1 - lo)], axis=0)
    return [_dot_nt(qs, k) for k in ks]


def _attn_combine(s, vs, lam_full, gain, l):
    n_q = s[0].shape[0] // 2
    m = functools.reduce(jnp.maximum, [jnp.max(t, axis=-1, keepdims=True) for t in s])
    pv = functools.reduce(jnp.add, [
        _dot(jnp.exp(t - m).astype(BF16), jnp.concatenate([v, jnp.ones_like(v)], axis=1))
        for t, v in zip(s, vs)])
    pv = pv[:, :V_DIM] / pv[:, V_DIM:]
    o = pv[:n_q] - lam_full * pv[n_q:]
    return _rms(o) * gain * (1.0 - _lam_init(l))


def _attn_heads(q_ref, k_refs, v_refs, lam_ref, gain_ref, o_ref, l):
    lam_full = _lam_full(lam_ref, l)
    gain = gain_ref[l:l + 1, :]

    def scores(hd):
        cs = slice(hd * V_DIM, (hd + 1) * V_DIM)
        return _attn_scores(q_ref[:, cs], [r[:, cs].astype(BF16) for r in k_refs])

    nxt = scores(0)
    for hd in range(N_HEADS):
        cs = slice(hd * V_DIM, (hd + 1) * V_DIM)
        s = nxt
        if hd + 1 < N_HEADS:
            nxt = scores(hd + 1)
        o = _attn_combine(s, [r[:, cs].astype(BF16) for r in v_refs], lam_full, gain, l)
        o_ref[:, cs] = o.astype(BF16)


def _attn_ctx_kernel(q_ref, k_ref, v_ref, lam_ref, gain_ref, o_ref, *, l):
    _attn_heads(q_ref, [k_ref], [v_ref], lam_ref, gain_ref, o_ref, l)


def _attn_lat_kernel(q_ref, k_ref, v_ref, past_k, past_v, lam_ref, gain_ref, _, o_ref, pk_ref, pv_ref, *, l):
    pk_ref[...] = past_k[...].reshape(PAST_LEN, D_MODEL).astype(BF16)
    pv_ref[...] = past_v[...].reshape(PAST_LEN, D_MODEL).astype(BF16)
    _attn_heads(q_ref, [pk_ref, k_ref], [pv_ref, v_ref], lam_ref, gain_ref, o_ref, l)


def _attn_ctx(qkv, lam, subln_gain, l):
    part = [pl.BlockSpec((SEQ, D_MODEL), functools.partial(lambda i, c: (i, c), c=c)) for c in range(3)]
    return pl.pallas_call(
        functools.partial(_attn_ctx_kernel, l=l),
        grid=(BATCH,),
        in_specs=part + [_resident((DEPTH, 4, HEAD_DIM), lambda i: (0, 0, 0)),
                         _resident((DEPTH, V_DIM), lambda i: (0, 0))],
        out_specs=pl.BlockSpec((SEQ, D_MODEL), lambda i: (i, 0)),
        out_shape=jax.ShapeDtypeStruct((N_TOK, D_MODEL), BF16),
        compiler_params=_params(),
        name="attn_context",
    )(qkv, qkv, qkv, lam, subln_gain)


def _attn_lat(qkv, past_k, past_v, lam, subln_gain, b_all, l):
    tiles = DEC_SEQ // Q_TILE
    qspec = pl.BlockSpec((Q_TILE, D_MODEL), lambda b, j: (b * tiles + j, 0))
    kspec = pl.BlockSpec((DEC_SEQ, D_MODEL), lambda b, j: (b, 1))
    vspec = pl.BlockSpec((DEC_SEQ, D_MODEL), lambda b, j: (b, 2))
    past = pl.BlockSpec((None, None, PAST_LEN, N_HEADS, V_DIM), lambda b, j: (b, l, 0, 0, 0))
    return pl.pallas_call(
        functools.partial(_attn_lat_kernel, l=l),
        grid=(DEC_BATCH, tiles),
        in_specs=[qspec, kspec, vspec, past, past,
                  _resident((DEPTH, 4, HEAD_DIM), lambda b, j: (0, 0, 0)),
                  _resident((DEPTH, V_DIM), lambda b, j: (0, 0)),
                  pl.BlockSpec(memory_space=pl.ANY)],
        out_specs=pl.BlockSpec((Q_TILE, D_MODEL), lambda b, j: (N_CTX // Q_TILE + b * tiles + j, 0)),
        out_shape=jax.ShapeDtypeStruct((N_TOK, D_MODEL), BF16),
        scratch_shapes=[pltpu.VMEM((PAST_LEN, D_MODEL), BF16)] * 2,
        input_output_aliases={7: 0},
        compiler_params=_params(),
        name="attn_latent",
    )(qkv, qkv, qkv, past_k, past_v, lam, subln_gain, b_all)


def _merge_kernel(x_ref, a_ref, b_ref, mod_ref, g_ref, wga_ref, wgb_ref, wa_ref, wb_ref, wo_ref, o_ref):
    row = _stacked_mod_row(pl.program_id(0))
    x = x_ref[...]
    h = _norm_mod(x, g_ref[1:2, :], _mod_chunk(mod_ref, row, 3), _mod_chunk(mod_ref, row, 4))
    merged = (jax.nn.sigmoid(_dot(h, wga_ref[...])) * _dot(a_ref[...].astype(F32), wa_ref[...])
              + jax.nn.sigmoid(_dot(h, wgb_ref[...])) * _dot(b_ref[...].astype(F32), wb_ref[...]))
    o_ref[...] = x + _mod_chunk(mod_ref, row, 5) * _dot(merged, wo_ref[...])


def _merge(x, a, b, mod, g_norm, w_in, w_branch_a, w_branch_b, w_out, l):
    tile = pl.BlockSpec((ROW_TILE, D_MODEL), lambda i: (i, 0))
    square = _resident((None, D_MODEL, D_MODEL), lambda i: (l, 0, 0))
    return pl.pallas_call(
        _merge_kernel,
        grid=(N_TOK // ROW_TILE,),
        in_specs=[tile, tile, tile,
                  _resident((None, COND_ROWS, N_MOD * D_MODEL), lambda i: (l, 0, 0)),
                  _resident((None, 3, D_MODEL), lambda i: (l, 0, 0)),
                  _resident((None, D_MODEL, D_MODEL), lambda i: (l, 0, 5)),
                  _resident((None, D_MODEL, D_MODEL), lambda i: (l, 0, 6)),
                  square, square, square],
        out_specs=tile,
        out_shape=jax.ShapeDtypeStruct((N_TOK, D_MODEL), F32),
        input_output_aliases={0: 0},
        compiler_params=_params(),
        name="merge",
    )(x, a, b, mod, g_norm, w_in, w_in, w_branch_a, w_branch_b, w_out)


def kernel(x_prompt, x_sample, cache_k, cache_v, c, c_ctx, w_mod, b_mod, g_norm, ffn1_w13, ffn1_w2, w_in,
           sgu_gain, w_spatial, b_spatial, lam, subln_gain, w_branch_a, w_branch_b, w_out, ffn2_w13,
           ffn2_w2, g_final):
    cond = jnp.concatenate(
        [c_ctx[None, :], c, jnp.zeros((COND_ROWS - 1 - DEC_BATCH, D_MODEL), F32)], axis=0)
    mod = _modulation(cond, w_mod, b_mod)
    bias_tile = jnp.repeat(jnp.swapaxes(b_spatial, 1, 2), GROUP_W, axis=2)
    rope = _rope_tables()

    x = (x_prompt.reshape(N_CTX, D_MODEL), x_sample.reshape(N_LAT, D_MODEL))
    caches = None
    for l in range(DEPTH):
        x = _ffn(x, mod, g_norm, ffn1_w13, ffn1_w2, None, l, 0)
        a = _sgu(x, mod, g_norm, w_in, sgu_gain, w_spatial, bias_tile, l)
        qkv_ctx, *caches = _qkv(x, mod, g_norm, w_in, l, latent=False, caches=caches)
        qkv_lat, = _qkv(x, mod, g_norm, w_in, l, latent=True, rope=rope)
        b = _attn_ctx(qkv_ctx, lam, subln_gain, l)
        b = _attn_lat(qkv_lat, cache_k, cache_v, lam, subln_gain, b, l)
        x = _merge(x, a, b, mod, g_norm, w_in, w_branch_a, w_branch_b, w_out, l)
        x = _ffn(x, mod, g_norm, ffn2_w13, ffn2_w2, g_final if l == DEPTH - 1 else None, l, 2)

    y_ctx, y_lat = x
    new_k, new_v = caches
    return y_ctx.reshape(BATCH, SEQ, D_MODEL), y_lat.reshape(DEC_BATCH, DEC_SEQ, D_MODEL), new_k, new_v
```

```python
import functools
import math

import numpy as np
import jax
import jax.numpy as jnp
from jax import lax
from jax.experimental import pallas as pl
from jax.experimental.pallas import tpu as pltpu

D_MODEL = 1024
BATCH = 16
SEQ = 256
DEPTH = 4
DEC_BATCH = 2
DEC_SEQ = 1024
PAST_LEN = 256
GRID_W = 64
CHUNK = 128
N_GROUPS_A = 4
GROUP_W = D_MODEL // N_GROUPS_A
N_HEADS = 8
HEAD_DIM = D_MODEL // (2 * N_HEADS)
V_DIM = 2 * HEAD_DIM
D_FF = ((8 * D_MODEL // 3 + 127) // 128) * 128
IN_W = 7 * D_MODEL
N_MOD = 9
ROPE_THETA = 10000.0
EPS = 1e-6

N_CTX = BATCH * SEQ
N_LAT = DEC_BATCH * DEC_SEQ
N_TOK = N_CTX + N_LAT
COND_ROWS = 8

ROW_TILE = 512
SGU_SUB = 256
FFN_TILE = 512
FF_CHUNK = 256
MOD_TILE = 2304
Q_TILE = 512
CTX_SEQS = 4
V7X_VMEM_BYTES = 64 * 1024 * 1024
VMEM_LIMIT_BYTES = V7X_VMEM_BYTES - 4 * 1024 * 1024

F32 = jnp.float32
BF16 = jnp.bfloat16


def _params():
    return pltpu.CompilerParams(vmem_limit_bytes=VMEM_LIMIT_BYTES)


def _resident(block_shape, index_map):
    return pl.BlockSpec(block_shape, index_map, pipeline_mode=pl.Buffered(1))


def _dot(a, b):
    return jnp.dot(a, b, preferred_element_type=F32)


def _dot_nt(a, b):
    return lax.dot_general(a, b, (((1,), (1,)), ((), ())), preferred_element_type=F32)


def _rms(x):
    return x * lax.rsqrt(jnp.mean(x * x, axis=-1, keepdims=True) + EPS)


def _norm_mod(x, g, shift, scale):
    return _rms(x) * g * (1.0 + scale) + shift


def _stacked_mod_row(i, tile=ROW_TILE):
    n_ctx_tiles = N_CTX // tile
    return jnp.where(i < n_ctx_tiles, 0, 1 + (i - n_ctx_tiles) // (DEC_SEQ // tile))


def _mod_chunk(mod_ref, row, k):
    return mod_ref[pl.ds(row, 1), k * D_MODEL:(k + 1) * D_MODEL]


def _mod_kernel(c_ref, w_ref, b_ref, o_ref):
    c = c_ref[...]
    o_ref[...] = _dot(c * jax.nn.sigmoid(c), w_ref[...]) + b_ref[...]


def _modulation(cond, w_mod, b_mod):
    n = N_MOD * D_MODEL
    return pl.pallas_call(
        _mod_kernel,
        grid=(DEPTH, n // MOD_TILE),
        in_specs=[
            pl.BlockSpec((COND_ROWS, D_MODEL), lambda l, j: (0, 0)),
            pl.BlockSpec((None, D_MODEL, MOD_TILE), lambda l, j: (l, 0, j)),
            pl.BlockSpec((None, 1, MOD_TILE), lambda l, j: (l, 0, j)),
        ],
        out_specs=pl.BlockSpec((None, COND_ROWS, MOD_TILE), lambda l, j: (l, 0, j)),
        out_shape=jax.ShapeDtypeStruct((DEPTH, COND_ROWS, n), F32),
        compiler_params=_params(),
        name="modulation",
    )(cond, w_mod, b_mod.reshape(DEPTH, 1, n))


def _ffn_kernel(*refs, sub, split_in, final):
    n_x = 2 if split_in else 1
    x_refs, (mod_ref, g_ref, w13_ref, w2_ref), rest = refs[:n_x], refs[n_x:n_x + 4], refs[n_x + 4:]
    i = pl.program_id(0)
    is_ctx = i < N_CTX // FFN_TILE
    row = _stacked_mod_row(i, FFN_TILE)
    x = jnp.where(is_ctx, x_refs[0][...], x_refs[1][...]) if split_in else x_refs[0][...]
    h = _norm_mod(x, g_ref[sub:sub + 1, :], _mod_chunk(mod_ref, row, 3 * sub),
                  _mod_chunk(mod_ref, row, 3 * sub + 1))

    def gate_up(lo):
        return _dot(h, w13_ref[:, lo:lo + FF_CHUNK]), _dot(h, w13_ref[:, D_FF + lo:D_FF + lo + FF_CHUNK])

    acc = jnp.zeros((FFN_TILE, D_MODEL), F32)
    nxt = gate_up(0)
    for lo in range(0, D_FF, FF_CHUNK):
        gate, up = nxt
        if lo + FF_CHUNK < D_FF:
            nxt = gate_up(lo + FF_CHUNK)
        acc = acc + _dot(gate * jax.nn.sigmoid(gate) * up, w2_ref[lo:lo + FF_CHUNK, :])
    y = x + 0.5 * _mod_chunk(mod_ref, row, 3 * sub + 2) * acc
    if not final:
        rest[0][...] = y
        return
    gf_ref, ctx_ref, lat_ref = rest
    y = _rms(y) * gf_ref[...]

    @pl.when(is_ctx)
    def _():
        ctx_ref[...] = y

    @pl.when(jnp.logical_not(is_ctx))
    def _():
        lat_ref[...] = y


def _ffn(xs, mod, g_norm, w13, w2, g_final, l, sub):
    split_in, final = isinstance(xs, tuple), g_final is not None
    n_ctx_tiles = N_CTX // FFN_TILE
    tile = pl.BlockSpec((FFN_TILE, D_MODEL), lambda i: (i, 0))
    ctx_tile = pl.BlockSpec((FFN_TILE, D_MODEL), lambda i: (jnp.minimum(i, n_ctx_tiles - 1), 0))
    lat_tile = pl.BlockSpec((FFN_TILE, D_MODEL), lambda i: (jnp.maximum(i - n_ctx_tiles, 0), 0))
    in_specs = ([ctx_tile, lat_tile] if split_in else [tile]) + [
        _resident((None, COND_ROWS, N_MOD * D_MODEL), lambda i: (l, 0, 0)),
        _resident((None, 3, D_MODEL), lambda i: (l, 0, 0)),
        _resident((None, D_MODEL, 2 * D_FF), lambda i: (l, 0, 0)),
        _resident((None, D_FF, D_MODEL), lambda i: (l, 0, 0)),
    ]
    args = (list(xs) if split_in else [xs]) + [mod, g_norm, w13, w2]
    if final:
        in_specs.append(_resident((1, D_MODEL), lambda i: (0, 0)))
        args.append(g_final.reshape(1, D_MODEL))
        out_specs = [ctx_tile, lat_tile]
        out_shape = [jax.ShapeDtypeStruct((N_CTX, D_MODEL), F32), jax.ShapeDtypeStruct((N_LAT, D_MODEL), F32)]
    else:
        out_specs, out_shape = tile, jax.ShapeDtypeStruct((N_TOK, D_MODEL), F32)
    return pl.pallas_call(
        functools.partial(_ffn_kernel, sub=sub, split_in=split_in, final=final),
        grid=(N_TOK // FFN_TILE,),
        in_specs=in_specs,
        out_specs=out_specs,
        out_shape=out_shape,
        input_output_aliases={} if split_in or final else {0: 0},
        compiler_params=_params(),
        name="ffn",
    )(*args)


def _gelu_tanh(x):
    return x * (0.5 * (1.0 + jnp.tanh(math.sqrt(2.0 / math.pi) * (x + 0.044715 * (x * x * x)))))


def _sgu_kernel(x_ref, mod_ref, g_ref, wz_ref, gain_ref, ws_ref, bias_ref, o_ref, *, l):
    row = _stacked_mod_row(pl.program_id(0))

    def gated(r0):
        h = _norm_mod(x_ref[r0:r0 + SGU_SUB, :], g_ref[1:2, :], _mod_chunk(mod_ref, row, 3),
                      _mod_chunk(mod_ref, row, 4))
        return _gelu_tanh(_dot(h, wz_ref[...]))

    def mix(r0, z):
        u = z[:, :D_MODEL]
        v = _rms(z[:, D_MODEL:]) * gain_ref[l:l + 1, :]
        for c in range(0, SGU_SUB, CHUNK):
            for g in range(N_GROUPS_A):
                c0 = g * GROUP_W
                mixed = _dot(ws_ref[l, g], v[c:c + CHUNK, c0:c0 + GROUP_W]) + bias_ref[:, c0:c0 + GROUP_W]
                o_ref[r0 + c:r0 + c + CHUNK, c0:c0 + GROUP_W] = (
                    u[c:c + CHUNK, c0:c0 + GROUP_W] * mixed).astype(BF16)

    starts = list(range(0, ROW_TILE, SGU_SUB))
    nxt = gated(starts[0])
    for i, r0 in enumerate(starts):
        z = nxt
        if i + 1 < len(starts):
            nxt = gated(starts[i + 1])
        mix(r0, z)


def _sgu(x, mod, g_norm, w_in, sgu_gain, w_spatial, bias_tile, l):
    return pl.pallas_call(
        functools.partial(_sgu_kernel, l=l),
        grid=(N_TOK // ROW_TILE,),
        in_specs=[
            pl.BlockSpec((ROW_TILE, D_MODEL), lambda i: (i, 0)),
            _resident((None, COND_ROWS, N_MOD * D_MODEL), lambda i: (l, 0, 0)),
            _resident((None, 3, D_MODEL), lambda i: (l, 0, 0)),
            _resident((None, D_MODEL, 2 * D_MODEL), lambda i: (l, 0, 0)),
            _resident((DEPTH, D_MODEL), lambda i: (0, 0)),
            _resident((DEPTH, N_GROUPS_A, CHUNK, CHUNK), lambda i: (0, 0, 0, 0)),
            _resident((None, CHUNK, D_MODEL), lambda i: (l, 0, 0)),
        ],
        out_specs=pl.BlockSpec((ROW_TILE, D_MODEL), lambda i: (i, 0)),
        out_shape=jax.ShapeDtypeStruct((N_TOK, D_MODEL), BF16),
        compiler_params=_params(),
        name="sgu",
    )(x, mod, g_norm, w_in, sgu_gain, w_spatial, bias_tile)


def _rope_tables():
    pos = np.arange(DEC_SEQ)
    lane = np.arange(V_DIM)
    half = HEAD_DIM // 2
    within = lane % half
    coord = np.where((lane % HEAD_DIM) < half, pos[:, None] // GRID_W, pos[:, None] % GRID_W)
    freqs = ROPE_THETA ** (-np.arange(0, half, 2, dtype=np.float64) / half)
    ang = coord * freqs[within % (half // 2)][None, :]
    first = (within < half // 2)[None, :]
    cos = np.cos(ang)
    sin_lo = np.where(first, -np.sin(ang), 0.0)
    sin_hi = np.where(first, 0.0, np.sin(ang))
    return tuple(jnp.asarray(t, dtype=F32) for t in (cos, sin_lo, sin_hi))


def _qkv_kernel(x_ref, mod_ref, g_ref, wq_ref, wk_ref, wv_ref, *rest, latent, n_extra):
    extra, outs = rest[:n_extra], rest[n_extra:]
    qkv_ref = outs[0]
    row = 1 + pl.program_id(0) // (DEC_SEQ // ROW_TILE) if latent else 0
    h = _norm_mod(x_ref[...], g_ref[1:2, :], _mod_chunk(mod_ref, row, 3), _mod_chunk(mod_ref, row, 4))
    q = _dot(h, wq_ref[...]) * (HEAD_DIM ** -0.5)
    k = _dot(h, wk_ref[...])
    v = _dot(h, wv_ref[...])
    qkv_ref[:, 2 * D_MODEL:] = v.astype(BF16)
    if not latent:
        kc_ref, vc_ref = outs[1:]
        kc_ref[...] = k.reshape(kc_ref.shape)
        vc_ref[...] = v.reshape(vc_ref.shape)
        qkv_ref[:, :D_MODEL] = q.astype(BF16)
        qkv_ref[:, D_MODEL:2 * D_MODEL] = k.astype(BF16)
        return
    cos, sin_lo, sin_hi = (r[...] for r in extra)
    for c0 in range(0, D_MODEL, V_DIM):
        for src, base in ((q, 0), (k, D_MODEL)):
            t = src[:, c0:c0 + V_DIM]
            t = (t * cos + pltpu.roll(t, V_DIM - HEAD_DIM // 4, 1) * sin_lo
                 + pltpu.roll(t, HEAD_DIM // 4, 1) * sin_hi)
            qkv_ref[:, base + c0:base + c0 + V_DIM] = t.astype(BF16)


def _qkv(x, mod, g_norm, w_in, l, latent, rope=None, caches=None):
    rows, tile0 = (N_LAT, N_CTX // ROW_TILE) if latent else (N_CTX, 0)
    col0 = 2
    in_specs = [
        pl.BlockSpec((ROW_TILE, D_MODEL), lambda i: (tile0 + i, 0)),
        _resident((None, COND_ROWS, N_MOD * D_MODEL), lambda i: (l, 0, 0)),
        _resident((None, 3, D_MODEL), lambda i: (l, 0, 0)),
    ] + [_resident((None, D_MODEL, D_MODEL), functools.partial(lambda i, c: (l, 0, c), c=col0 + j))
         for j in range(3)]
    args = [x, mod, g_norm, w_in, w_in, w_in]
    out_shape = [jax.ShapeDtypeStruct((rows, 3 * D_MODEL), BF16)]
    out_specs = [pl.BlockSpec((ROW_TILE, 3 * D_MODEL), lambda i: (i, 0))]
    aliases = {}
    if latent:
        tiles = DEC_SEQ // ROW_TILE
        in_specs += [pl.BlockSpec((ROW_TILE, V_DIM), lambda i: (i % tiles, 0))] * 3
        args += list(rope)
    else:
        seqs = ROW_TILE // SEQ
        cache = jax.ShapeDtypeStruct((BATCH, DEPTH, SEQ, N_HEADS, V_DIM), F32)
        out_shape += [cache, cache]
        out_specs += [pl.BlockSpec((seqs, None, SEQ, N_HEADS, V_DIM), lambda i: (i, l, 0, 0, 0))] * 2
        if caches is not None:
            in_specs += [pl.BlockSpec(memory_space=pl.ANY)] * 2
            args += list(caches)
            aliases = {len(args) - 2: 1, len(args) - 1: 2}
    return pl.pallas_call(
        functools.partial(_qkv_kernel, latent=latent, n_extra=len(args) - 6),
        grid=(rows // ROW_TILE,),
        in_specs=in_specs,
        out_specs=out_specs,
        out_shape=out_shape,
        input_output_aliases=aliases,
        compiler_params=_params(),
        name="qkv_latent" if latent else "qkv_context",
    )(*args)


def _lam_full(lam_ref, l):
    lp = lam_ref[l]
    s01 = jnp.sum(lp[0:1, :] * lp[1:2, :], axis=-1, keepdims=True)
    s23 = jnp.sum(lp[2:3, :] * lp[3:4, :], axis=-1, keepdims=True)
    return jnp.exp(s01) - jnp.exp(s23) + _lam_init(l)


def _lam_init(l):
    return 0.8 - 0.6 * math.exp(-0.3 * l)


def _attn_scores(q, ks):
    lane = lax.broadcasted_iota(jnp.int32, (1, V_DIM), 1)
    lo = (lane < HEAD_DIM).astype(BF16)
    qs = jnp.concatenate([q * lo, q * (1 - lo)], axis=0)
    return [_dot_nt(qs, k) for k in ks]


def _attn_combine(s, vs, lam_full, gain, l):
    n_q = s[0].shape[0] // 2
    m = functools.reduce(jnp.maximum, [jnp.max(t, axis=-1, keepdims=True) for t in s])
    pv = functools.reduce(jnp.add, [
        _dot(jnp.exp(t - m).astype(BF16), jnp.concatenate([v, jnp.ones_like(v)], axis=1))
        for t, v in zip(s, vs)])
    pv = pv[:, :V_DIM] / pv[:, V_DIM:]
    o = pv[:n_q] - lam_full * pv[n_q:]
    return _rms(o) * gain * (1.0 - _lam_init(l))


def _attn_heads(q_ref, k_refs, v_refs, lam_ref, gain_ref, o_ref, l, n_seqs=1):
    lam_full = _lam_full(lam_ref, l)
    gain = gain_ref[l:l + 1, :]
    units = [(sq, hd) for hd in range(N_HEADS) for sq in range(n_seqs)]

    def part(ref, sq, hd):
        n = ref.shape[0] // n_seqs
        return ref[sq * n:(sq + 1) * n, hd * V_DIM:(hd + 1) * V_DIM].astype(BF16)

    def scores(sq, hd):
        return _attn_scores(part(q_ref, sq, hd), [part(r, sq, hd) for r in k_refs])

    nxt = scores(*units[0])
    for n, (sq, hd) in enumerate(units):
        s = nxt
        if n + 1 < len(units):
            nxt = scores(*units[n + 1])
        o = _attn_combine(s, [part(r, sq, hd) for r in v_refs], lam_full, gain, l)
        rows = q_ref.shape[0] // n_seqs
        o_ref[sq * rows:(sq + 1) * rows, hd * V_DIM:(hd + 1) * V_DIM] = o.astype(BF16)


def _attn_ctx_kernel(q_ref, k_ref, v_ref, lam_ref, gain_ref, o_ref, *, l):
    _attn_heads(q_ref, [k_ref], [v_ref], lam_ref, gain_ref, o_ref, l, n_seqs=CTX_SEQS)


def _attn_lat_kernel(q_ref, k_ref, v_ref, past_k, past_v, lam_ref, gain_ref, _, o_ref, pk_ref, pv_ref, *, l):
    pk_ref[...] = past_k[...].reshape(PAST_LEN, D_MODEL).astype(BF16)
    pv_ref[...] = past_v[...].reshape(PAST_LEN, D_MODEL).astype(BF16)
    _attn_heads(q_ref, [pk_ref, k_ref], [pv_ref, v_ref], lam_ref, gain_ref, o_ref, l)


def _attn_ctx(qkv, lam, subln_gain, l):
    rows = CTX_SEQS * SEQ
    part = [pl.BlockSpec((rows, D_MODEL), functools.partial(lambda i, c: (i, c), c=c)) for c in range(3)]
    return pl.pallas_call(
        functools.partial(_attn_ctx_kernel, l=l),
        grid=(BATCH // CTX_SEQS,),
        in_specs=part + [_resident((DEPTH, 4, HEAD_DIM), lambda i: (0, 0, 0)),
                         _resident((DEPTH, V_DIM), lambda i: (0, 0))],
        out_specs=pl.BlockSpec((rows, D_MODEL), lambda i: (i, 0)),
        out_shape=jax.ShapeDtypeStruct((N_TOK, D_MODEL), BF16),
        compiler_params=_params(),
        name="attn_context",
    )(qkv, qkv, qkv, lam, subln_gain)


def _attn_lat(qkv, past_k, past_v, lam, subln_gain, b_all, l):
    tiles = DEC_SEQ // Q_TILE
    qspec = pl.BlockSpec((Q_TILE, D_MODEL), lambda b, j: (b * tiles + j, 0))
    kspec = pl.BlockSpec((DEC_SEQ, D_MODEL), lambda b, j: (b, 1))
    vspec = pl.BlockSpec((DEC_SEQ, D_MODEL), lambda b, j: (b, 2))
    past = pl.BlockSpec((None, None, PAST_LEN, N_HEADS, V_DIM), lambda b, j: (b, l, 0, 0, 0))
    return pl.pallas_call(
        functools.partial(_attn_lat_kernel, l=l),
        grid=(DEC_BATCH, tiles),
        in_specs=[qspec, kspec, vspec, past, past,
                  _resident((DEPTH, 4, HEAD_DIM), lambda b, j: (0, 0, 0)),
                  _resident((DEPTH, V_DIM), lambda b, j: (0, 0)),
                  pl.BlockSpec(memory_space=pl.ANY)],
        out_specs=pl.BlockSpec((Q_TILE, D_MODEL), lambda b, j: (N_CTX // Q_TILE + b * tiles + j, 0)),
        out_shape=jax.ShapeDtypeStruct((N_TOK, D_MODEL), BF16),
        scratch_shapes=[pltpu.VMEM((PAST_LEN, D_MODEL), BF16)] * 2,
        input_output_aliases={7: 0},
        compiler_params=_params(),
        name="attn_latent",
    )(qkv, qkv, qkv, past_k, past_v, lam, subln_gain, b_all)


def _merge_kernel(x_ref, a_ref, b_ref, mod_ref, g_ref, wga_ref, wgb_ref, wa_ref, wb_ref, wo_ref, o_ref):
    row = _stacked_mod_row(pl.program_id(0))
    x = x_ref[...]
    h = _norm_mod(x, g_ref[1:2, :], _mod_chunk(mod_ref, row, 3), _mod_chunk(mod_ref, row, 4))
    merged = (jax.nn.sigmoid(_dot(h, wga_ref[...])) * _dot(a_ref[...].astype(F32), wa_ref[...])
              + jax.nn.sigmoid(_dot(h, wgb_ref[...])) * _dot(b_ref[...].astype(F32), wb_ref[...]))
    o_ref[...] = x + _mod_chunk(mod_ref, row, 5) * _dot(merged, wo_ref[...])


def _merge(x, a, b, mod, g_norm, w_in, w_branch_a, w_branch_b, w_out, l):
    tile = pl.BlockSpec((ROW_TILE, D_MODEL), lambda i: (i, 0))
    square = _resident((None, D_MODEL, D_MODEL), lambda i: (l, 0, 0))
    return pl.pallas_call(
        _merge_kernel,
        grid=(N_TOK // ROW_TILE,),
        in_specs=[tile, tile, tile,
                  _resident((None, COND_ROWS, N_MOD * D_MODEL), lambda i: (l, 0, 0)),
                  _resident((None, 3, D_MODEL), lambda i: (l, 0, 0)),
                  _resident((None, D_MODEL, D_MODEL), lambda i: (l, 0, 5)),
                  _resident((None, D_MODEL, D_MODEL), lambda i: (l, 0, 6)),
                  square, square, square],
        out_specs=tile,
        out_shape=jax.ShapeDtypeStruct((N_TOK, D_MODEL), F32),
        input_output_aliases={0: 0},
        compiler_params=_params(),
        name="merge",
    )(x, a, b, mod, g_norm, w_in, w_in, w_branch_a, w_branch_b, w_out)


def kernel(x_prompt, x_sample, cache_k, cache_v, c, c_ctx, w_mod, b_mod, g_norm, ffn1_w13, ffn1_w2, w_in,
           sgu_gain, w_spatial, b_spatial, lam, subln_gain, w_branch_a, w_branch_b, w_out, ffn2_w13,
           ffn2_w2, g_final):
    cond = jnp.concatenate(
        [c_ctx[None, :], c, jnp.zeros((COND_ROWS - 1 - DEC_BATCH, D_MODEL), F32)], axis=0)
    mod = _modulation(cond, w_mod, b_mod)
    bias_tile = jnp.repeat(jnp.swapaxes(b_spatial, 1, 2), GROUP_W, axis=2)
    rope = _rope_tables()

    x = (x_prompt.reshape(N_CTX, D_MODEL), x_sample.reshape(N_LAT, D_MODEL))
    caches = None
    for l in range(DEPTH):
        x = _ffn(x, mod, g_norm, ffn1_w13, ffn1_w2, None, l, 0)
        a = _sgu(x, mod, g_norm, w_in, sgu_gain, w_spatial, bias_tile, l)
        qkv_ctx, *caches = _qkv(x, mod, g_norm, w_in, l, latent=False, caches=caches)
        qkv_lat, = _qkv(x, mod, g_norm, w_in, l, latent=True, rope=rope)
        b = _attn_ctx(qkv_ctx, lam, subln_gain, l)
        b = _attn_lat(qkv_lat, cache_k, cache_v, lam, subln_gain, b, l)
        x = _merge(x, a, b, mod, g_norm, w_in, w_branch_a, w_branch_b, w_out, l)
        x = _ffn(x, mod, g_norm, ffn2_w13, ffn2_w2, g_final if l == DEPTH - 1 else None, l, 2)

    y_ctx, y_lat = x
    new_k, new_v = caches
    return y_ctx.reshape(BATCH, SEQ, D_MODEL), y_lat.reshape(DEC_BATCH, DEC_SEQ, D_MODEL), new_k, new_v
```

```python
import functools
import math

import numpy as np
import jax
import jax.numpy as jnp
from jax import lax
from jax.experimental import pallas as pl
from jax.experimental.pallas import tpu as pltpu

D_MODEL = 1024
BATCH = 16
SEQ = 256
DEPTH = 4
DEC_BATCH = 2
DEC_SEQ = 1024
PAST_LEN = 256
GRID_W = 64
CHUNK = 128
N_GROUPS_A = 4
GROUP_W = D_MODEL // N_GROUPS_A
N_HEADS = 8
HEAD_DIM = D_MODEL // (2 * N_HEADS)
V_DIM = 2 * HEAD_DIM
D_FF = ((8 * D_MODEL // 3 + 127) // 128) * 128
IN_W = 7 * D_MODEL
N_MOD = 9
ROPE_THETA = 10000.0
EPS = 1e-6

N_CTX = BATCH * SEQ
N_LAT = DEC_BATCH * DEC_SEQ
N_TOK = N_CTX + N_LAT
COND_ROWS = 8

ROW_TILE = 512
SUB_TILE = 256
FFN_TILE = 512
FF_CHUNK = 256
MOD_TILE = 2304
Q_TILE = 512
CTX_SEQS = 4
V7X_VMEM_BYTES = 64 * 1024 * 1024
VMEM_LIMIT_BYTES = V7X_VMEM_BYTES - 4 * 1024 * 1024

F32 = jnp.float32
BF16 = jnp.bfloat16


def _params():
    return pltpu.CompilerParams(vmem_limit_bytes=VMEM_LIMIT_BYTES)


def _resident(block_shape, index_map):
    return pl.BlockSpec(block_shape, index_map, pipeline_mode=pl.Buffered(1))


def _dot(a, b):
    return jnp.dot(a, b, preferred_element_type=F32)


def _dot_nt(a, b):
    return lax.dot_general(a, b, (((1,), (1,)), ((), ())), preferred_element_type=F32)


def _rms(x):
    return x * lax.rsqrt(jnp.mean(x * x, axis=-1, keepdims=True) + EPS)


def _norm_mod(x, g, shift, scale):
    return _rms(x) * g * (1.0 + scale) + shift


def _stacked_mod_row(i, tile=ROW_TILE):
    n_ctx_tiles = N_CTX // tile
    return jnp.where(i < n_ctx_tiles, 0, 1 + (i - n_ctx_tiles) // (DEC_SEQ // tile))


def _mod_chunk(mod_ref, row, k):
    return mod_ref[pl.ds(row, 1), k * D_MODEL:(k + 1) * D_MODEL]


def _mod_kernel(c_ref, w_ref, b_ref, o_ref):
    c = c_ref[...]
    o_ref[...] = _dot(c * jax.nn.sigmoid(c), w_ref[...]) + b_ref[...]


def _modulation(cond, w_mod, b_mod):
    n = N_MOD * D_MODEL
    return pl.pallas_call(
        _mod_kernel,
        grid=(DEPTH, n // MOD_TILE),
        in_specs=[
            pl.BlockSpec((COND_ROWS, D_MODEL), lambda l, j: (0, 0)),
            pl.BlockSpec((None, D_MODEL, MOD_TILE), lambda l, j: (l, 0, j)),
            pl.BlockSpec((None, 1, MOD_TILE), lambda l, j: (l, 0, j)),
        ],
        out_specs=pl.BlockSpec((None, COND_ROWS, MOD_TILE), lambda l, j: (l, 0, j)),
        out_shape=jax.ShapeDtypeStruct((DEPTH, COND_ROWS, n), F32),
        compiler_params=_params(),
        name="modulation",
    )(cond, w_mod, b_mod.reshape(DEPTH, 1, n))


def _ffn_kernel(*refs, sub, split_in, final):
    n_x = 2 if split_in else 1
    x_refs, (mod_ref, g_ref, w13_ref, w2_ref), rest = refs[:n_x], refs[n_x:n_x + 4], refs[n_x + 4:]
    i = pl.program_id(0)
    is_ctx = i < N_CTX // FFN_TILE
    row = _stacked_mod_row(i, FFN_TILE)
    x = jnp.where(is_ctx, x_refs[0][...], x_refs[1][...]) if split_in else x_refs[0][...]
    h = _norm_mod(x, g_ref[sub:sub + 1, :], _mod_chunk(mod_ref, row, 3 * sub),
                  _mod_chunk(mod_ref, row, 3 * sub + 1))

    def gate_up(lo):
        return _dot(h, w13_ref[:, lo:lo + FF_CHUNK]), _dot(h, w13_ref[:, D_FF + lo:D_FF + lo + FF_CHUNK])

    acc = jnp.zeros((FFN_TILE, D_MODEL), F32)
    nxt = gate_up(0)
    for lo in range(0, D_FF, FF_CHUNK):
        gate, up = nxt
        if lo + FF_CHUNK < D_FF:
            nxt = gate_up(lo + FF_CHUNK)
        acc = acc + _dot(gate * jax.nn.sigmoid(gate) * up, w2_ref[lo:lo + FF_CHUNK, :])
    y = x + 0.5 * _mod_chunk(mod_ref, row, 3 * sub + 2) * acc
    if not final:
        rest[0][...] = y
        return
    gf_ref, ctx_ref, lat_ref = rest
    y = _rms(y) * gf_ref[...]

    @pl.when(is_ctx)
    def _():
        ctx_ref[...] = y

    @pl.when(jnp.logical_not(is_ctx))
    def _():
        lat_ref[...] = y


def _ffn(xs, mod, g_norm, w13, w2, g_final, l, sub):
    split_in, final = isinstance(xs, tuple), g_final is not None
    n_ctx_tiles = N_CTX // FFN_TILE
    tile = pl.BlockSpec((FFN_TILE, D_MODEL), lambda i: (i, 0))
    ctx_tile = pl.BlockSpec((FFN_TILE, D_MODEL), lambda i: (jnp.minimum(i, n_ctx_tiles - 1), 0))
    lat_tile = pl.BlockSpec((FFN_TILE, D_MODEL), lambda i: (jnp.maximum(i - n_ctx_tiles, 0), 0))
    in_specs = ([ctx_tile, lat_tile] if split_in else [tile]) + [
        _resident((None, COND_ROWS, N_MOD * D_MODEL), lambda i: (l, 0, 0)),
        _resident((None, 3, D_MODEL), lambda i: (l, 0, 0)),
        _resident((None, D_MODEL, 2 * D_FF), lambda i: (l, 0, 0)),
        _resident((None, D_FF, D_MODEL), lambda i: (l, 0, 0)),
    ]
    args = (list(xs) if split_in else [xs]) + [mod, g_norm, w13, w2]
    if final:
        in_specs.append(_resident((1, D_MODEL), lambda i: (0, 0)))
        args.append(g_final.reshape(1, D_MODEL))
        out_specs = [ctx_tile, lat_tile]
        out_shape = [jax.ShapeDtypeStruct((N_CTX, D_MODEL), F32), jax.ShapeDtypeStruct((N_LAT, D_MODEL), F32)]
    else:
        out_specs, out_shape = tile, jax.ShapeDtypeStruct((N_TOK, D_MODEL), F32)
    return pl.pallas_call(
        functools.partial(_ffn_kernel, sub=sub, split_in=split_in, final=final),
        grid=(N_TOK // FFN_TILE,),
        in_specs=in_specs,
        out_specs=out_specs,
        out_shape=out_shape,
        input_output_aliases={} if split_in or final else {0: 0},
        compiler_params=_params(),
        name="ffn",
    )(*args)


def _gelu_tanh(x):
    return x * (0.5 * (1.0 + jnp.tanh(math.sqrt(2.0 / math.pi) * (x + 0.044715 * (x * x * x)))))


def _rope_tables():
    pos = np.arange(DEC_SEQ)
    lane = np.arange(V_DIM)
    half = HEAD_DIM // 2
    within = lane % half
    coord = np.where((lane % HEAD_DIM) < half, pos[:, None] // GRID_W, pos[:, None] % GRID_W)
    freqs = ROPE_THETA ** (-np.arange(0, half, 2, dtype=np.float64) / half)
    ang = coord * freqs[within % (half // 2)][None, :]
    first = (within < half // 2)[None, :]
    identity = np.zeros((ROW_TILE, V_DIM))
    cos = np.concatenate([np.cos(ang), identity + 1.0])
    sin_lo = np.concatenate([np.where(first, -np.sin(ang), 0.0), identity])
    sin_hi = np.concatenate([np.where(first, 0.0, np.sin(ang)), identity])
    return tuple(jnp.asarray(t, dtype=F32) for t in (cos, sin_lo, sin_hi))


def _proj_kernel(x_ref, mod_ref, g_ref, wz_ref, wq_ref, wk_ref, wv_ref, gain_ref, ws_ref, bias_ref,
                 cos_ref, lo_ref, hi_ref, *rest, l):
    a_ref, qkv_ref, kc_ref, vc_ref = rest[-4:]
    i = pl.program_id(0)
    row = _stacked_mod_row(i)

    def project(r0):
        h = _norm_mod(x_ref[r0:r0 + SUB_TILE, :], g_ref[1:2, :], _mod_chunk(mod_ref, row, 3),
                      _mod_chunk(mod_ref, row, 4))
        z = _gelu_tanh(_dot(h, wz_ref[...]))
        return z, _dot(h, wq_ref[...]) * (HEAD_DIM ** -0.5), _dot(h, wk_ref[...]), _dot(h, wv_ref[...])

    def gate(r0, z):
        u = z[:, :D_MODEL]
        v = _rms(z[:, D_MODEL:]) * gain_ref[l:l + 1, :]
        for c in range(0, SUB_TILE, CHUNK):
            for g in range(N_GROUPS_A):
                c0 = g * GROUP_W
                mixed = _dot(ws_ref[l, g], v[c:c + CHUNK, c0:c0 + GROUP_W]) + bias_ref[:, c0:c0 + GROUP_W]
                a_ref[r0 + c:r0 + c + CHUNK, c0:c0 + GROUP_W] = (
                    u[c:c + CHUNK, c0:c0 + GROUP_W] * mixed).astype(BF16)

    def rotate(r0, q, k):
        rows = slice(r0, r0 + SUB_TILE)
        cos, sin_lo, sin_hi = cos_ref[rows, :], lo_ref[rows, :], hi_ref[rows, :]
        for c0 in range(0, D_MODEL, V_DIM):
            for src, base in ((q, 0), (k, D_MODEL)):
                t = src[:, c0:c0 + V_DIM]
                t = (t * cos + pltpu.roll(t, V_DIM - HEAD_DIM // 4, 1) * sin_lo
                     + pltpu.roll(t, HEAD_DIM // 4, 1) * sin_hi)
                qkv_ref[rows, base + c0:base + c0 + V_DIM] = t.astype(BF16)

    starts = list(range(0, ROW_TILE, SUB_TILE))
    kv = []
    nxt = project(starts[0])
    for n, r0 in enumerate(starts):
        z, q, k, v = nxt
        if n + 1 < len(starts):
            nxt = project(starts[n + 1])
        gate(r0, z)
        rotate(r0, q, k)
        qkv_ref[r0:r0 + SUB_TILE, 2 * D_MODEL:] = v.astype(BF16)
        kv.append((k, v))

    @pl.when(i < N_CTX // ROW_TILE)
    def _():
        for n, (k, v) in enumerate(kv):
            kc_ref[n] = k.reshape(SEQ, N_HEADS, V_DIM)
            vc_ref[n] = v.reshape(SEQ, N_HEADS, V_DIM)


def _proj(x, mod, g_norm, w_in, sgu_gain, w_spatial, bias_tile, rope, caches, l):
    assert SUB_TILE == SEQ
    n_ctx_tiles, lat_tiles = N_CTX // ROW_TILE, DEC_SEQ // ROW_TILE
    tile = lambda w: pl.BlockSpec((ROW_TILE, w), lambda i: (i, 0))
    w_block = lambda c: _resident((None, D_MODEL, D_MODEL), lambda i: (l, 0, c))
    table = pl.BlockSpec((ROW_TILE, V_DIM),
                         lambda i: (jnp.where(i < n_ctx_tiles, lat_tiles, (i - n_ctx_tiles) % lat_tiles), 0))
    cache_block = pl.BlockSpec((ROW_TILE // SEQ, None, SEQ, N_HEADS, V_DIM),
                               lambda i: (jnp.minimum(i, n_ctx_tiles - 1), l, 0, 0, 0))
    cache = jax.ShapeDtypeStruct((BATCH, DEPTH, SEQ, N_HEADS, V_DIM), F32)
    in_specs = [
        tile(D_MODEL),
        _resident((None, COND_ROWS, N_MOD * D_MODEL), lambda i: (l, 0, 0)),
        _resident((None, 3, D_MODEL), lambda i: (l, 0, 0)),
        _resident((None, D_MODEL, 2 * D_MODEL), lambda i: (l, 0, 0)),
        w_block(2), w_block(3), w_block(4),
        _resident((DEPTH, D_MODEL), lambda i: (0, 0)),
        _resident((DEPTH, N_GROUPS_A, CHUNK, CHUNK), lambda i: (0, 0, 0, 0)),
        _resident((None, CHUNK, D_MODEL), lambda i: (l, 0, 0)),
        table, table, table,
    ]
    args = [x, mod, g_norm, w_in, w_in, w_in, w_in, sgu_gain, w_spatial, bias_tile, *rope]
    aliases = {}
    if caches is not None:
        in_specs += [pl.BlockSpec(memory_space=pl.ANY)] * 2
        aliases = {len(args): 2, len(args) + 1: 3}
        args += list(caches)
    return pl.pallas_call(
        functools.partial(_proj_kernel, l=l),
        grid=(N_TOK // ROW_TILE,),
        in_specs=in_specs,
        out_specs=[tile(D_MODEL), tile(3 * D_MODEL), cache_block, cache_block],
        out_shape=[jax.ShapeDtypeStruct((N_TOK, D_MODEL), BF16),
                   jax.ShapeDtypeStruct((N_TOK, 3 * D_MODEL), BF16), cache, cache],
        input_output_aliases=aliases,
        compiler_params=_params(),
        name="proj",
    )(*args)


def _lam_full(lam_ref, l):
    lp = lam_ref[l]
    s01 = jnp.sum(lp[0:1, :] * lp[1:2, :], axis=-1, keepdims=True)
    s23 = jnp.sum(lp[2:3, :] * lp[3:4, :], axis=-1, keepdims=True)
    return jnp.exp(s01) - jnp.exp(s23) + _lam_init(l)


def _lam_init(l):
    return 0.8 - 0.6 * math.exp(-0.3 * l)


def _attn_scores(q, ks):
    lane = lax.broadcasted_iota(jnp.int32, (1, V_DIM), 1)
    lo = (lane < HEAD_DIM).astype(BF16)
    qs = jnp.concatenate([q * lo, q * (1 - lo)], axis=0)
    return [_dot_nt(qs, k) for k in ks]


def _attn_combine(s, vs, lam_full, gain, l):
    n_q = s[0].shape[0] // 2
    m = functools.reduce(jnp.maximum, [jnp.max(t, axis=-1, keepdims=True) for t in s])
    pv = functools.reduce(jnp.add, [
        _dot(jnp.exp(t - m).astype(BF16), jnp.concatenate([v, jnp.ones_like(v)], axis=1))
        for t, v in zip(s, vs)])
    pv = pv[:, :V_DIM] / pv[:, V_DIM:]
    o = pv[:n_q] - lam_full * pv[n_q:]
    return _rms(o) * gain * (1.0 - _lam_init(l))


def _attn_heads(q_ref, k_refs, v_refs, lam_ref, gain_ref, o_ref, l, n_seqs=1):
    lam_full = _lam_full(lam_ref, l)
    gain = gain_ref[l:l + 1, :]
    units = [(sq, hd) for hd in range(N_HEADS) for sq in range(n_seqs)]

    def part(ref, sq, hd):
        n = ref.shape[0] // n_seqs
        return ref[sq * n:(sq + 1) * n, hd * V_DIM:(hd + 1) * V_DIM].astype(BF16)

    def scores(sq, hd):
        return _attn_scores(part(q_ref, sq, hd), [part(r, sq, hd) for r in k_refs])

    nxt = scores(*units[0])
    for n, (sq, hd) in enumerate(units):
        s = nxt
        if n + 1 < len(units):
            nxt = scores(*units[n + 1])
        o = _attn_combine(s, [part(r, sq, hd) for r in v_refs], lam_full, gain, l)
        rows = q_ref.shape[0] // n_seqs
        o_ref[sq * rows:(sq + 1) * rows, hd * V_DIM:(hd + 1) * V_DIM] = o.astype(BF16)


def _attn_ctx_kernel(q_ref, k_ref, v_ref, lam_ref, gain_ref, o_ref, *, l):
    _attn_heads(q_ref, [k_ref], [v_ref], lam_ref, gain_ref, o_ref, l, n_seqs=CTX_SEQS)


def _attn_lat_kernel(q_ref, k_ref, v_ref, past_k, past_v, lam_ref, gain_ref, _, o_ref, pk_ref, pv_ref, *, l):
    pk_ref[...] = past_k[...].reshape(PAST_LEN, D_MODEL).astype(BF16)
    pv_ref[...] = past_v[...].reshape(PAST_LEN, D_MODEL).astype(BF16)
    _attn_heads(q_ref, [pk_ref, k_ref], [pv_ref, v_ref], lam_ref, gain_ref, o_ref, l)


def _attn_ctx(qkv, lam, subln_gain, l):
    rows = CTX_SEQS * SEQ
    part = [pl.BlockSpec((rows, D_MODEL), functools.partial(lambda i, c: (i, c), c=c)) for c in range(3)]
    return pl.pallas_call(
        functools.partial(_attn_ctx_kernel, l=l),
        grid=(BATCH // CTX_SEQS,),
        in_specs=part + [_resident((DEPTH, 4, HEAD_DIM), lambda i: (0, 0, 0)),
                         _resident((DEPTH, V_DIM), lambda i: (0, 0))],
        out_specs=pl.BlockSpec((rows, D_MODEL), lambda i: (i, 0)),
        out_shape=jax.ShapeDtypeStruct((N_TOK, D_MODEL), BF16),
        compiler_params=_params(),
        name="attn_context",
    )(qkv, qkv, qkv, lam, subln_gain)


def _attn_lat(qkv, past_k, past_v, lam, subln_gain, b_all, l):
    tiles = DEC_SEQ // Q_TILE
    qspec = pl.BlockSpec((Q_TILE, D_MODEL), lambda b, j: (N_CTX // Q_TILE + b * tiles + j, 0))
    kspec = pl.BlockSpec((DEC_SEQ, D_MODEL), lambda b, j: (N_CTX // DEC_SEQ + b, 1))
    vspec = pl.BlockSpec((DEC_SEQ, D_MODEL), lambda b, j: (N_CTX // DEC_SEQ + b, 2))
    past = pl.BlockSpec((None, None, PAST_LEN, N_HEADS, V_DIM), lambda b, j: (b, l, 0, 0, 0))
    return pl.pallas_call(
        functools.partial(_attn_lat_kernel, l=l),
        grid=(DEC_BATCH, tiles),
        in_specs=[qspec, kspec, vspec, past, past,
                  _resident((DEPTH, 4, HEAD_DIM), lambda b, j: (0, 0, 0)),
                  _resident((DEPTH, V_DIM), lambda b, j: (0, 0)),
                  pl.BlockSpec(memory_space=pl.ANY)],
        out_specs=pl.BlockSpec((Q_TILE, D_MODEL), lambda b, j: (N_CTX // Q_TILE + b * tiles + j, 0)),
        out_shape=jax.ShapeDtypeStruct((N_TOK, D_MODEL), BF16),
        scratch_shapes=[pltpu.VMEM((PAST_LEN, D_MODEL), BF16)] * 2,
        input_output_aliases={7: 0},
        compiler_params=_params(),
        name="attn_latent",
    )(qkv, qkv, qkv, past_k, past_v, lam, subln_gain, b_all)


def _merge_kernel(x_ref, a_ref, b_ref, mod_ref, g_ref, wga_ref, wgb_ref, wa_ref, wb_ref, wo_ref, o_ref):
    row = _stacked_mod_row(pl.program_id(0))
    x = x_ref[...]
    h = _norm_mod(x, g_ref[1:2, :], _mod_chunk(mod_ref, row, 3), _mod_chunk(mod_ref, row, 4))
    merged = (jax.nn.sigmoid(_dot(h, wga_ref[...])) * _dot(a_ref[...].astype(F32), wa_ref[...])
              + jax.nn.sigmoid(_dot(h, wgb_ref[...])) * _dot(b_ref[...].astype(F32), wb_ref[...]))
    o_ref[...] = x + _mod_chunk(mod_ref, row, 5) * _dot(merged, wo_ref[...])


def _merge(x, a, b, mod, g_norm, w_in, w_branch_a, w_branch_b, w_out, l):
    tile = pl.BlockSpec((ROW_TILE, D_MODEL), lambda i: (i, 0))
    square = _resident((None, D_MODEL, D_MODEL), lambda i: (l, 0, 0))
    return pl.pallas_call(
        _merge_kernel,
        grid=(N_TOK // ROW_TILE,),
        in_specs=[tile, tile, tile,
                  _resident((None, COND_ROWS, N_MOD * D_MODEL), lambda i: (l, 0, 0)),
                  _resident((None, 3, D_MODEL), lambda i: (l, 0, 0)),
                  _resident((None, D_MODEL, D_MODEL), lambda i: (l, 0, 5)),
                  _resident((None, D_MODEL, D_MODEL), lambda i: (l, 0, 6)),
                  square, square, square],
        out_specs=tile,
        out_shape=jax.ShapeDtypeStruct((N_TOK, D_MODEL), F32),
        input_output_aliases={0: 0},
        compiler_params=_params(),
        name="merge",
    )(x, a, b, mod, g_norm, w_in, w_in, w_branch_a, w_branch_b, w_out)


def kernel(x_prompt, x_sample, cache_k, cache_v, c, c_ctx, w_mod, b_mod, g_norm, ffn1_w13, ffn1_w2, w_in,
           sgu_gain, w_spatial, b_spatial, lam, subln_gain, w_branch_a, w_branch_b, w_out, ffn2_w13,
           ffn2_w2, g_final):
    cond = jnp.concatenate(
        [c_ctx[None, :], c, jnp.zeros((COND_ROWS - 1 - DEC_BATCH, D_MODEL), F32)], axis=0)
    mod = _modulation(cond, w_mod, b_mod)
    bias_tile = jnp.repeat(jnp.swapaxes(b_spatial, 1, 2), GROUP_W, axis=2)
    rope = _rope_tables()

    x = (x_prompt.reshape(N_CTX, D_MODEL), x_sample.reshape(N_LAT, D_MODEL))
    caches = None
    for l in range(DEPTH):
        x = _ffn(x, mod, g_norm, ffn1_w13, ffn1_w2, None, l, 0)
        a, qkv, *caches = _proj(x, mod, g_norm, w_in, sgu_gain, w_spatial, bias_tile, rope, caches, l)
        b = _attn_ctx(qkv, lam, subln_gain, l)
        b = _attn_lat(qkv, cache_k, cache_v, lam, subln_gain, b, l)
        x = _merge(x, a, b, mod, g_norm, w_in, w_branch_a, w_branch_b, w_out, l)
        x = _ffn(x, mod, g_norm, ffn2_w13, ffn2_w2, g_final if l == DEPTH - 1 else None, l, 2)

    y_ctx, y_lat = x
    new_k, new_v = caches
    return y_ctx.reshape(BATCH, SEQ, D_MODEL), y_lat.reshape(DEC_BATCH, DEC_SEQ, D_MODEL), new_k, new_v
```

```python
import functools
import math

import numpy as np
import jax
import jax.numpy as jnp
from jax import lax
from jax.experimental import pallas as pl
from jax.experimental.pallas import tpu as pltpu

D_MODEL = 1024
BATCH = 16
SEQ = 256
DEPTH = 4
DEC_BATCH = 2
DEC_SEQ = 1024
PAST_LEN = 256
GRID_W = 64
CHUNK = 128
N_GROUPS_A = 4
GROUP_W = D_MODEL // N_GROUPS_A
N_HEADS = 8
HEAD_DIM = D_MODEL // (2 * N_HEADS)
V_DIM = 2 * HEAD_DIM
D_FF = ((8 * D_MODEL // 3 + 127) // 128) * 128
IN_W = 7 * D_MODEL
N_MOD = 9
ROPE_THETA = 10000.0
EPS = 1e-6

N_CTX = BATCH * SEQ
N_LAT = DEC_BATCH * DEC_SEQ
N_TOK = N_CTX + N_LAT
COND_ROWS = 8

ROW_TILE = 512
SUB_TILE = 256
FFN_TILE = 512
FF_CHUNK = 256
MOD_TILE = 2304
Q_TILE = 512
CTX_SEQS = 4
V7X_VMEM_BYTES = 64 * 1024 * 1024
VMEM_LIMIT_BYTES = V7X_VMEM_BYTES - 4 * 1024 * 1024

F32 = jnp.float32
BF16 = jnp.bfloat16


def _params():
    return pltpu.CompilerParams(vmem_limit_bytes=VMEM_LIMIT_BYTES)


def _resident(block_shape, index_map):
    return pl.BlockSpec(block_shape, index_map, pipeline_mode=pl.Buffered(1))


def _dot(a, b):
    return jnp.dot(a, b, preferred_element_type=F32)


def _dot_nt(a, b):
    return lax.dot_general(a, b, (((1,), (1,)), ((), ())), preferred_element_type=F32)


def _rms(x):
    return x * lax.rsqrt(jnp.mean(x * x, axis=-1, keepdims=True) + EPS)


def _norm_mod(x, g, shift, scale):
    return _rms(x) * g * (1.0 + scale) + shift


def _stacked_mod_row(i, tile=ROW_TILE):
    n_ctx_tiles = N_CTX // tile
    return jnp.where(i < n_ctx_tiles, 0, 1 + (i - n_ctx_tiles) // (DEC_SEQ // tile))


def _mod_chunk(mod_ref, row, k):
    return mod_ref[pl.ds(row, 1), k * D_MODEL:(k + 1) * D_MODEL]


def _mod_kernel(c_ref, w_ref, b_ref, o_ref):
    c = c_ref[...]
    o_ref[...] = _dot(c * jax.nn.sigmoid(c), w_ref[...]) + b_ref[...]


def _modulation(cond, w_mod, b_mod):
    n = N_MOD * D_MODEL
    return pl.pallas_call(
        _mod_kernel,
        grid=(DEPTH, n // MOD_TILE),
        in_specs=[
            pl.BlockSpec((COND_ROWS, D_MODEL), lambda l, j: (0, 0)),
            pl.BlockSpec((None, D_MODEL, MOD_TILE), lambda l, j: (l, 0, j)),
            pl.BlockSpec((None, 1, MOD_TILE), lambda l, j: (l, 0, j)),
        ],
        out_specs=pl.BlockSpec((None, COND_ROWS, MOD_TILE), lambda l, j: (l, 0, j)),
        out_shape=jax.ShapeDtypeStruct((DEPTH, COND_ROWS, n), F32),
        compiler_params=_params(),
        name="modulation",
    )(cond, w_mod, b_mod.reshape(DEPTH, 1, n))


def _ffn_weight_copies(w13_hbm, w2_hbm, w13_ref, w2_ref, sems, l, c):
    lo = pl.multiple_of(c * FF_CHUNK, FF_CHUNK)
    cols, up_cols = pl.ds(lo, FF_CHUNK), pl.ds(D_FF + lo, FF_CHUNK)
    return (pltpu.make_async_copy(w13_hbm.at[l, :, cols], w13_ref.at[:, cols], sems.at[c, 0]),
            pltpu.make_async_copy(w13_hbm.at[l, :, up_cols], w13_ref.at[:, up_cols], sems.at[c, 1]),
            pltpu.make_async_copy(w2_hbm.at[l, cols, :], w2_ref.at[cols, :], sems.at[c, 2]))


def _ffn_kernel(*refs, l, sub, split_in, final):
    n_x = 2 if split_in else 1
    x_refs, (mod_ref, g_ref, w13_hbm, w2_hbm) = refs[:n_x], refs[n_x:n_x + 4]
    *rest, w13_ref, w2_ref, acc_ref, sems = refs[n_x + 4:]
    i = pl.program_id(0)
    is_ctx = i < N_CTX // FFN_TILE
    row = _stacked_mod_row(i, FFN_TILE)
    n_chunks = D_FF // FF_CHUNK

    def normed():
        x = jnp.where(is_ctx, x_refs[0][...], x_refs[1][...]) if split_in else x_refs[0][...]
        return x, _norm_mod(x, g_ref[sub:sub + 1, :], _mod_chunk(mod_ref, row, 3 * sub),
                            _mod_chunk(mod_ref, row, 3 * sub + 1))

    def swiglu(gate, up, w2):
        return _dot(gate * jax.nn.sigmoid(gate) * up, w2)

    def finish(x, acc):
        y = x + 0.5 * _mod_chunk(mod_ref, row, 3 * sub + 2) * acc
        if not final:
            rest[0][...] = y
            return
        gf_ref, ctx_ref, lat_ref = rest
        y = _rms(y) * gf_ref[...]

        @pl.when(is_ctx)
        def _():
            ctx_ref[...] = y

        @pl.when(jnp.logical_not(is_ctx))
        def _():
            lat_ref[...] = y

    def first_step():
        def start(c, _):
            for cp in _ffn_weight_copies(w13_hbm, w2_hbm, w13_ref, w2_ref, sems, l, c):
                cp.start()
            return 0

        lax.fori_loop(0, n_chunks, start, 0)
        x, h = normed()
        acc_ref[...] = jnp.zeros_like(acc_ref)

        def chunk(c, _):
            for cp in _ffn_weight_copies(w13_hbm, w2_hbm, w13_ref, w2_ref, sems, l, c):
                cp.wait()
            lo = pl.multiple_of(c * FF_CHUNK, FF_CHUNK)
            gate = _dot(h, w13_ref[:, pl.ds(lo, FF_CHUNK)])
            up = _dot(h, w13_ref[:, pl.ds(D_FF + lo, FF_CHUNK)])
            acc_ref[...] += swiglu(gate, up, w2_ref[pl.ds(lo, FF_CHUNK), :])
            return 0

        lax.fori_loop(0, n_chunks, chunk, 0)
        finish(x, acc_ref[...])

    def later_step():
        x, h = normed()

        def gate_up(lo):
            return (_dot(h, w13_ref[:, lo:lo + FF_CHUNK]),
                    _dot(h, w13_ref[:, D_FF + lo:D_FF + lo + FF_CHUNK]))

        acc = jnp.zeros((FFN_TILE, D_MODEL), F32)
        nxt = gate_up(0)
        for lo in range(0, D_FF, FF_CHUNK):
            gate, up = nxt
            if lo + FF_CHUNK < D_FF:
                nxt = gate_up(lo + FF_CHUNK)
            acc = acc + swiglu(gate, up, w2_ref[lo:lo + FF_CHUNK, :])
        finish(x, acc)

    lax.cond(i == 0, first_step, later_step)


def _ffn(xs, mod, g_norm, w13, w2, g_final, l, sub):
    split_in, final = isinstance(xs, tuple), g_final is not None
    n_ctx_tiles = N_CTX // FFN_TILE
    tile = pl.BlockSpec((FFN_TILE, D_MODEL), lambda i: (i, 0))
    ctx_tile = pl.BlockSpec((FFN_TILE, D_MODEL), lambda i: (jnp.minimum(i, n_ctx_tiles - 1), 0))
    lat_tile = pl.BlockSpec((FFN_TILE, D_MODEL), lambda i: (jnp.maximum(i - n_ctx_tiles, 0), 0))
    in_specs = ([ctx_tile, lat_tile] if split_in else [tile]) + [
        _resident((None, COND_ROWS, N_MOD * D_MODEL), lambda i: (l, 0, 0)),
        _resident((None, 3, D_MODEL), lambda i: (l, 0, 0)),
        pl.BlockSpec(memory_space=pl.ANY),
        pl.BlockSpec(memory_space=pl.ANY),
    ]
    args = (list(xs) if split_in else [xs]) + [mod, g_norm, w13, w2]
    if final:
        in_specs.append(_resident((1, D_MODEL), lambda i: (0, 0)))
        args.append(g_final.reshape(1, D_MODEL))
        out_specs = [ctx_tile, lat_tile]
        out_shape = [jax.ShapeDtypeStruct((N_CTX, D_MODEL), F32), jax.ShapeDtypeStruct((N_LAT, D_MODEL), F32)]
    else:
        out_specs, out_shape = tile, jax.ShapeDtypeStruct((N_TOK, D_MODEL), F32)
    return pl.pallas_call(
        functools.partial(_ffn_kernel, l=l, sub=sub, split_in=split_in, final=final),
        grid=(N_TOK // FFN_TILE,),
        in_specs=in_specs,
        out_specs=out_specs,
        out_shape=out_shape,
        scratch_shapes=[pltpu.VMEM((D_MODEL, 2 * D_FF), F32), pltpu.VMEM((D_FF, D_MODEL), F32),
                        pltpu.VMEM((FFN_TILE, D_MODEL), F32),
                        pltpu.SemaphoreType.DMA((D_FF // FF_CHUNK, 3))],
        input_output_aliases={} if split_in or final else {0: 0},
        compiler_params=_params(),
        name="ffn",
    )(*args)


def _gelu_tanh(x):
    return x * (0.5 * (1.0 + jnp.tanh(math.sqrt(2.0 / math.pi) * (x + 0.044715 * (x * x * x)))))


def _rope_tables():
    pos = np.arange(DEC_SEQ)
    lane = np.arange(V_DIM)
    half = HEAD_DIM // 2
    within = lane % half
    coord = np.where((lane % HEAD_DIM) < half, pos[:, None] // GRID_W, pos[:, None] % GRID_W)
    freqs = ROPE_THETA ** (-np.arange(0, half, 2, dtype=np.float64) / half)
    ang = coord * freqs[within % (half // 2)][None, :]
    first = (within < half // 2)[None, :]
    identity = np.zeros((ROW_TILE, V_DIM))
    cos = np.concatenate([np.cos(ang), identity + 1.0])
    sin_lo = np.concatenate([np.where(first, -np.sin(ang), 0.0), identity])
    sin_hi = np.concatenate([np.where(first, 0.0, np.sin(ang)), identity])
    return tuple(jnp.asarray(t, dtype=F32) for t in (cos, sin_lo, sin_hi))


def _proj_kernel(x_ref, mod_ref, g_ref, wz_ref, wq_ref, wk_ref, wv_ref, gain_ref, ws_ref, bias_ref,
                 cos_ref, lo_ref, hi_ref, *rest, l):
    a_ref, qkv_ref, kc_ref, vc_ref = rest[-4:]
    i = pl.program_id(0)
    row = _stacked_mod_row(i)

    def project(r0):
        h = _norm_mod(x_ref[r0:r0 + SUB_TILE, :], g_ref[1:2, :], _mod_chunk(mod_ref, row, 3),
                      _mod_chunk(mod_ref, row, 4))
        z = _gelu_tanh(_dot(h, wz_ref[...]))
        return z, _dot(h, wq_ref[...]) * (HEAD_DIM ** -0.5), _dot(h, wk_ref[...]), _dot(h, wv_ref[...])

    def gate(r0, z):
        u = z[:, :D_MODEL]
        v = _rms(z[:, D_MODEL:]) * gain_ref[l:l + 1, :]
        for c in range(0, SUB_TILE, CHUNK):
            for g in range(N_GROUPS_A):
                c0 = g * GROUP_W
                mixed = _dot(ws_ref[l, g], v[c:c + CHUNK, c0:c0 + GROUP_W]) + bias_ref[:, c0:c0 + GROUP_W]
                a_ref[r0 + c:r0 + c + CHUNK, c0:c0 + GROUP_W] = (
                    u[c:c + CHUNK, c0:c0 + GROUP_W] * mixed).astype(BF16)

    def rotate(r0, q, k):
        rows = slice(r0, r0 + SUB_TILE)
        cos, sin_lo, sin_hi = cos_ref[rows, :], lo_ref[rows, :], hi_ref[rows, :]
        for c0 in range(0, D_MODEL, V_DIM):
            for src, base in ((q, 0), (k, D_MODEL)):
                t = src[:, c0:c0 + V_DIM]
                t = (t * cos + pltpu.roll(t, V_DIM - HEAD_DIM // 4, 1) * sin_lo
                     + pltpu.roll(t, HEAD_DIM // 4, 1) * sin_hi)
                qkv_ref[rows, base + c0:base + c0 + V_DIM] = t.astype(BF16)

    starts = list(range(0, ROW_TILE, SUB_TILE))
    kv = []
    nxt = project(starts[0])
    for n, r0 in enumerate(starts):
        z, q, k, v = nxt
        if n + 1 < len(starts):
            nxt = project(starts[n + 1])
        gate(r0, z)
        rotate(r0, q, k)
        qkv_ref[r0:r0 + SUB_TILE, 2 * D_MODEL:] = v.astype(BF16)
        kv.append((k, v))

    @pl.when(i < N_CTX // ROW_TILE)
    def _():
        for n, (k, v) in enumerate(kv):
            kc_ref[n] = k.reshape(SEQ, N_HEADS, V_DIM)
            vc_ref[n] = v.reshape(SEQ, N_HEADS, V_DIM)


def _proj(x, mod, g_norm, w_in, sgu_gain, w_spatial, bias_tile, rope, caches, l):
    assert SUB_TILE == SEQ
    n_ctx_tiles, lat_tiles = N_CTX // ROW_TILE, DEC_SEQ // ROW_TILE
    tile = lambda w: pl.BlockSpec((ROW_TILE, w), lambda i: (i, 0))
    w_block = lambda c: _resident((None, D_MODEL, D_MODEL), lambda i: (l, 0, c))
    table = pl.BlockSpec((ROW_TILE, V_DIM),
                         lambda i: (jnp.where(i < n_ctx_tiles, lat_tiles, (i - n_ctx_tiles) % lat_tiles), 0))
    cache_block = pl.BlockSpec((ROW_TILE // SEQ, None, SEQ, N_HEADS, V_DIM),
                               lambda i: (jnp.minimum(i, n_ctx_tiles - 1), l, 0, 0, 0))
    cache = jax.ShapeDtypeStruct((BATCH, DEPTH, SEQ, N_HEADS, V_DIM), F32)
    in_specs = [
        tile(D_MODEL),
        _resident((None, COND_ROWS, N_MOD * D_MODEL), lambda i: (l, 0, 0)),
        _resident((None, 3, D_MODEL), lambda i: (l, 0, 0)),
        _resident((None, D_MODEL, 2 * D_MODEL), lambda i: (l, 0, 0)),
        w_block(2), w_block(3), w_block(4),
        _resident((DEPTH, D_MODEL), lambda i: (0, 0)),
        _resident((DEPTH, N_GROUPS_A, CHUNK, CHUNK), lambda i: (0, 0, 0, 0)),
        _resident((None, CHUNK, D_MODEL), lambda i: (l, 0, 0)),
        table, table, table,
    ]
    args = [x, mod, g_norm, w_in, w_in, w_in, w_in, sgu_gain, w_spatial, bias_tile, *rope]
    aliases = {}
    if caches is not None:
        in_specs += [pl.BlockSpec(memory_space=pl.ANY)] * 2
        aliases = {len(args): 2, len(args) + 1: 3}
        args += list(caches)
    return pl.pallas_call(
        functools.partial(_proj_kernel, l=l),
        grid=(N_TOK // ROW_TILE,),
        in_specs=in_specs,
        out_specs=[tile(D_MODEL), tile(3 * D_MODEL), cache_block, cache_block],
        out_shape=[jax.ShapeDtypeStruct((N_TOK, D_MODEL), BF16),
                   jax.ShapeDtypeStruct((N_TOK, 3 * D_MODEL), BF16), cache, cache],
        input_output_aliases=aliases,
        compiler_params=_params(),
        name="proj",
    )(*args)


def _lam_full(lam_ref, l):
    lp = lam_ref[l]
    s01 = jnp.sum(lp[0:1, :] * lp[1:2, :], axis=-1, keepdims=True)
    s23 = jnp.sum(lp[2:3, :] * lp[3:4, :], axis=-1, keepdims=True)
    return jnp.exp(s01) - jnp.exp(s23) + _lam_init(l)


def _lam_init(l):
    return 0.8 - 0.6 * math.exp(-0.3 * l)


def _attn_scores(q, ks):
    lane = lax.broadcasted_iota(jnp.int32, (1, V_DIM), 1)
    lo = (lane < HEAD_DIM).astype(BF16)
    qs = jnp.concatenate([q * lo, q * (1 - lo)], axis=0)
    return [_dot_nt(qs, k) for k in ks]


def _attn_combine(s, vs, lam_full, gain, l):
    n_q = s[0].shape[0] // 2
    m = functools.reduce(jnp.maximum, [jnp.max(t, axis=-1, keepdims=True) for t in s])
    pv = functools.reduce(jnp.add, [
        _dot(jnp.exp(t - m).astype(BF16), jnp.concatenate([v, jnp.ones_like(v)], axis=1))
        for t, v in zip(s, vs)])
    pv = pv[:, :V_DIM] / pv[:, V_DIM:]
    o = pv[:n_q] - lam_full * pv[n_q:]
    return _rms(o) * gain * (1.0 - _lam_init(l))


def _attn_heads(q_ref, k_refs, v_refs, lam_ref, gain_ref, o_ref, l, n_seqs=1):
    lam_full = _lam_full(lam_ref, l)
    gain = gain_ref[l:l + 1, :]
    units = [(sq, hd) for hd in range(N_HEADS) for sq in range(n_seqs)]

    def part(ref, sq, hd):
        n = ref.shape[0] // n_seqs
        return ref[sq * n:(sq + 1) * n, hd * V_DIM:(hd + 1) * V_DIM].astype(BF16)

    def scores(sq, hd):
        return _attn_scores(part(q_ref, sq, hd), [part(r, sq, hd) for r in k_refs])

    nxt = scores(*units[0])
    for n, (sq, hd) in enumerate(units):
        s = nxt
        if n + 1 < len(units):
            nxt = scores(*units[n + 1])
        o = _attn_combine(s, [part(r, sq, hd) for r in v_refs], lam_full, gain, l)
        rows = q_ref.shape[0] // n_seqs
        o_ref[sq * rows:(sq + 1) * rows, hd * V_DIM:(hd + 1) * V_DIM] = o.astype(BF16)


def _attn_ctx_kernel(q_ref, k_ref, v_ref, lam_ref, gain_ref, o_ref, *, l):
    _attn_heads(q_ref, [k_ref], [v_ref], lam_ref, gain_ref, o_ref, l, n_seqs=CTX_SEQS)


def _attn_lat_kernel(q_ref, k_ref, v_ref, past_k, past_v, lam_ref, gain_ref, _, o_ref, pk_ref, pv_ref, *, l):
    pk_ref[...] = past_k[...].reshape(PAST_LEN, D_MODEL).astype(BF16)
    pv_ref[...] = past_v[...].reshape(PAST_LEN, D_MODEL).astype(BF16)
    _attn_heads(q_ref, [pk_ref, k_ref], [pv_ref, v_ref], lam_ref, gain_ref, o_ref, l)


def _attn_ctx(qkv, lam, subln_gain, l):
    rows = CTX_SEQS * SEQ
    part = [pl.BlockSpec((rows, D_MODEL), functools.partial(lambda i, c: (i, c), c=c)) for c in range(3)]
    return pl.pallas_call(
        functools.partial(_attn_ctx_kernel, l=l),
        grid=(BATCH // CTX_SEQS,),
        in_specs=part + [_resident((DEPTH, 4, HEAD_DIM), lambda i: (0, 0, 0)),
                         _resident((DEPTH, V_DIM), lambda i: (0, 0))],
        out_specs=pl.BlockSpec((rows, D_MODEL), lambda i: (i, 0)),
        out_shape=jax.ShapeDtypeStruct((N_TOK, D_MODEL), BF16),
        compiler_params=_params(),
        name="attn_context",
    )(qkv, qkv, qkv, lam, subln_gain)


def _attn_lat(qkv, past_k, past_v, lam, subln_gain, b_all, l):
    tiles = DEC_SEQ // Q_TILE
    qspec = pl.BlockSpec((Q_TILE, D_MODEL), lambda b, j: (N_CTX // Q_TILE + b * tiles + j, 0))
    kspec = pl.BlockSpec((DEC_SEQ, D_MODEL), lambda b, j: (N_CTX // DEC_SEQ + b, 1))
    vspec = pl.BlockSpec((DEC_SEQ, D_MODEL), lambda b, j: (N_CTX // DEC_SEQ + b, 2))
    past = pl.BlockSpec((None, None, PAST_LEN, N_HEADS, V_DIM), lambda b, j: (b, l, 0, 0, 0))
    return pl.pallas_call(
        functools.partial(_attn_lat_kernel, l=l),
        grid=(DEC_BATCH, tiles),
        in_specs=[qspec, kspec, vspec, past, past,
                  _resident((DEPTH, 4, HEAD_DIM), lambda b, j: (0, 0, 0)),
                  _resident((DEPTH, V_DIM), lambda b, j: (0, 0)),
                  pl.BlockSpec(memory_space=pl.ANY)],
        out_specs=pl.BlockSpec((Q_TILE, D_MODEL), lambda b, j: (N_CTX // Q_TILE + b * tiles + j, 0)),
        out_shape=jax.ShapeDtypeStruct((N_TOK, D_MODEL), BF16),
        scratch_shapes=[pltpu.VMEM((PAST_LEN, D_MODEL), BF16)] * 2,
        input_output_aliases={7: 0},
        compiler_params=_params(),
        name="attn_latent",
    )(qkv, qkv, qkv, past_k, past_v, lam, subln_gain, b_all)


def _merge_kernel(x_ref, a_ref, b_ref, mod_ref, g_ref, wga_ref, wgb_ref, wa_ref, wb_ref, wo_ref, o_ref):
    row = _stacked_mod_row(pl.program_id(0))
    x = x_ref[...]
    h = _norm_mod(x, g_ref[1:2, :], _mod_chunk(mod_ref, row, 3), _mod_chunk(mod_ref, row, 4))
    merged = (jax.nn.sigmoid(_dot(h, wga_ref[...])) * _dot(a_ref[...].astype(F32), wa_ref[...])
              + jax.nn.sigmoid(_dot(h, wgb_ref[...])) * _dot(b_ref[...].astype(F32), wb_ref[...]))
    o_ref[...] = x + _mod_chunk(mod_ref, row, 5) * _dot(merged, wo_ref[...])


def _merge(x, a, b, mod, g_norm, w_in, w_branch_a, w_branch_b, w_out, l):
    tile = pl.BlockSpec((ROW_TILE, D_MODEL), lambda i: (i, 0))
    square = _resident((None, D_MODEL, D_MODEL), lambda i: (l, 0, 0))
    return pl.pallas_call(
        _merge_kernel,
        grid=(N_TOK // ROW_TILE,),
        in_specs=[tile, tile, tile,
                  _resident((None, COND_ROWS, N_MOD * D_MODEL), lambda i: (l, 0, 0)),
                  _resident((None, 3, D_MODEL), lambda i: (l, 0, 0)),
                  _resident((None, D_MODEL, D_MODEL), lambda i: (l, 0, 5)),
                  _resident((None, D_MODEL, D_MODEL), lambda i: (l, 0, 6)),
                  square, square, square],
        out_specs=tile,
        out_shape=jax.ShapeDtypeStruct((N_TOK, D_MODEL), F32),
        input_output_aliases={0: 0},
        compiler_params=_params(),
        name="merge",
    )(x, a, b, mod, g_norm, w_in, w_in, w_branch_a, w_branch_b, w_out)


def kernel(x_prompt, x_sample, cache_k, cache_v, c, c_ctx, w_mod, b_mod, g_norm, ffn1_w13, ffn1_w2, w_in,
           sgu_gain, w_spatial, b_spatial, lam, subln_gain, w_branch_a, w_branch_b, w_out, ffn2_w13,
           ffn2_w2, g_final):
    cond = jnp.concatenate(
        [c_ctx[None, :], c, jnp.zeros((COND_ROWS - 1 - DEC_BATCH, D_MODEL), F32)], axis=0)
    mod = _modulation(cond, w_mod, b_mod)
    bias_tile = jnp.repeat(jnp.swapaxes(b_spatial, 1, 2), GROUP_W, axis=2)
    rope = _rope_tables()

    x = (x_prompt.reshape(N_CTX, D_MODEL), x_sample.reshape(N_LAT, D_MODEL))
    caches = None
    for l in range(DEPTH):
        x = _ffn(x, mod, g_norm, ffn1_w13, ffn1_w2, None, l, 0)
        a, qkv, *caches = _proj(x, mod, g_norm, w_in, sgu_gain, w_spatial, bias_tile, rope, caches, l)
        b = _attn_ctx(qkv, lam, subln_gain, l)
        b = _attn_lat(qkv, cache_k, cache_v, lam, subln_gain, b, l)
        x = _merge(x, a, b, mod, g_norm, w_in, w_branch_a, w_branch_b, w_out, l)
        x = _ffn(x, mod, g_norm, ffn2_w13, ffn2_w2, g_final if l == DEPTH - 1 else None, l, 2)

    y_ctx, y_lat = x
    new_k, new_v = caches
    return y_ctx.reshape(BATCH, SEQ, D_MODEL), y_lat.reshape(DEC_BATCH, DEC_SEQ, D_MODEL), new_k, new_v
```

```python
import functools
import math

import numpy as np
import jax
import jax.numpy as jnp
from jax import lax
from jax.experimental import pallas as pl
from jax.experimental.pallas import tpu as pltpu

D_MODEL = 1024
BATCH = 16
SEQ = 256
DEPTH = 4
DEC_BATCH = 2
DEC_SEQ = 1024
PAST_LEN = 256
GRID_W = 64
CHUNK = 128
N_GROUPS_A = 4
GROUP_W = D_MODEL // N_GROUPS_A
N_HEADS = 8
HEAD_DIM = D_MODEL // (2 * N_HEADS)
V_DIM = 2 * HEAD_DIM
D_FF = ((8 * D_MODEL // 3 + 127) // 128) * 128
IN_W = 7 * D_MODEL
N_MOD = 9
ROPE_THETA = 10000.0
EPS = 1e-6

N_CTX = BATCH * SEQ
N_LAT = DEC_BATCH * DEC_SEQ
N_TOK = N_CTX + N_LAT
COND_ROWS = 8

ROW_TILE = 512
SUB_TILE = 256
FFN_TILE = 512
FF_CHUNK = 256
MOD_TILE = 2304
Q_TILE = 512
CTX_SEQS = 4
V7X_VMEM_BYTES = 64 * 1024 * 1024
VMEM_LIMIT_BYTES = V7X_VMEM_BYTES - 4 * 1024 * 1024

F32 = jnp.float32
BF16 = jnp.bfloat16


def _params():
    return pltpu.CompilerParams(vmem_limit_bytes=VMEM_LIMIT_BYTES)


def _resident(block_shape, index_map):
    return pl.BlockSpec(block_shape, index_map, pipeline_mode=pl.Buffered(1))


def _dot(a, b):
    return jnp.dot(a, b, preferred_element_type=F32)


def _dot_nt(a, b):
    return lax.dot_general(a, b, (((1,), (1,)), ((), ())), preferred_element_type=F32)


def _rms(x):
    return x * lax.rsqrt(jnp.mean(x * x, axis=-1, keepdims=True) + EPS)


def _norm_mod(x, g, shift, scale):
    return _rms(x) * g * (1.0 + scale) + shift


def _stacked_mod_row(i, tile=ROW_TILE):
    n_ctx_tiles = N_CTX // tile
    return jnp.where(i < n_ctx_tiles, 0, 1 + (i - n_ctx_tiles) // (DEC_SEQ // tile))


def _mod_chunk(mod_ref, row, k):
    return mod_ref[pl.ds(row, 1), k * D_MODEL:(k + 1) * D_MODEL]


def _mod_kernel(c_ref, w_ref, b_ref, o_ref):
    c = c_ref[...]
    o_ref[...] = _dot(c * jax.nn.sigmoid(c), w_ref[...]) + b_ref[...]


def _modulation(cond, w_mod, b_mod):
    n = N_MOD * D_MODEL
    return pl.pallas_call(
        _mod_kernel,
        grid=(DEPTH, n // MOD_TILE),
        in_specs=[
            pl.BlockSpec((COND_ROWS, D_MODEL), lambda l, j: (0, 0)),
            pl.BlockSpec((None, D_MODEL, MOD_TILE), lambda l, j: (l, 0, j)),
            pl.BlockSpec((None, 1, MOD_TILE), lambda l, j: (l, 0, j)),
        ],
        out_specs=pl.BlockSpec((None, COND_ROWS, MOD_TILE), lambda l, j: (l, 0, j)),
        out_shape=jax.ShapeDtypeStruct((DEPTH, COND_ROWS, n), F32),
        compiler_params=_params(),
        name="modulation",
    )(cond, w_mod, b_mod.reshape(DEPTH, 1, n))


def _ffn_weight_copies(w13_hbm, w2_hbm, w13_ref, w2_ref, sems, l, c):
    lo = pl.multiple_of(c * FF_CHUNK, FF_CHUNK)
    cols, up_cols = pl.ds(lo, FF_CHUNK), pl.ds(D_FF + lo, FF_CHUNK)
    return (pltpu.make_async_copy(w13_hbm.at[l, :, cols], w13_ref.at[:, cols], sems.at[c, 0]),
            pltpu.make_async_copy(w13_hbm.at[l, :, up_cols], w13_ref.at[:, up_cols], sems.at[c, 1]),
            pltpu.make_async_copy(w2_hbm.at[l, cols, :], w2_ref.at[cols, :], sems.at[c, 2]))


def _ffn_kernel(*refs, l, sub, split_in, final):
    n_x = 2 if split_in else 1
    x_refs, (mod_ref, g_ref, w13_hbm, w2_hbm) = refs[:n_x], refs[n_x:n_x + 4]
    *rest, w13_ref, w2_ref, acc_ref, sems = refs[n_x + 4:]
    i = pl.program_id(0)
    is_ctx = i < N_CTX // FFN_TILE
    row = _stacked_mod_row(i, FFN_TILE)
    n_chunks = D_FF // FF_CHUNK

    def normed():
        x = jnp.where(is_ctx, x_refs[0][...], x_refs[1][...]) if split_in else x_refs[0][...]
        return x, _norm_mod(x, g_ref[sub:sub + 1, :], _mod_chunk(mod_ref, row, 3 * sub),
                            _mod_chunk(mod_ref, row, 3 * sub + 1))

    def swiglu(gate, up, w2):
        return _dot(gate * jax.nn.sigmoid(gate) * up, w2)

    def finish(x, acc):
        y = x + 0.5 * _mod_chunk(mod_ref, row, 3 * sub + 2) * acc
        if not final:
            rest[0][...] = y
            return
        gf_ref, ctx_ref, lat_ref = rest
        y = _rms(y) * gf_ref[...]

        @pl.when(is_ctx)
        def _():
            ctx_ref[...] = y

        @pl.when(jnp.logical_not(is_ctx))
        def _():
            lat_ref[...] = y

    def first_step():
        def start(c, _):
            for cp in _ffn_weight_copies(w13_hbm, w2_hbm, w13_ref, w2_ref, sems, l, c):
                cp.start()
            return 0

        lax.fori_loop(0, n_chunks, start, 0)
        x, h = normed()
        acc_ref[...] = jnp.zeros_like(acc_ref)

        def chunk(c, _):
            for cp in _ffn_weight_copies(w13_hbm, w2_hbm, w13_ref, w2_ref, sems, l, c):
                cp.wait()
            lo = pl.multiple_of(c * FF_CHUNK, FF_CHUNK)
            gate = _dot(h, w13_ref[:, pl.ds(lo, FF_CHUNK)])
            up = _dot(h, w13_ref[:, pl.ds(D_FF + lo, FF_CHUNK)])
            acc_ref[...] += swiglu(gate, up, w2_ref[pl.ds(lo, FF_CHUNK), :])
            return 0

        lax.fori_loop(0, n_chunks, chunk, 0)
        finish(x, acc_ref[...])

    def later_step():
        x, h = normed()

        def gate_up(lo):
            return (_dot(h, w13_ref[:, lo:lo + FF_CHUNK]),
                    _dot(h, w13_ref[:, D_FF + lo:D_FF + lo + FF_CHUNK]))

        acc = jnp.zeros((FFN_TILE, D_MODEL), F32)
        nxt = gate_up(0)
        for lo in range(0, D_FF, FF_CHUNK):
            gate, up = nxt
            if lo + FF_CHUNK < D_FF:
                nxt = gate_up(lo + FF_CHUNK)
            acc = acc + swiglu(gate, up, w2_ref[lo:lo + FF_CHUNK, :])
        finish(x, acc)

    lax.cond(i == 0, first_step, later_step)


def _ffn(xs, mod, g_norm, w13, w2, g_final, l, sub):
    split_in, final = isinstance(xs, tuple), g_final is not None
    n_ctx_tiles = N_CTX // FFN_TILE
    tile = pl.BlockSpec((FFN_TILE, D_MODEL), lambda i: (i, 0))
    ctx_tile = pl.BlockSpec((FFN_TILE, D_MODEL), lambda i: (jnp.minimum(i, n_ctx_tiles - 1), 0))
    lat_tile = pl.BlockSpec((FFN_TILE, D_MODEL), lambda i: (jnp.maximum(i - n_ctx_tiles, 0), 0))
    in_specs = ([ctx_tile, lat_tile] if split_in else [tile]) + [
        _resident((None, COND_ROWS, N_MOD * D_MODEL), lambda i: (l, 0, 0)),
        _resident((None, 3, D_MODEL), lambda i: (l, 0, 0)),
        pl.BlockSpec(memory_space=pl.ANY),
        pl.BlockSpec(memory_space=pl.ANY),
    ]
    args = (list(xs) if split_in else [xs]) + [mod, g_norm, w13, w2]
    if final:
        in_specs.append(_resident((1, D_MODEL), lambda i: (0, 0)))
        args.append(g_final.reshape(1, D_MODEL))
        out_specs = [ctx_tile, lat_tile]
        out_shape = [jax.ShapeDtypeStruct((N_CTX, D_MODEL), F32), jax.ShapeDtypeStruct((N_LAT, D_MODEL), F32)]
    else:
        out_specs, out_shape = tile, jax.ShapeDtypeStruct((N_TOK, D_MODEL), F32)
    return pl.pallas_call(
        functools.partial(_ffn_kernel, l=l, sub=sub, split_in=split_in, final=final),
        grid=(N_TOK // FFN_TILE,),
        in_specs=in_specs,
        out_specs=out_specs,
        out_shape=out_shape,
        scratch_shapes=[pltpu.VMEM((D_MODEL, 2 * D_FF), F32), pltpu.VMEM((D_FF, D_MODEL), F32),
                        pltpu.VMEM((FFN_TILE, D_MODEL), F32),
                        pltpu.SemaphoreType.DMA((D_FF // FF_CHUNK, 3))],
        input_output_aliases={} if split_in or final else {0: 0},
        compiler_params=_params(),
        name="ffn",
    )(*args)


def _gelu_tanh(x):
    return x * (0.5 * (1.0 + jnp.tanh(math.sqrt(2.0 / math.pi) * (x + 0.044715 * (x * x * x)))))


def _rope_tables():
    pos = np.arange(DEC_SEQ)
    lane = np.arange(V_DIM)
    half = HEAD_DIM // 2
    within = lane % half
    coord = np.where((lane % HEAD_DIM) < half, pos[:, None] // GRID_W, pos[:, None] % GRID_W)
    freqs = ROPE_THETA ** (-np.arange(0, half, 2, dtype=np.float64) / half)
    ang = coord * freqs[within % (half // 2)][None, :]
    first = (within < half // 2)[None, :]
    identity = np.zeros((ROW_TILE, V_DIM))
    cos = np.concatenate([np.cos(ang), identity + 1.0])
    sin_lo = np.concatenate([np.where(first, -np.sin(ang), 0.0), identity])
    sin_hi = np.concatenate([np.where(first, 0.0, np.sin(ang)), identity])
    return tuple(jnp.asarray(t, dtype=F32) for t in (cos, sin_lo, sin_hi))


def _proj_kernel(x_ref, mod_ref, g_ref, wz_ref, wq_ref, wk_ref, wv_ref, gain_ref, ws_ref, bias_ref,
                 cos_ref, lo_ref, hi_ref, *rest, l):
    a_ref, qkv_ref, kc_ref, vc_ref = rest[-4:]
    i = pl.program_id(0)
    row = _stacked_mod_row(i)

    def project(r0):
        h = _norm_mod(x_ref[r0:r0 + SUB_TILE, :], g_ref[1:2, :], _mod_chunk(mod_ref, row, 3),
                      _mod_chunk(mod_ref, row, 4))
        z = _gelu_tanh(_dot(h, wz_ref[...]))
        return z, _dot(h, wq_ref[...]) * (HEAD_DIM ** -0.5), _dot(h, wk_ref[...]), _dot(h, wv_ref[...])

    def gate(r0, z):
        u = z[:, :D_MODEL]
        v = _rms(z[:, D_MODEL:]) * gain_ref[l:l + 1, :]
        for c in range(0, SUB_TILE, CHUNK):
            for g in range(N_GROUPS_A):
                c0 = g * GROUP_W
                mixed = _dot(ws_ref[l, g], v[c:c + CHUNK, c0:c0 + GROUP_W]) + bias_ref[:, c0:c0 + GROUP_W]
                a_ref[r0 + c:r0 + c + CHUNK, c0:c0 + GROUP_W] = (
                    u[c:c + CHUNK, c0:c0 + GROUP_W] * mixed).astype(BF16)

    def rotate(r0, q, k):
        rows = slice(r0, r0 + SUB_TILE)
        cos, sin_lo, sin_hi = cos_ref[rows, :], lo_ref[rows, :], hi_ref[rows, :]
        for c0 in range(0, D_MODEL, V_DIM):
            for src, base in ((q, 0), (k, D_MODEL)):
                t = src[:, c0:c0 + V_DIM]
                t = (t * cos + pltpu.roll(t, V_DIM - HEAD_DIM // 4, 1) * sin_lo
                     + pltpu.roll(t, HEAD_DIM // 4, 1) * sin_hi)
                qkv_ref[rows, base + c0:base + c0 + V_DIM] = t.astype(BF16)

    starts = list(range(0, ROW_TILE, SUB_TILE))
    kv = []
    nxt = project(starts[0])
    for n, r0 in enumerate(starts):
        z, q, k, v = nxt
        if n + 1 < len(starts):
            nxt = project(starts[n + 1])
        gate(r0, z)
        rotate(r0, q, k)
        qkv_ref[r0:r0 + SUB_TILE, 2 * D_MODEL:] = v.astype(BF16)
        kv.append((k, v))

    @pl.when(i < N_CTX // ROW_TILE)
    def _():
        for n, (k, v) in enumerate(kv):
            kc_ref[n] = k.reshape(SEQ, N_HEADS, V_DIM)
            vc_ref[n] = v.reshape(SEQ, N_HEADS, V_DIM)


def _proj(x, mod, g_norm, w_in, sgu_gain, w_spatial, bias_tile, rope, caches, l):
    assert SUB_TILE == SEQ
    n_ctx_tiles, lat_tiles = N_CTX // ROW_TILE, DEC_SEQ // ROW_TILE
    tile = lambda w: pl.BlockSpec((ROW_TILE, w), lambda i: (i, 0))
    w_block = lambda c: _resident((None, D_MODEL, D_MODEL), lambda i: (l, 0, c))
    table = pl.BlockSpec((ROW_TILE, V_DIM),
                         lambda i: (jnp.where(i < n_ctx_tiles, lat_tiles, (i - n_ctx_tiles) % lat_tiles), 0))
    cache_block = pl.BlockSpec((ROW_TILE // SEQ, None, SEQ, N_HEADS, V_DIM),
                               lambda i: (jnp.minimum(i, n_ctx_tiles - 1), l, 0, 0, 0))
    cache = jax.ShapeDtypeStruct((BATCH, DEPTH, SEQ, N_HEADS, V_DIM), F32)
    in_specs = [
        tile(D_MODEL),
        _resident((None, COND_ROWS, N_MOD * D_MODEL), lambda i: (l, 0, 0)),
        _resident((None, 3, D_MODEL), lambda i: (l, 0, 0)),
        _resident((None, D_MODEL, 2 * D_MODEL), lambda i: (l, 0, 0)),
        w_block(2), w_block(3), w_block(4),
        _resident((DEPTH, D_MODEL), lambda i: (0, 0)),
        _resident((DEPTH, N_GROUPS_A, CHUNK, CHUNK), lambda i: (0, 0, 0, 0)),
        _resident((None, CHUNK, D_MODEL), lambda i: (l, 0, 0)),
        table, table, table,
    ]
    args = [x, mod, g_norm, w_in, w_in, w_in, w_in, sgu_gain, w_spatial, bias_tile, *rope]
    aliases = {}
    if caches is not None:
        in_specs += [pl.BlockSpec(memory_space=pl.ANY)] * 2
        aliases = {len(args): 2, len(args) + 1: 3}
        args += list(caches)
    return pl.pallas_call(
        functools.partial(_proj_kernel, l=l),
        grid=(N_TOK // ROW_TILE,),
        in_specs=in_specs,
        out_specs=[tile(D_MODEL), tile(3 * D_MODEL), cache_block, cache_block],
        out_shape=[jax.ShapeDtypeStruct((N_TOK, D_MODEL), BF16),
                   jax.ShapeDtypeStruct((N_TOK, 3 * D_MODEL), BF16), cache, cache],
        input_output_aliases=aliases,
        compiler_params=_params(),
        name="proj",
    )(*args)


def _lam_full(lam_ref, l):
    lp = lam_ref[l]
    s01 = jnp.sum(lp[0:1, :] * lp[1:2, :], axis=-1, keepdims=True)
    s23 = jnp.sum(lp[2:3, :] * lp[3:4, :], axis=-1, keepdims=True)
    return jnp.exp(s01) - jnp.exp(s23) + _lam_init(l)


def _lam_init(l):
    return 0.8 - 0.6 * math.exp(-0.3 * l)


def _attn_scores(q, ks):
    lane = lax.broadcasted_iota(jnp.int32, (1, V_DIM), 1)
    lo = (lane < HEAD_DIM).astype(BF16)
    qs = jnp.concatenate([q * lo, q * (1 - lo)], axis=0)
    return [_dot_nt(qs, k) for k in ks]


def _attn_combine(s, vs, lam_full, gain, l):
    n_q = s[0].shape[0] // 2
    m = functools.reduce(jnp.maximum, [jnp.max(t, axis=-1, keepdims=True) for t in s])
    pv = functools.reduce(jnp.add, [
        _dot(jnp.exp(t - m).astype(BF16), jnp.concatenate([v, jnp.ones_like(v)], axis=1))
        for t, v in zip(s, vs)])
    pv = pv[:, :V_DIM] / pv[:, V_DIM:]
    o = pv[:n_q] - lam_full * pv[n_q:]
    return _rms(o) * gain * (1.0 - _lam_init(l))


def _attn_heads(q_ref, k_refs, v_refs, lam_ref, gain_ref, o_ref, l, n_seqs=1):
    lam_full = _lam_full(lam_ref, l)
    gain = gain_ref[l:l + 1, :]
    units = [(sq, hd) for hd in range(N_HEADS) for sq in range(n_seqs)]

    def part(ref, sq, hd):
        n = ref.shape[0] // n_seqs
        return ref[sq * n:(sq + 1) * n, hd * V_DIM:(hd + 1) * V_DIM].astype(BF16)

    def scores(sq, hd):
        return _attn_scores(part(q_ref, sq, hd), [part(r, sq, hd) for r in k_refs])

    nxt = scores(*units[0])
    for n, (sq, hd) in enumerate(units):
        s = nxt
        if n + 1 < len(units):
            nxt = scores(*units[n + 1])
        o = _attn_combine(s, [part(r, sq, hd) for r in v_refs], lam_full, gain, l)
        rows = q_ref.shape[0] // n_seqs
        o_ref[sq * rows:(sq + 1) * rows, hd * V_DIM:(hd + 1) * V_DIM] = o.astype(BF16)


def _attn_ctx_kernel(q_ref, k_ref, v_ref, lam_ref, gain_ref, o_ref, *, l):
    _attn_heads(q_ref, [k_ref], [v_ref], lam_ref, gain_ref, o_ref, l, n_seqs=CTX_SEQS)


def _attn_lat_kernel(q_ref, k_ref, v_ref, past_k, past_v, lam_ref, gain_ref, _, o_ref, pk_ref, pv_ref, *, l):
    pk_ref[...] = past_k[...].reshape(PAST_LEN, D_MODEL).astype(BF16)
    pv_ref[...] = past_v[...].reshape(PAST_LEN, D_MODEL).astype(BF16)
    _attn_heads(q_ref, [pk_ref, k_ref], [pv_ref, v_ref], lam_ref, gain_ref, o_ref, l)


def _attn_ctx(qkv, lam, subln_gain, l):
    rows = CTX_SEQS * SEQ
    part = [pl.BlockSpec((rows, D_MODEL), functools.partial(lambda i, c: (i, c), c=c)) for c in range(3)]
    return pl.pallas_call(
        functools.partial(_attn_ctx_kernel, l=l),
        grid=(BATCH // CTX_SEQS,),
        in_specs=part + [_resident((DEPTH, 4, HEAD_DIM), lambda i: (0, 0, 0)),
                         _resident((DEPTH, V_DIM), lambda i: (0, 0))],
        out_specs=pl.BlockSpec((rows, D_MODEL), lambda i: (i, 0)),
        out_shape=jax.ShapeDtypeStruct((N_TOK, D_MODEL), BF16),
        compiler_params=_params(),
        name="attn_context",
    )(qkv, qkv, qkv, lam, subln_gain)


def _attn_lat(qkv, past_k, past_v, lam, subln_gain, b_all, l):
    tiles = DEC_SEQ // Q_TILE
    qspec = pl.BlockSpec((Q_TILE, D_MODEL), lambda b, j: (N_CTX // Q_TILE + b * tiles + j, 0))
    kspec = pl.BlockSpec((DEC_SEQ, D_MODEL), lambda b, j: (N_CTX // DEC_SEQ + b, 1))
    vspec = pl.BlockSpec((DEC_SEQ, D_MODEL), lambda b, j: (N_CTX // DEC_SEQ + b, 2))
    past = pl.BlockSpec((None, None, PAST_LEN, N_HEADS, V_DIM), lambda b, j: (b, l, 0, 0, 0))
    return pl.pallas_call(
        functools.partial(_attn_lat_kernel, l=l),
        grid=(DEC_BATCH, tiles),
        in_specs=[qspec, kspec, vspec, past, past,
                  _resident((DEPTH, 4, HEAD_DIM), lambda b, j: (0, 0, 0)),
                  _resident((DEPTH, V_DIM), lambda b, j: (0, 0)),
                  pl.BlockSpec(memory_space=pl.ANY)],
        out_specs=pl.BlockSpec((Q_TILE, D_MODEL), lambda b, j: (N_CTX // Q_TILE + b * tiles + j, 0)),
        out_shape=jax.ShapeDtypeStruct((N_TOK, D_MODEL), BF16),
        scratch_shapes=[pltpu.VMEM((PAST_LEN, D_MODEL), BF16)] * 2,
        input_output_aliases={7: 0},
        compiler_params=_params(),
        name="attn_latent",
    )(qkv, qkv, qkv, past_k, past_v, lam, subln_gain, b_all)


def _merge_weight_copies(w_in_hbm, wa_hbm, wb_hbm, wo_hbm, w_ref, sems, l):
    sources = [w_in_hbm.at[l, :, pl.ds(5 * D_MODEL, D_MODEL)], w_in_hbm.at[l, :, pl.ds(6 * D_MODEL, D_MODEL)],
               wa_hbm.at[l], wb_hbm.at[l], wo_hbm.at[l]]
    return [pltpu.make_async_copy(src, w_ref.at[k], sems.at[k]) for k, src in enumerate(sources)]


def _merge_kernel(x_ref, a_ref, b_ref, mod_ref, g_ref, w_in_hbm, wa_hbm, wb_hbm, wo_hbm, o_ref, w_ref, sems, *, l):
    i = pl.program_id(0)
    row = _stacked_mod_row(i)

    def body(copies):
        def weight(k):
            if copies is not None:
                copies[k].wait()
            return w_ref[k]

        x = x_ref[...]
        h = _norm_mod(x, g_ref[1:2, :], _mod_chunk(mod_ref, row, 3), _mod_chunk(mod_ref, row, 4))
        gate_a = jax.nn.sigmoid(_dot(h, weight(0)))
        gate_b = jax.nn.sigmoid(_dot(h, weight(1)))
        merged = (gate_a * _dot(a_ref[...].astype(F32), weight(2))
                  + gate_b * _dot(b_ref[...].astype(F32), weight(3)))
        o_ref[...] = x + _mod_chunk(mod_ref, row, 5) * _dot(merged, weight(4))

    def first_step():
        copies = _merge_weight_copies(w_in_hbm, wa_hbm, wb_hbm, wo_hbm, w_ref, sems, l)
        for cp in copies:
            cp.start()
        body(copies)

    lax.cond(i == 0, first_step, lambda: body(None))


def _merge(x, a, b, mod, g_norm, w_in, w_branch_a, w_branch_b, w_out, l):
    tile = pl.BlockSpec((ROW_TILE, D_MODEL), lambda i: (i, 0))
    hbm = pl.BlockSpec(memory_space=pl.ANY)
    return pl.pallas_call(
        functools.partial(_merge_kernel, l=l),
        grid=(N_TOK // ROW_TILE,),
        in_specs=[tile, tile, tile,
                  _resident((None, COND_ROWS, N_MOD * D_MODEL), lambda i: (l, 0, 0)),
                  _resident((None, 3, D_MODEL), lambda i: (l, 0, 0)),
                  hbm, hbm, hbm, hbm],
        out_specs=tile,
        out_shape=jax.ShapeDtypeStruct((N_TOK, D_MODEL), F32),
        scratch_shapes=[pltpu.VMEM((5, D_MODEL, D_MODEL), F32), pltpu.SemaphoreType.DMA((5,))],
        input_output_aliases={0: 0},
        compiler_params=_params(),
        name="merge",
    )(x, a, b, mod, g_norm, w_in, w_branch_a, w_branch_b, w_out)


def kernel(x_prompt, x_sample, cache_k, cache_v, c, c_ctx, w_mod, b_mod, g_norm, ffn1_w13, ffn1_w2, w_in,
           sgu_gain, w_spatial, b_spatial, lam, subln_gain, w_branch_a, w_branch_b, w_out, ffn2_w13,
           ffn2_w2, g_final):
    cond = jnp.concatenate(
        [c_ctx[None, :], c, jnp.zeros((COND_ROWS - 1 - DEC_BATCH, D_MODEL), F32)], axis=0)
    mod = _modulation(cond, w_mod, b_mod)
    bias_tile = jnp.repeat(jnp.swapaxes(b_spatial, 1, 2), GROUP_W, axis=2)
    rope = _rope_tables()

    x = (x_prompt.reshape(N_CTX, D_MODEL), x_sample.reshape(N_LAT, D_MODEL))
    caches = None
    for l in range(DEPTH):
        x = _ffn(x, mod, g_norm, ffn1_w13, ffn1_w2, None, l, 0)
        a, qkv, *caches = _proj(x, mod, g_norm, w_in, sgu_gain, w_spatial, bias_tile, rope, caches, l)
        b = _attn_ctx(qkv, lam, subln_gain, l)
        b = _attn_lat(qkv, cache_k, cache_v, lam, subln_gain, b, l)
        x = _merge(x, a, b, mod, g_norm, w_in, w_branch_a, w_branch_b, w_out, l)
        x = _ffn(x, mod, g_norm, ffn2_w13, ffn2_w2, g_final if l == DEPTH - 1 else None, l, 2)

    y_ctx, y_lat = x
    new_k, new_v = caches
    return y_ctx.reshape(BATCH, SEQ, D_MODEL), y_lat.reshape(DEC_BATCH, DEC_SEQ, D_MODEL), new_k, new_v
```

```python
import functools
import math

import numpy as np
import jax
import jax.numpy as jnp
from jax import lax
from jax.experimental import pallas as pl
from jax.experimental.pallas import tpu as pltpu

D_MODEL = 1024
BATCH = 16
SEQ = 256
DEPTH = 4
DEC_BATCH = 2
DEC_SEQ = 1024
PAST_LEN = 256
GRID_W = 64
CHUNK = 128
N_GROUPS_A = 4
GROUP_W = D_MODEL // N_GROUPS_A
N_HEADS = 8
HEAD_DIM = D_MODEL // (2 * N_HEADS)
V_DIM = 2 * HEAD_DIM
D_FF = ((8 * D_MODEL // 3 + 127) // 128) * 128
IN_W = 7 * D_MODEL
N_MOD = 9
ROPE_THETA = 10000.0
EPS = 1e-6

N_CTX = BATCH * SEQ
N_LAT = DEC_BATCH * DEC_SEQ
N_TOK = N_CTX + N_LAT
COND_ROWS = 8

ROW_TILE = 512
SUB_TILE = 256
FFN_TILE = 512
FF_CHUNK = 256
MOD_TILE = 2304
Q_TILE = 512
CTX_SEQS = 4
V7X_VMEM_BYTES = 64 * 1024 * 1024
VMEM_LIMIT_BYTES = V7X_VMEM_BYTES - 4 * 1024 * 1024

F32 = jnp.float32
BF16 = jnp.bfloat16


def _params():
    return pltpu.CompilerParams(vmem_limit_bytes=VMEM_LIMIT_BYTES)


def _resident(block_shape, index_map):
    return pl.BlockSpec(block_shape, index_map, pipeline_mode=pl.Buffered(1))


def _dot(a, b):
    return jnp.dot(a, b, preferred_element_type=F32)


def _dot_nt(a, b):
    return lax.dot_general(a, b, (((1,), (1,)), ((), ())), preferred_element_type=F32)


def _rms(x):
    return x * lax.rsqrt(jnp.mean(x * x, axis=-1, keepdims=True) + EPS)


def _norm_mod(x, g, shift, scale):
    return _rms(x) * g * (1.0 + scale) + shift


def _stacked_mod_row(i, tile=ROW_TILE):
    n_ctx_tiles = N_CTX // tile
    return jnp.where(i < n_ctx_tiles, 0, 1 + (i - n_ctx_tiles) // (DEC_SEQ // tile))


def _mod_chunk(mod_ref, row, k):
    return mod_ref[pl.ds(row, 1), k * D_MODEL:(k + 1) * D_MODEL]


def _mod_kernel(c_ref, w_ref, b_ref, o_ref):
    c = c_ref[...]
    o_ref[...] = _dot(c * jax.nn.sigmoid(c), w_ref[...]) + b_ref[...]


def _modulation(cond, w_mod, b_mod):
    n = N_MOD * D_MODEL
    return pl.pallas_call(
        _mod_kernel,
        grid=(DEPTH, n // MOD_TILE),
        in_specs=[
            pl.BlockSpec((COND_ROWS, D_MODEL), lambda l, j: (0, 0)),
            pl.BlockSpec((None, D_MODEL, MOD_TILE), lambda l, j: (l, 0, j)),
            pl.BlockSpec((None, 1, MOD_TILE), lambda l, j: (l, 0, j)),
        ],
        out_specs=pl.BlockSpec((None, COND_ROWS, MOD_TILE), lambda l, j: (l, 0, j)),
        out_shape=jax.ShapeDtypeStruct((DEPTH, COND_ROWS, n), F32),
        compiler_params=_params(),
        name="modulation",
    )(cond, w_mod, b_mod.reshape(DEPTH, 1, n))


def _ffn_weight_copies(w13_hbm, w2_hbm, w13_ref, w2_ref, sems, l, c):
    lo = pl.multiple_of(c * FF_CHUNK, FF_CHUNK)
    cols, up_cols = pl.ds(lo, FF_CHUNK), pl.ds(D_FF + lo, FF_CHUNK)
    return (pltpu.make_async_copy(w13_hbm.at[l, :, cols], w13_ref.at[:, cols], sems.at[c, 0]),
            pltpu.make_async_copy(w13_hbm.at[l, :, up_cols], w13_ref.at[:, up_cols], sems.at[c, 1]),
            pltpu.make_async_copy(w2_hbm.at[l, cols, :], w2_ref.at[cols, :], sems.at[c, 2]))


def _ffn_kernel(*refs, l, sub, split_in, final):
    n_x = 2 if split_in else 1
    x_refs, (mod_ref, g_ref, w13_hbm, w2_hbm) = refs[:n_x], refs[n_x:n_x + 4]
    *rest, w13_ref, w2_ref, acc_ref, sems = refs[n_x + 4:]
    i = pl.program_id(0)
    is_ctx = i < N_CTX // FFN_TILE
    row = _stacked_mod_row(i, FFN_TILE)
    n_chunks = D_FF // FF_CHUNK

    def normed():
        x = jnp.where(is_ctx, x_refs[0][...], x_refs[1][...]) if split_in else x_refs[0][...]
        return x, _norm_mod(x, g_ref[sub:sub + 1, :], _mod_chunk(mod_ref, row, 3 * sub),
                            _mod_chunk(mod_ref, row, 3 * sub + 1))

    def swiglu(gate, up, w2):
        return _dot(gate * jax.nn.sigmoid(gate) * up, w2)

    def finish(x, acc):
        y = x + 0.5 * _mod_chunk(mod_ref, row, 3 * sub + 2) * acc
        if not final:
            rest[0][...] = y
            return
        gf_ref, ctx_ref, lat_ref = rest
        y = _rms(y) * gf_ref[...]

        @pl.when(is_ctx)
        def _():
            ctx_ref[...] = y

        @pl.when(jnp.logical_not(is_ctx))
        def _():
            lat_ref[...] = y

    def first_step():
        def start(c, _):
            for cp in _ffn_weight_copies(w13_hbm, w2_hbm, w13_ref, w2_ref, sems, l, c):
                cp.start()
            return 0

        lax.fori_loop(0, n_chunks, start, 0)
        x, h = normed()
        acc_ref[...] = jnp.zeros_like(acc_ref)

        def chunk(c, _):
            for cp in _ffn_weight_copies(w13_hbm, w2_hbm, w13_ref, w2_ref, sems, l, c):
                cp.wait()
            lo = pl.multiple_of(c * FF_CHUNK, FF_CHUNK)
            gate = _dot(h, w13_ref[:, pl.ds(lo, FF_CHUNK)])
            up = _dot(h, w13_ref[:, pl.ds(D_FF + lo, FF_CHUNK)])
            acc_ref[...] += swiglu(gate, up, w2_ref[pl.ds(lo, FF_CHUNK), :])
            return 0

        lax.fori_loop(0, n_chunks, chunk, 0)
        finish(x, acc_ref[...])

    def later_step():
        x, h = normed()

        def gate_up(lo):
            return (_dot(h, w13_ref[:, lo:lo + FF_CHUNK]),
                    _dot(h, w13_ref[:, D_FF + lo:D_FF + lo + FF_CHUNK]))

        acc = jnp.zeros((FFN_TILE, D_MODEL), F32)
        nxt = gate_up(0)
        for lo in range(0, D_FF, FF_CHUNK):
            gate, up = nxt
            if lo + FF_CHUNK < D_FF:
                nxt = gate_up(lo + FF_CHUNK)
            acc = acc + swiglu(gate, up, w2_ref[lo:lo + FF_CHUNK, :])
        finish(x, acc)

    lax.cond(i == 0, first_step, later_step)


def _ffn(xs, mod, g_norm, w13, w2, g_final, l, sub):
    split_in, final = isinstance(xs, tuple), g_final is not None
    n_ctx_tiles = N_CTX // FFN_TILE
    tile = pl.BlockSpec((FFN_TILE, D_MODEL), lambda i: (i, 0))
    ctx_tile = pl.BlockSpec((FFN_TILE, D_MODEL), lambda i: (jnp.minimum(i, n_ctx_tiles - 1), 0))
    lat_tile = pl.BlockSpec((FFN_TILE, D_MODEL), lambda i: (jnp.maximum(i - n_ctx_tiles, 0), 0))
    in_specs = ([ctx_tile, lat_tile] if split_in else [tile]) + [
        _resident((None, COND_ROWS, N_MOD * D_MODEL), lambda i: (l, 0, 0)),
        _resident((None, 3, D_MODEL), lambda i: (l, 0, 0)),
        pl.BlockSpec(memory_space=pl.ANY),
        pl.BlockSpec(memory_space=pl.ANY),
    ]
    args = (list(xs) if split_in else [xs]) + [mod, g_norm, w13, w2]
    if final:
        in_specs.append(_resident((1, D_MODEL), lambda i: (0, 0)))
        args.append(g_final.reshape(1, D_MODEL))
        out_specs = [ctx_tile, lat_tile]
        out_shape = [jax.ShapeDtypeStruct((N_CTX, D_MODEL), F32), jax.ShapeDtypeStruct((N_LAT, D_MODEL), F32)]
    else:
        out_specs, out_shape = tile, jax.ShapeDtypeStruct((N_TOK, D_MODEL), F32)
    return pl.pallas_call(
        functools.partial(_ffn_kernel, l=l, sub=sub, split_in=split_in, final=final),
        grid=(N_TOK // FFN_TILE,),
        in_specs=in_specs,
        out_specs=out_specs,
        out_shape=out_shape,
        scratch_shapes=[pltpu.VMEM((D_MODEL, 2 * D_FF), F32), pltpu.VMEM((D_FF, D_MODEL), F32),
                        pltpu.VMEM((FFN_TILE, D_MODEL), F32),
                        pltpu.SemaphoreType.DMA((D_FF // FF_CHUNK, 3))],
        input_output_aliases={} if split_in or final else {0: 0},
        compiler_params=_params(),
        name="ffn",
    )(*args)


def _gelu_tanh(x):
    return x * (0.5 * (1.0 + jnp.tanh(math.sqrt(2.0 / math.pi) * (x + 0.044715 * (x * x * x)))))


def _rope_tables():
    pos = np.arange(DEC_SEQ)
    lane = np.arange(V_DIM)
    half = HEAD_DIM // 2
    within = lane % half
    coord = np.where((lane % HEAD_DIM) < half, pos[:, None] // GRID_W, pos[:, None] % GRID_W)
    freqs = ROPE_THETA ** (-np.arange(0, half, 2, dtype=np.float64) / half)
    ang = coord * freqs[within % (half // 2)][None, :]
    first = (within < half // 2)[None, :]
    identity = np.zeros((ROW_TILE, V_DIM))
    cos = np.concatenate([np.cos(ang), identity + 1.0])
    sin_lo = np.concatenate([np.where(first, -np.sin(ang), 0.0), identity])
    sin_hi = np.concatenate([np.where(first, 0.0, np.sin(ang)), identity])
    return tuple(jnp.asarray(t, dtype=F32) for t in (cos, sin_lo, sin_hi))


def _proj_weight_copies(w_in_hbm, wz_ref, wqkv_ref, sems, l):
    copies = [pltpu.make_async_copy(w_in_hbm.at[l, :, pl.ds(0, 2 * D_MODEL)], wz_ref, sems.at[0])]
    for k in range(3):
        src = w_in_hbm.at[l, :, pl.ds((2 + k) * D_MODEL, D_MODEL)]
        copies.append(pltpu.make_async_copy(src, wqkv_ref.at[k], sems.at[1 + k]))
    return copies


def _proj_kernel(x_ref, mod_ref, g_ref, w_in_hbm, gain_ref, ws_ref, bias_ref, cos_ref, lo_ref, hi_ref,
                 *rest, l):
    a_ref, qkv_ref, kc_ref, vc_ref, wz_ref, wqkv_ref, sems = rest[-7:]
    i = pl.program_id(0)

    def first_step():
        copies = _proj_weight_copies(w_in_hbm, wz_ref, wqkv_ref, sems, l)
        for cp in copies:
            cp.start()
        _proj_body(x_ref, mod_ref, g_ref, wz_ref, wqkv_ref, gain_ref, ws_ref, bias_ref, cos_ref, lo_ref, hi_ref,
                   a_ref, qkv_ref, kc_ref, vc_ref, l, copies)

    def later_step():
        _proj_body(x_ref, mod_ref, g_ref, wz_ref, wqkv_ref, gain_ref, ws_ref, bias_ref, cos_ref, lo_ref, hi_ref,
                   a_ref, qkv_ref, kc_ref, vc_ref, l, None)

    lax.cond(i == 0, first_step, later_step)


def _proj_body(x_ref, mod_ref, g_ref, wz_ref, wqkv_ref, gain_ref, ws_ref, bias_ref, cos_ref, lo_ref, hi_ref,
               a_ref, qkv_ref, kc_ref, vc_ref, l, copies):
    i = pl.program_id(0)
    row = _stacked_mod_row(i)
    pending = dict(enumerate(copies or []))

    def arrived(k):
        if k in pending:
            pending.pop(k).wait()

    def project(r0):
        h = _norm_mod(x_ref[r0:r0 + SUB_TILE, :], g_ref[1:2, :], _mod_chunk(mod_ref, row, 3),
                      _mod_chunk(mod_ref, row, 4))
        arrived(0)
        z = _gelu_tanh(_dot(h, wz_ref[...]))
        qkv = []
        for k in range(3):
            arrived(1 + k)
            qkv.append(_dot(h, wqkv_ref[k]))
        return z, qkv[0] * (HEAD_DIM ** -0.5), qkv[1], qkv[2]

    def gate(r0, z):
        u = z[:, :D_MODEL]
        v = _rms(z[:, D_MODEL:]) * gain_ref[l:l + 1, :]
        for c in range(0, SUB_TILE, CHUNK):
            for g in range(N_GROUPS_A):
                c0 = g * GROUP_W
                mixed = _dot(ws_ref[l, g], v[c:c + CHUNK, c0:c0 + GROUP_W]) + bias_ref[:, c0:c0 + GROUP_W]
                a_ref[r0 + c:r0 + c + CHUNK, c0:c0 + GROUP_W] = (
                    u[c:c + CHUNK, c0:c0 + GROUP_W] * mixed).astype(BF16)

    def rotate(r0, q, k):
        rows = slice(r0, r0 + SUB_TILE)
        cos, sin_lo, sin_hi = cos_ref[rows, :], lo_ref[rows, :], hi_ref[rows, :]
        for c0 in range(0, D_MODEL, V_DIM):
            for src, base in ((q, 0), (k, D_MODEL)):
                t = src[:, c0:c0 + V_DIM]
                t = (t * cos + pltpu.roll(t, V_DIM - HEAD_DIM // 4, 1) * sin_lo
                     + pltpu.roll(t, HEAD_DIM // 4, 1) * sin_hi)
                qkv_ref[rows, base + c0:base + c0 + V_DIM] = t.astype(BF16)

    starts = list(range(0, ROW_TILE, SUB_TILE))
    kv = []
    nxt = project(starts[0])
    for n, r0 in enumerate(starts):
        z, q, k, v = nxt
        if n + 1 < len(starts):
            nxt = project(starts[n + 1])
        gate(r0, z)
        rotate(r0, q, k)
        qkv_ref[r0:r0 + SUB_TILE, 2 * D_MODEL:] = v.astype(BF16)
        kv.append((k, v))

    @pl.when(i < N_CTX // ROW_TILE)
    def _():
        for n, (k, v) in enumerate(kv):
            kc_ref[n] = k.reshape(SEQ, N_HEADS, V_DIM)
            vc_ref[n] = v.reshape(SEQ, N_HEADS, V_DIM)


def _proj(x, mod, g_norm, w_in, sgu_gain, w_spatial, bias_tile, rope, caches, l):
    assert SUB_TILE == SEQ
    n_ctx_tiles, lat_tiles = N_CTX // ROW_TILE, DEC_SEQ // ROW_TILE
    tile = lambda w: pl.BlockSpec((ROW_TILE, w), lambda i: (i, 0))
    table = pl.BlockSpec((ROW_TILE, V_DIM),
                         lambda i: (jnp.where(i < n_ctx_tiles, lat_tiles, (i - n_ctx_tiles) % lat_tiles), 0))
    cache_block = pl.BlockSpec((ROW_TILE // SEQ, None, SEQ, N_HEADS, V_DIM),
                               lambda i: (jnp.minimum(i, n_ctx_tiles - 1), l, 0, 0, 0))
    cache = jax.ShapeDtypeStruct((BATCH, DEPTH, SEQ, N_HEADS, V_DIM), F32)
    in_specs = [
        tile(D_MODEL),
        _resident((None, COND_ROWS, N_MOD * D_MODEL), lambda i: (l, 0, 0)),
        _resident((None, 3, D_MODEL), lambda i: (l, 0, 0)),
        pl.BlockSpec(memory_space=pl.ANY),
        _resident((DEPTH, D_MODEL), lambda i: (0, 0)),
        _resident((DEPTH, N_GROUPS_A, CHUNK, CHUNK), lambda i: (0, 0, 0, 0)),
        _resident((None, CHUNK, D_MODEL), lambda i: (l, 0, 0)),
        table, table, table,
    ]
    args = [x, mod, g_norm, w_in, sgu_gain, w_spatial, bias_tile, *rope]
    aliases = {}
    if caches is not None:
        in_specs += [pl.BlockSpec(memory_space=pl.ANY)] * 2
        aliases = {len(args): 2, len(args) + 1: 3}
        args += list(caches)
    return pl.pallas_call(
        functools.partial(_proj_kernel, l=l),
        grid=(N_TOK // ROW_TILE,),
        in_specs=in_specs,
        out_specs=[tile(D_MODEL), tile(3 * D_MODEL), cache_block, cache_block],
        out_shape=[jax.ShapeDtypeStruct((N_TOK, D_MODEL), BF16),
                   jax.ShapeDtypeStruct((N_TOK, 3 * D_MODEL), BF16), cache, cache],
        scratch_shapes=[pltpu.VMEM((D_MODEL, 2 * D_MODEL), F32), pltpu.VMEM((3, D_MODEL, D_MODEL), F32),
                        pltpu.SemaphoreType.DMA((4,))],
        input_output_aliases=aliases,
        compiler_params=_params(),
        name="proj",
    )(*args)


def _lam_full(lam_ref, l):
    lp = lam_ref[l]
    s01 = jnp.sum(lp[0:1, :] * lp[1:2, :], axis=-1, keepdims=True)
    s23 = jnp.sum(lp[2:3, :] * lp[3:4, :], axis=-1, keepdims=True)
    return jnp.exp(s01) - jnp.exp(s23) + _lam_init(l)


def _lam_init(l):
    return 0.8 - 0.6 * math.exp(-0.3 * l)


def _attn_scores(q, ks):
    lane = lax.broadcasted_iota(jnp.int32, (1, V_DIM), 1)
    lo = (lane < HEAD_DIM).astype(BF16)
    qs = jnp.concatenate([q * lo, q * (1 - lo)], axis=0)
    return [_dot_nt(qs, k) for k in ks]


def _attn_combine(s, vs, lam_full, gain, l):
    n_q = s[0].shape[0] // 2
    m = functools.reduce(jnp.maximum, [jnp.max(t, axis=-1, keepdims=True) for t in s])
    pv = functools.reduce(jnp.add, [
        _dot(jnp.exp(t - m).astype(BF16), jnp.concatenate([v, jnp.ones_like(v)], axis=1))
        for t, v in zip(s, vs)])
    pv = pv[:, :V_DIM] / pv[:, V_DIM:]
    o = pv[:n_q] - lam_full * pv[n_q:]
    return _rms(o) * gain * (1.0 - _lam_init(l))


def _attn_heads(q_ref, k_refs, v_refs, lam_ref, gain_ref, o_ref, l, n_seqs=1):
    lam_full = _lam_full(lam_ref, l)
    gain = gain_ref[l:l + 1, :]
    units = [(sq, hd) for hd in range(N_HEADS) for sq in range(n_seqs)]

    def part(ref, sq, hd):
        n = ref.shape[0] // n_seqs
        return ref[sq * n:(sq + 1) * n, hd * V_DIM:(hd + 1) * V_DIM].astype(BF16)

    def scores(sq, hd):
        return _attn_scores(part(q_ref, sq, hd), [part(r, sq, hd) for r in k_refs])

    nxt = scores(*units[0])
    for n, (sq, hd) in enumerate(units):
        s = nxt
        if n + 1 < len(units):
            nxt = scores(*units[n + 1])
        o = _attn_combine(s, [part(r, sq, hd) for r in v_refs], lam_full, gain, l)
        rows = q_ref.shape[0] // n_seqs
        o_ref[sq * rows:(sq + 1) * rows, hd * V_DIM:(hd + 1) * V_DIM] = o.astype(BF16)


def _attn_ctx_kernel(q_ref, k_ref, v_ref, lam_ref, gain_ref, o_ref, *, l):
    _attn_heads(q_ref, [k_ref], [v_ref], lam_ref, gain_ref, o_ref, l, n_seqs=CTX_SEQS)


def _attn_lat_kernel(q_ref, k_ref, v_ref, past_k, past_v, lam_ref, gain_ref, _, o_ref, pk_ref, pv_ref, *, l):
    pk_ref[...] = past_k[...].reshape(PAST_LEN, D_MODEL).astype(BF16)
    pv_ref[...] = past_v[...].reshape(PAST_LEN, D_MODEL).astype(BF16)
    _attn_heads(q_ref, [pk_ref, k_ref], [pv_ref, v_ref], lam_ref, gain_ref, o_ref, l)


def _attn_ctx(qkv, lam, subln_gain, l):
    rows = CTX_SEQS * SEQ
    part = [pl.BlockSpec((rows, D_MODEL), functools.partial(lambda i, c: (i, c), c=c)) for c in range(3)]
    return pl.pallas_call(
        functools.partial(_attn_ctx_kernel, l=l),
        grid=(BATCH // CTX_SEQS,),
        in_specs=part + [_resident((DEPTH, 4, HEAD_DIM), lambda i: (0, 0, 0)),
                         _resident((DEPTH, V_DIM), lambda i: (0, 0))],
        out_specs=pl.BlockSpec((rows, D_MODEL), lambda i: (i, 0)),
        out_shape=jax.ShapeDtypeStruct((N_TOK, D_MODEL), BF16),
        compiler_params=_params(),
        name="attn_context",
    )(qkv, qkv, qkv, lam, subln_gain)


def _attn_lat(qkv, past_k, past_v, lam, subln_gain, b_all, l):
    tiles = DEC_SEQ // Q_TILE
    qspec = pl.BlockSpec((Q_TILE, D_MODEL), lambda b, j: (N_CTX // Q_TILE + b * tiles + j, 0))
    kspec = pl.BlockSpec((DEC_SEQ, D_MODEL), lambda b, j: (N_CTX // DEC_SEQ + b, 1))
    vspec = pl.BlockSpec((DEC_SEQ, D_MODEL), lambda b, j: (N_CTX // DEC_SEQ + b, 2))
    past = pl.BlockSpec((None, None, PAST_LEN, N_HEADS, V_DIM), lambda b, j: (b, l, 0, 0, 0))
    return pl.pallas_call(
        functools.partial(_attn_lat_kernel, l=l),
        grid=(DEC_BATCH, tiles),
        in_specs=[qspec, kspec, vspec, past, past,
                  _resident((DEPTH, 4, HEAD_DIM), lambda b, j: (0, 0, 0)),
                  _resident((DEPTH, V_DIM), lambda b, j: (0, 0)),
                  pl.BlockSpec(memory_space=pl.ANY)],
        out_specs=pl.BlockSpec((Q_TILE, D_MODEL), lambda b, j: (N_CTX // Q_TILE + b * tiles + j, 0)),
        out_shape=jax.ShapeDtypeStruct((N_TOK, D_MODEL), BF16),
        scratch_shapes=[pltpu.VMEM((PAST_LEN, D_MODEL), BF16)] * 2,
        input_output_aliases={7: 0},
        compiler_params=_params(),
        name="attn_latent",
    )(qkv, qkv, qkv, past_k, past_v, lam, subln_gain, b_all)


def _merge_weight_copies(w_in_hbm, wa_hbm, wb_hbm, wo_hbm, w_ref, sems, l):
    sources = [w_in_hbm.at[l, :, pl.ds(5 * D_MODEL, D_MODEL)], w_in_hbm.at[l, :, pl.ds(6 * D_MODEL, D_MODEL)],
               wa_hbm.at[l], wb_hbm.at[l], wo_hbm.at[l]]
    return [pltpu.make_async_copy(src, w_ref.at[k], sems.at[k]) for k, src in enumerate(sources)]


def _merge_kernel(x_ref, a_ref, b_ref, mod_ref, g_ref, w_in_hbm, wa_hbm, wb_hbm, wo_hbm, o_ref, w_ref, sems, *, l):
    i = pl.program_id(0)
    row = _stacked_mod_row(i)

    def body(copies):
        def weight(k):
            if copies is not None:
                copies[k].wait()
            return w_ref[k]

        x = x_ref[...]
        h = _norm_mod(x, g_ref[1:2, :], _mod_chunk(mod_ref, row, 3), _mod_chunk(mod_ref, row, 4))
        gate_a = jax.nn.sigmoid(_dot(h, weight(0)))
        gate_b = jax.nn.sigmoid(_dot(h, weight(1)))
        merged = (gate_a * _dot(a_ref[...].astype(F32), weight(2))
                  + gate_b * _dot(b_ref[...].astype(F32), weight(3)))
        o_ref[...] = x + _mod_chunk(mod_ref, row, 5) * _dot(merged, weight(4))

    def first_step():
        copies = _merge_weight_copies(w_in_hbm, wa_hbm, wb_hbm, wo_hbm, w_ref, sems, l)
        for cp in copies:
            cp.start()
        body(copies)

    lax.cond(i == 0, first_step, lambda: body(None))


def _merge(x, a, b, mod, g_norm, w_in, w_branch_a, w_branch_b, w_out, l):
    tile = pl.BlockSpec((ROW_TILE, D_MODEL), lambda i: (i, 0))
    hbm = pl.BlockSpec(memory_space=pl.ANY)
    return pl.pallas_call(
        functools.partial(_merge_kernel, l=l),
        grid=(N_TOK // ROW_TILE,),
        in_specs=[tile, tile, tile,
                  _resident((None, COND_ROWS, N_MOD * D_MODEL), lambda i: (l, 0, 0)),
                  _resident((None, 3, D_MODEL), lambda i: (l, 0, 0)),
                  hbm, hbm, hbm, hbm],
        out_specs=tile,
        out_shape=jax.ShapeDtypeStruct((N_TOK, D_MODEL), F32),
        scratch_shapes=[pltpu.VMEM((5, D_MODEL, D_MODEL), F32), pltpu.SemaphoreType.DMA((5,))],
        input_output_aliases={0: 0},
        compiler_params=_params(),
        name="merge",
    )(x, a, b, mod, g_norm, w_in, w_branch_a, w_branch_b, w_out)


def kernel(x_prompt, x_sample, cache_k, cache_v, c, c_ctx, w_mod, b_mod, g_norm, ffn1_w13, ffn1_w2, w_in,
           sgu_gain, w_spatial, b_spatial, lam, subln_gain, w_branch_a, w_branch_b, w_out, ffn2_w13,
           ffn2_w2, g_final):
    cond = jnp.concatenate(
        [c_ctx[None, :], c, jnp.zeros((COND_ROWS - 1 - DEC_BATCH, D_MODEL), F32)], axis=0)
    mod = _modulation(cond, w_mod, b_mod)
    bias_tile = jnp.repeat(jnp.swapaxes(b_spatial, 1, 2), GROUP_W, axis=2)
    rope = _rope_tables()

    x = (x_prompt.reshape(N_CTX, D_MODEL), x_sample.reshape(N_LAT, D_MODEL))
    caches = None
    for l in range(DEPTH):
        x = _ffn(x, mod, g_norm, ffn1_w13, ffn1_w2, None, l, 0)
        a, qkv, *caches = _proj(x, mod, g_norm, w_in, sgu_gain, w_spatial, bias_tile, rope, caches, l)
        b = _attn_ctx(qkv, lam, subln_gain, l)
        b = _attn_lat(qkv, cache_k, cache_v, lam, subln_gain, b, l)
        x = _merge(x, a, b, mod, g_norm, w_in, w_branch_a, w_branch_b, w_out, l)
        x = _ffn(x, mod, g_norm, ffn2_w13, ffn2_w2, g_final if l == DEPTH - 1 else None, l, 2)

    y_ctx, y_lat = x
    new_k, new_v = caches
    return y_ctx.reshape(BATCH, SEQ, D_MODEL), y_lat.reshape(DEC_BATCH, DEC_SEQ, D_MODEL), new_k, new_v
```

```python
import functools
import math

import numpy as np
import jax
import jax.numpy as jnp
from jax import lax
from jax.experimental import pallas as pl
from jax.experimental.pallas import tpu as pltpu

D_MODEL = 1024
BATCH = 16
SEQ = 256
DEPTH = 4
DEC_BATCH = 2
DEC_SEQ = 1024
PAST_LEN = 256
GRID_W = 64
CHUNK = 128
N_GROUPS_A = 4
GROUP_W = D_MODEL // N_GROUPS_A
N_HEADS = 8
HEAD_DIM = D_MODEL // (2 * N_HEADS)
V_DIM = 2 * HEAD_DIM
D_FF = ((8 * D_MODEL // 3 + 127) // 128) * 128
IN_W = 7 * D_MODEL
N_MOD = 9
ROPE_THETA = 10000.0
EPS = 1e-6

N_CTX = BATCH * SEQ
N_LAT = DEC_BATCH * DEC_SEQ
N_TOK = N_CTX + N_LAT
COND_ROWS = 8

ROW_TILE = 512
SUB_TILE = 256
FFN_TILE = 512
FF_CHUNK = 256
MOD_TILE = 2304
Q_TILE = 512
V7X_VMEM_BYTES = 64 * 1024 * 1024
VMEM_LIMIT_BYTES = V7X_VMEM_BYTES - 4 * 1024 * 1024

F32 = jnp.float32
BF16 = jnp.bfloat16


def _params():
    return pltpu.CompilerParams(vmem_limit_bytes=VMEM_LIMIT_BYTES)


def _resident(block_shape, index_map):
    return pl.BlockSpec(block_shape, index_map, pipeline_mode=pl.Buffered(1))


def _dot(a, b):
    return jnp.dot(a, b, preferred_element_type=F32)


def _dot_nt(a, b):
    return lax.dot_general(a, b, (((1,), (1,)), ((), ())), preferred_element_type=F32)


def _rms(x):
    return x * lax.rsqrt(jnp.mean(x * x, axis=-1, keepdims=True) + EPS)


def _norm_mod(x, g, shift, scale):
    return _rms(x) * g * (1.0 + scale) + shift


def _stacked_mod_row(i, tile=ROW_TILE):
    n_ctx_tiles = N_CTX // tile
    return jnp.where(i < n_ctx_tiles, 0, 1 + (i - n_ctx_tiles) // (DEC_SEQ // tile))


def _mod_chunk(mod_ref, row, k):
    return mod_ref[pl.ds(row, 1), k * D_MODEL:(k + 1) * D_MODEL]


def _mod_kernel(c_ref, w_ref, b_ref, o_ref):
    c = c_ref[...]
    o_ref[...] = _dot(c * jax.nn.sigmoid(c), w_ref[...]) + b_ref[...]


def _modulation(cond, w_mod, b_mod):
    n = N_MOD * D_MODEL
    return pl.pallas_call(
        _mod_kernel,
        grid=(DEPTH, n // MOD_TILE),
        in_specs=[
            pl.BlockSpec((COND_ROWS, D_MODEL), lambda l, j: (0, 0)),
            pl.BlockSpec((None, D_MODEL, MOD_TILE), lambda l, j: (l, 0, j)),
            pl.BlockSpec((None, 1, MOD_TILE), lambda l, j: (l, 0, j)),
        ],
        out_specs=pl.BlockSpec((None, COND_ROWS, MOD_TILE), lambda l, j: (l, 0, j)),
        out_shape=jax.ShapeDtypeStruct((DEPTH, COND_ROWS, n), F32),
        compiler_params=_params(),
        name="modulation",
    )(cond, w_mod, b_mod.reshape(DEPTH, 1, n))


def _ffn_weight_copies(w13_hbm, w2_hbm, w13_ref, w2_ref, sems, l, c):
    lo = pl.multiple_of(c * FF_CHUNK, FF_CHUNK)
    cols, up_cols = pl.ds(lo, FF_CHUNK), pl.ds(D_FF + lo, FF_CHUNK)
    return (pltpu.make_async_copy(w13_hbm.at[l, :, cols], w13_ref.at[:, cols], sems.at[c, 0]),
            pltpu.make_async_copy(w13_hbm.at[l, :, up_cols], w13_ref.at[:, up_cols], sems.at[c, 1]),
            pltpu.make_async_copy(w2_hbm.at[l, cols, :], w2_ref.at[cols, :], sems.at[c, 2]))


def _ffn_kernel(*refs, l, sub, split_in, final):
    n_x = 2 if split_in else 1
    x_refs, (mod_ref, g_ref, w13_hbm, w2_hbm) = refs[:n_x], refs[n_x:n_x + 4]
    *rest, w13_ref, w2_ref, acc_ref, sems = refs[n_x + 4:]
    i = pl.program_id(0)
    is_ctx = i < N_CTX // FFN_TILE
    row = _stacked_mod_row(i, FFN_TILE)
    n_chunks = D_FF // FF_CHUNK

    def normed():
        x = jnp.where(is_ctx, x_refs[0][...], x_refs[1][...]) if split_in else x_refs[0][...]
        return x, _norm_mod(x, g_ref[sub:sub + 1, :], _mod_chunk(mod_ref, row, 3 * sub),
                            _mod_chunk(mod_ref, row, 3 * sub + 1))

    def swiglu(gate, up, w2):
        return _dot(gate * jax.nn.sigmoid(gate) * up, w2)

    def finish(x, acc):
        y = x + 0.5 * _mod_chunk(mod_ref, row, 3 * sub + 2) * acc
        if not final:
            rest[0][...] = y
            return
        gf_ref, ctx_ref, lat_ref = rest
        y = _rms(y) * gf_ref[...]

        @pl.when(is_ctx)
        def _():
            ctx_ref[...] = y

        @pl.when(jnp.logical_not(is_ctx))
        def _():
            lat_ref[...] = y

    def first_step():
        def start(c, _):
            for cp in _ffn_weight_copies(w13_hbm, w2_hbm, w13_ref, w2_ref, sems, l, c):
                cp.start()
            return 0

        lax.fori_loop(0, n_chunks, start, 0)
        x, h = normed()
        acc_ref[...] = jnp.zeros_like(acc_ref)

        def chunk(c, _):
            for cp in _ffn_weight_copies(w13_hbm, w2_hbm, w13_ref, w2_ref, sems, l, c):
                cp.wait()
            lo = pl.multiple_of(c * FF_CHUNK, FF_CHUNK)
            gate = _dot(h, w13_ref[:, pl.ds(lo, FF_CHUNK)])
            up = _dot(h, w13_ref[:, pl.ds(D_FF + lo, FF_CHUNK)])
            acc_ref[...] += swiglu(gate, up, w2_ref[pl.ds(lo, FF_CHUNK), :])
            return 0

        lax.fori_loop(0, n_chunks, chunk, 0)
        finish(x, acc_ref[...])

    def later_step():
        x, h = normed()

        def gate_up(lo):
            return (_dot(h, w13_ref[:, lo:lo + FF_CHUNK]),
                    _dot(h, w13_ref[:, D_FF + lo:D_FF + lo + FF_CHUNK]))

        acc = jnp.zeros((FFN_TILE, D_MODEL), F32)
        nxt = gate_up(0)
        for lo in range(0, D_FF, FF_CHUNK):
            gate, up = nxt
            if lo + FF_CHUNK < D_FF:
                nxt = gate_up(lo + FF_CHUNK)
            acc = acc + swiglu(gate, up, w2_ref[lo:lo + FF_CHUNK, :])
        finish(x, acc)

    lax.cond(i == 0, first_step, later_step)


def _ffn(xs, mod, g_norm, w13, w2, g_final, l, sub):
    split_in, final = isinstance(xs, tuple), g_final is not None
    n_ctx_tiles = N_CTX // FFN_TILE
    tile = pl.BlockSpec((FFN_TILE, D_MODEL), lambda i: (i, 0))
    ctx_tile = pl.BlockSpec((FFN_TILE, D_MODEL), lambda i: (jnp.minimum(i, n_ctx_tiles - 1), 0))
    lat_tile = pl.BlockSpec((FFN_TILE, D_MODEL), lambda i: (jnp.maximum(i - n_ctx_tiles, 0), 0))
    in_specs = ([ctx_tile, lat_tile] if split_in else [tile]) + [
        _resident((None, COND_ROWS, N_MOD * D_MODEL), lambda i: (l, 0, 0)),
        _resident((None, 3, D_MODEL), lambda i: (l, 0, 0)),
        pl.BlockSpec(memory_space=pl.ANY),
        pl.BlockSpec(memory_space=pl.ANY),
    ]
    args = (list(xs) if split_in else [xs]) + [mod, g_norm, w13, w2]
    if final:
        in_specs.append(_resident((1, D_MODEL), lambda i: (0, 0)))
        args.append(g_final.reshape(1, D_MODEL))
        out_specs = [ctx_tile, lat_tile]
        out_shape = [jax.ShapeDtypeStruct((N_CTX, D_MODEL), F32), jax.ShapeDtypeStruct((N_LAT, D_MODEL), F32)]
    else:
        out_specs, out_shape = tile, jax.ShapeDtypeStruct((N_TOK, D_MODEL), F32)
    return pl.pallas_call(
        functools.partial(_ffn_kernel, l=l, sub=sub, split_in=split_in, final=final),
        grid=(N_TOK // FFN_TILE,),
        in_specs=in_specs,
        out_specs=out_specs,
        out_shape=out_shape,
        scratch_shapes=[pltpu.VMEM((D_MODEL, 2 * D_FF), F32), pltpu.VMEM((D_FF, D_MODEL), F32),
                        pltpu.VMEM((FFN_TILE, D_MODEL), F32),
                        pltpu.SemaphoreType.DMA((D_FF // FF_CHUNK, 3))],
        input_output_aliases={} if split_in or final else {0: 0},
        compiler_params=_params(),
        name="ffn",
    )(*args)


def _gelu_tanh(x):
    return x * (0.5 * (1.0 + jnp.tanh(math.sqrt(2.0 / math.pi) * (x + 0.044715 * (x * x * x)))))


def _rope_tables():
    pos = np.arange(DEC_SEQ)
    lane = np.arange(V_DIM)
    half = HEAD_DIM // 2
    within = lane % half
    coord = np.where((lane % HEAD_DIM) < half, pos[:, None] // GRID_W, pos[:, None] % GRID_W)
    freqs = ROPE_THETA ** (-np.arange(0, half, 2, dtype=np.float64) / half)
    ang = coord * freqs[within % (half // 2)][None, :]
    first = (within < half // 2)[None, :]
    identity = np.zeros((ROW_TILE, V_DIM))
    cos = np.concatenate([np.cos(ang), identity + 1.0])
    sin_lo = np.concatenate([np.where(first, -np.sin(ang), 0.0), identity])
    sin_hi = np.concatenate([np.where(first, 0.0, np.sin(ang)), identity])
    return tuple(jnp.asarray(t, dtype=F32) for t in (cos, sin_lo, sin_hi))


def _proj_kernel(x_ref, mod_ref, g_ref, wz_ref, wq_ref, wk_ref, wv_ref, gain_ref, ws_ref, bias_ref,
                 cos_ref, lo_ref, hi_ref, *rest, l):
    a_ref, qkv_ref, kc_ref, vc_ref = rest[-4:]
    i = pl.program_id(0)
    row = _stacked_mod_row(i)

    def project(r0):
        h = _norm_mod(x_ref[r0:r0 + SUB_TILE, :], g_ref[1:2, :], _mod_chunk(mod_ref, row, 3),
                      _mod_chunk(mod_ref, row, 4))
        z = _gelu_tanh(_dot(h, wz_ref[...]))
        return z, _dot(h, wq_ref[...]) * (HEAD_DIM ** -0.5), _dot(h, wk_ref[...]), _dot(h, wv_ref[...])

    def gate(r0, z):
        u = z[:, :D_MODEL]
        v = _rms(z[:, D_MODEL:]) * gain_ref[l:l + 1, :]
        for c in range(0, SUB_TILE, CHUNK):
            for g in range(N_GROUPS_A):
                c0 = g * GROUP_W
                mixed = _dot(ws_ref[l, g], v[c:c + CHUNK, c0:c0 + GROUP_W]) + bias_ref[:, c0:c0 + GROUP_W]
                a_ref[r0 + c:r0 + c + CHUNK, c0:c0 + GROUP_W] = (
                    u[c:c + CHUNK, c0:c0 + GROUP_W] * mixed).astype(BF16)

    def rotate(r0, q, k):
        rows = slice(r0, r0 + SUB_TILE)
        cos, sin_lo, sin_hi = cos_ref[rows, :], lo_ref[rows, :], hi_ref[rows, :]
        for c0 in range(0, D_MODEL, V_DIM):
            for src, base in ((q, 0), (k, D_MODEL)):
                t = src[:, c0:c0 + V_DIM]
                t = (t * cos + pltpu.roll(t, V_DIM - HEAD_DIM // 4, 1) * sin_lo
                     + pltpu.roll(t, HEAD_DIM // 4, 1) * sin_hi)
                qkv_ref[rows, base + c0:base + c0 + V_DIM] = t.astype(BF16)

    starts = list(range(0, ROW_TILE, SUB_TILE))
    kv = []
    nxt = project(starts[0])
    for n, r0 in enumerate(starts):
        z, q, k, v = nxt
        if n + 1 < len(starts):
            nxt = project(starts[n + 1])
        gate(r0, z)
        rotate(r0, q, k)
        qkv_ref[r0:r0 + SUB_TILE, 2 * D_MODEL:] = v.astype(BF16)
        kv.append((k, v))

    @pl.when(i < N_CTX // ROW_TILE)
    def _():
        for n, (k, v) in enumerate(kv):
            kc_ref[n] = k.reshape(SEQ, N_HEADS, V_DIM)
            vc_ref[n] = v.reshape(SEQ, N_HEADS, V_DIM)


def _proj(x, mod, g_norm, w_in, sgu_gain, w_spatial, bias_tile, rope, caches, l):
    assert SUB_TILE == SEQ
    n_ctx_tiles, lat_tiles = N_CTX // ROW_TILE, DEC_SEQ // ROW_TILE
    tile = lambda w: pl.BlockSpec((ROW_TILE, w), lambda i: (i, 0))
    w_block = lambda c: _resident((None, D_MODEL, D_MODEL), lambda i: (l, 0, c))
    table = pl.BlockSpec((ROW_TILE, V_DIM),
                         lambda i: (jnp.where(i < n_ctx_tiles, lat_tiles, (i - n_ctx_tiles) % lat_tiles), 0))
    cache_block = pl.BlockSpec((ROW_TILE // SEQ, None, SEQ, N_HEADS, V_DIM),
                               lambda i: (jnp.minimum(i, n_ctx_tiles - 1), l, 0, 0, 0))
    cache = jax.ShapeDtypeStruct((BATCH, DEPTH, SEQ, N_HEADS, V_DIM), F32)
    in_specs = [
        tile(D_MODEL),
        _resident((None, COND_ROWS, N_MOD * D_MODEL), lambda i: (l, 0, 0)),
        _resident((None, 3, D_MODEL), lambda i: (l, 0, 0)),
        _resident((None, D_MODEL, 2 * D_MODEL), lambda i: (l, 0, 0)),
        w_block(2), w_block(3), w_block(4),
        _resident((DEPTH, D_MODEL), lambda i: (0, 0)),
        _resident((DEPTH, N_GROUPS_A, CHUNK, CHUNK), lambda i: (0, 0, 0, 0)),
        _resident((None, CHUNK, D_MODEL), lambda i: (l, 0, 0)),
        table, table, table,
    ]
    args = [x, mod, g_norm, w_in, w_in, w_in, w_in, sgu_gain, w_spatial, bias_tile, *rope]
    aliases = {}
    if caches is not None:
        in_specs += [pl.BlockSpec(memory_space=pl.ANY)] * 2
        aliases = {len(args): 2, len(args) + 1: 3}
        args += list(caches)
    return pl.pallas_call(
        functools.partial(_proj_kernel, l=l),
        grid=(N_TOK // ROW_TILE,),
        in_specs=in_specs,
        out_specs=[tile(D_MODEL), tile(3 * D_MODEL), cache_block, cache_block],
        out_shape=[jax.ShapeDtypeStruct((N_TOK, D_MODEL), BF16),
                   jax.ShapeDtypeStruct((N_TOK, 3 * D_MODEL), BF16), cache, cache],
        input_output_aliases=aliases,
        compiler_params=_params(),
        name="proj",
    )(*args)


def _lam_full(lam_ref, l):
    lp = lam_ref[l]
    s01 = jnp.sum(lp[0:1, :] * lp[1:2, :], axis=-1, keepdims=True)
    s23 = jnp.sum(lp[2:3, :] * lp[3:4, :], axis=-1, keepdims=True)
    return jnp.exp(s01) - jnp.exp(s23) + _lam_init(l)


def _lam_init(l):
    return 0.8 - 0.6 * math.exp(-0.3 * l)


def _attn_scores(q, ks):
    lane = lax.broadcasted_iota(jnp.int32, (1, V_DIM), 1)
    lo = (lane < HEAD_DIM).astype(BF16)
    qs = jnp.concatenate([q * lo, q * (1 - lo)], axis=0)
    return [_dot_nt(qs, k) for k in ks]


def _attn_combine(s, vs, lam_full, gain, l):
    n_q = s[0].shape[0] // 2
    m = functools.reduce(jnp.maximum, [jnp.max(t, axis=-1, keepdims=True) for t in s])
    pv = functools.reduce(jnp.add, [
        _dot(jnp.exp(t - m).astype(BF16), jnp.concatenate([v, jnp.ones_like(v)], axis=1))
        for t, v in zip(s, vs)])
    pv = pv[:, :V_DIM] / pv[:, V_DIM:]
    o = pv[:n_q] - lam_full * pv[n_q:]
    return _rms(o) * gain * (1.0 - _lam_init(l))


def _attn_heads(q_ref, k_refs, v_refs, lam_ref, gain_ref, o_ref, l, n_seqs=1):
    lam_full = _lam_full(lam_ref, l)
    gain = gain_ref[l:l + 1, :]
    units = [(sq, hd) for hd in range(N_HEADS) for sq in range(n_seqs)]

    def part(ref, sq, hd):
        n = ref.shape[0] // n_seqs
        return ref[sq * n:(sq + 1) * n, hd * V_DIM:(hd + 1) * V_DIM].astype(BF16)

    def scores(sq, hd):
        return _attn_scores(part(q_ref, sq, hd), [part(r, sq, hd) for r in k_refs])

    nxt = scores(*units[0])
    for n, (sq, hd) in enumerate(units):
        s = nxt
        if n + 1 < len(units):
            nxt = scores(*units[n + 1])
        o = _attn_combine(s, [part(r, sq, hd) for r in v_refs], lam_full, gain, l)
        rows = q_ref.shape[0] // n_seqs
        o_ref[sq * rows:(sq + 1) * rows, hd * V_DIM:(hd + 1) * V_DIM] = o.astype(BF16)


def _attn_lat_kernel(q_ref, k_ref, v_ref, past_k, past_v, lam_ref, gain_ref, o_ref, pk_ref, pv_ref, *, l):
    pk_ref[...] = past_k[...].reshape(PAST_LEN, D_MODEL).astype(BF16)
    pv_ref[...] = past_v[...].reshape(PAST_LEN, D_MODEL).astype(BF16)
    _attn_heads(q_ref, [pk_ref, k_ref], [pv_ref, v_ref], lam_ref, gain_ref, o_ref, l)


def _attn_lat(qkv, past_k, past_v, lam, subln_gain, l):
    tiles = DEC_SEQ // Q_TILE
    qspec = pl.BlockSpec((Q_TILE, D_MODEL), lambda b, j: (N_CTX // Q_TILE + b * tiles + j, 0))
    kspec = pl.BlockSpec((DEC_SEQ, D_MODEL), lambda b, j: (N_CTX // DEC_SEQ + b, 1))
    vspec = pl.BlockSpec((DEC_SEQ, D_MODEL), lambda b, j: (N_CTX // DEC_SEQ + b, 2))
    past = pl.BlockSpec((None, None, PAST_LEN, N_HEADS, V_DIM), lambda b, j: (b, l, 0, 0, 0))
    return pl.pallas_call(
        functools.partial(_attn_lat_kernel, l=l),
        grid=(DEC_BATCH, tiles),
        in_specs=[qspec, kspec, vspec, past, past,
                  _resident((DEPTH, 4, HEAD_DIM), lambda b, j: (0, 0, 0)),
                  _resident((DEPTH, V_DIM), lambda b, j: (0, 0))],
        out_specs=pl.BlockSpec((Q_TILE, D_MODEL), lambda b, j: (b * tiles + j, 0)),
        out_shape=jax.ShapeDtypeStruct((N_LAT, D_MODEL), BF16),
        scratch_shapes=[pltpu.VMEM((PAST_LEN, D_MODEL), BF16)] * 2,
        compiler_params=_params(),
        name="attn_latent",
    )(qkv, qkv, qkv, past_k, past_v, lam, subln_gain)


def _merge_weight_copies(w_in_hbm, wa_hbm, wb_hbm, wo_hbm, w_ref, sems, l):
    sources = [w_in_hbm.at[l, :, pl.ds(5 * D_MODEL, D_MODEL)], w_in_hbm.at[l, :, pl.ds(6 * D_MODEL, D_MODEL)],
               wa_hbm.at[l], wb_hbm.at[l], wo_hbm.at[l]]
    return [pltpu.make_async_copy(src, w_ref.at[k], sems.at[k]) for k, src in enumerate(sources)]


def _merge_kernel(x_ref, a_ref, b_lat_ref, q_ref, k_ref, v_ref, lam_ref, subln_ref, mod_ref, g_ref,
                  w_in_hbm, wa_hbm, wb_hbm, wo_hbm, o_ref, b_ref, w_ref, sems, *, l):
    i = pl.program_id(0)
    row = _stacked_mod_row(i)
    is_ctx = i < N_CTX // ROW_TILE

    @pl.when(is_ctx)
    def _():
        _attn_heads(q_ref, [k_ref], [v_ref], lam_ref, subln_ref, b_ref, l, n_seqs=ROW_TILE // SEQ)

    @pl.when(jnp.logical_not(is_ctx))
    def _():
        b_ref[...] = b_lat_ref[...]

    def body(copies):
        def weight(k):
            if copies is not None:
                copies[k].wait()
            return w_ref[k]

        x = x_ref[...]
        h = _norm_mod(x, g_ref[1:2, :], _mod_chunk(mod_ref, row, 3), _mod_chunk(mod_ref, row, 4))
        gate_a = jax.nn.sigmoid(_dot(h, weight(0)))
        gate_b = jax.nn.sigmoid(_dot(h, weight(1)))
        merged = (gate_a * _dot(a_ref[...].astype(F32), weight(2))
                  + gate_b * _dot(b_ref[...].astype(F32), weight(3)))
        o_ref[...] = x + _mod_chunk(mod_ref, row, 5) * _dot(merged, weight(4))

    def first_step():
        copies = _merge_weight_copies(w_in_hbm, wa_hbm, wb_hbm, wo_hbm, w_ref, sems, l)
        for cp in copies:
            cp.start()
        body(copies)

    lax.cond(i == 0, first_step, lambda: body(None))


def _merge(x, a, b_lat, qkv, lam, subln_gain, mod, g_norm, w_in, w_branch_a, w_branch_b, w_out, l):
    n_ctx_tiles = N_CTX // ROW_TILE
    tile = pl.BlockSpec((ROW_TILE, D_MODEL), lambda i: (i, 0))
    lat_tile = pl.BlockSpec((ROW_TILE, D_MODEL), lambda i: (jnp.maximum(i - n_ctx_tiles, 0), 0))
    ctx_part = [pl.BlockSpec((ROW_TILE, D_MODEL),
                             functools.partial(lambda i, c: (jnp.minimum(i, n_ctx_tiles - 1), c), c=c))
                for c in range(3)]
    hbm = pl.BlockSpec(memory_space=pl.ANY)
    return pl.pallas_call(
        functools.partial(_merge_kernel, l=l),
        grid=(N_TOK // ROW_TILE,),
        in_specs=[tile, tile, lat_tile] + ctx_part + [
                  _resident((DEPTH, 4, HEAD_DIM), lambda i: (0, 0, 0)),
                  _resident((DEPTH, V_DIM), lambda i: (0, 0)),
                  _resident((None, COND_ROWS, N_MOD * D_MODEL), lambda i: (l, 0, 0)),
                  _resident((None, 3, D_MODEL), lambda i: (l, 0, 0)),
                  hbm, hbm, hbm, hbm],
        out_specs=tile,
        out_shape=jax.ShapeDtypeStruct((N_TOK, D_MODEL), F32),
        scratch_shapes=[pltpu.VMEM((ROW_TILE, D_MODEL), BF16), pltpu.VMEM((5, D_MODEL, D_MODEL), F32),
                        pltpu.SemaphoreType.DMA((5,))],
        input_output_aliases={0: 0},
        compiler_params=_params(),
        name="merge",
    )(x, a, b_lat, qkv, qkv, qkv, lam, subln_gain, mod, g_norm, w_in, w_branch_a, w_branch_b, w_out)


def kernel(x_prompt, x_sample, cache_k, cache_v, c, c_ctx, w_mod, b_mod, g_norm, ffn1_w13, ffn1_w2, w_in,
           sgu_gain, w_spatial, b_spatial, lam, subln_gain, w_branch_a, w_branch_b, w_out, ffn2_w13,
           ffn2_w2, g_final):
    cond = jnp.concatenate(
        [c_ctx[None, :], c, jnp.zeros((COND_ROWS - 1 - DEC_BATCH, D_MODEL), F32)], axis=0)
    mod = _modulation(cond, w_mod, b_mod)
    bias_tile = jnp.repeat(jnp.swapaxes(b_spatial, 1, 2), GROUP_W, axis=2)
    rope = _rope_tables()

    x = (x_prompt.reshape(N_CTX, D_MODEL), x_sample.reshape(N_LAT, D_MODEL))
    caches = None
    for l in range(DEPTH):
        x = _ffn(x, mod, g_norm, ffn1_w13, ffn1_w2, None, l, 0)
        a, qkv, *caches = _proj(x, mod, g_norm, w_in, sgu_gain, w_spatial, bias_tile, rope, caches, l)
        b_lat = _attn_lat(qkv, cache_k, cache_v, lam, subln_gain, l)
        x = _merge(x, a, b_lat, qkv, lam, subln_gain, mod, g_norm, w_in, w_branch_a, w_branch_b, w_out, l)
        x = _ffn(x, mod, g_norm, ffn2_w13, ffn2_w2, g_final if l == DEPTH - 1 else None, l, 2)

    y_ctx, y_lat = x
    new_k, new_v = caches
    return y_ctx.reshape(BATCH, SEQ, D_MODEL), y_lat.reshape(DEC_BATCH, DEC_SEQ, D_MODEL), new_k, new_v
```

```python
import functools
import math

import numpy as np
import jax
import jax.numpy as jnp
from jax import lax
from jax.experimental import pallas as pl
from jax.experimental.pallas import tpu as pltpu

D_MODEL = 1024
BATCH = 16
SEQ = 256
DEPTH = 4
DEC_BATCH = 2
DEC_SEQ = 1024
PAST_LEN = 256
GRID_W = 64
CHUNK = 128
N_GROUPS_A = 4
GROUP_W = D_MODEL // N_GROUPS_A
N_HEADS = 8
HEAD_DIM = D_MODEL // (2 * N_HEADS)
V_DIM = 2 * HEAD_DIM
D_FF = ((8 * D_MODEL // 3 + 127) // 128) * 128
IN_W = 7 * D_MODEL
N_MOD = 9
ROPE_THETA = 10000.0
EPS = 1e-6

N_CTX = BATCH * SEQ
N_LAT = DEC_BATCH * DEC_SEQ
N_TOK = N_CTX + N_LAT
COND_ROWS = 8

ROW_TILE = 512
SUB_TILE = 256
FFN_TILE = 512
FF_CHUNK = 256
MOD_TILE = 2304
MOD_STREAMS = 2
Q_TILE = 512
V7X_VMEM_BYTES = 64 * 1024 * 1024
VMEM_LIMIT_BYTES = V7X_VMEM_BYTES - 4 * 1024 * 1024

F32 = jnp.float32
BF16 = jnp.bfloat16


def _params():
    return pltpu.CompilerParams(vmem_limit_bytes=VMEM_LIMIT_BYTES)


def _resident(block_shape, index_map):
    return pl.BlockSpec(block_shape, index_map, pipeline_mode=pl.Buffered(1))


def _dot(a, b):
    return jnp.dot(a, b, preferred_element_type=F32)


def _dot_nt(a, b):
    return lax.dot_general(a, b, (((1,), (1,)), ((), ())), preferred_element_type=F32)


def _rms(x):
    return x * lax.rsqrt(jnp.mean(x * x, axis=-1, keepdims=True) + EPS)


def _norm_mod(x, g, shift, scale):
    return _rms(x) * g * (1.0 + scale) + shift


def _stacked_mod_row(i, tile=ROW_TILE):
    n_ctx_tiles = N_CTX // tile
    return jnp.where(i < n_ctx_tiles, 0, 1 + (i - n_ctx_tiles) // (DEC_SEQ // tile))


def _mod_chunk(mod_ref, row, k):
    return mod_ref[pl.ds(row, 1), k * D_MODEL:(k + 1) * D_MODEL]


def _mod_kernel(c_ref, *refs):
    w_refs, (b_ref, o_ref) = refs[:MOD_STREAMS], refs[MOD_STREAMS:]
    c = c_ref[...]
    act = c * jax.nn.sigmoid(c)
    width = MOD_TILE // MOD_STREAMS
    for k, w_ref in enumerate(w_refs):
        cols = slice(k * width, (k + 1) * width)
        o_ref[:, cols] = _dot(act, w_ref[...]) + b_ref[:, cols]


def _modulation(cond, w_mod, b_mod):
    n = N_MOD * D_MODEL
    slabs = [pl.BlockSpec((None, D_MODEL, MOD_TILE // MOD_STREAMS),
                          functools.partial(lambda l, j, k: (l, 0, MOD_STREAMS * j + k), k=k))
             for k in range(MOD_STREAMS)]
    return pl.pallas_call(
        _mod_kernel,
        grid=(DEPTH, n // MOD_TILE),
        in_specs=[pl.BlockSpec((COND_ROWS, D_MODEL), lambda l, j: (0, 0))] + slabs + [
            pl.BlockSpec((None, 1, MOD_TILE), lambda l, j: (l, 0, j))],
        out_specs=pl.BlockSpec((None, COND_ROWS, MOD_TILE), lambda l, j: (l, 0, j)),
        out_shape=jax.ShapeDtypeStruct((DEPTH, COND_ROWS, n), F32),
        compiler_params=_params(),
        name="modulation",
    )(cond, *[w_mod] * MOD_STREAMS, b_mod.reshape(DEPTH, 1, n))


def _ffn_weight_copies(w13_hbm, w2_hbm, w13_ref, w2_ref, sems, l, c):
    lo = pl.multiple_of(c * FF_CHUNK, FF_CHUNK)
    cols, up_cols = pl.ds(lo, FF_CHUNK), pl.ds(D_FF + lo, FF_CHUNK)
    return (pltpu.make_async_copy(w13_hbm.at[l, :, cols], w13_ref.at[:, cols], sems.at[c, 0]),
            pltpu.make_async_copy(w13_hbm.at[l, :, up_cols], w13_ref.at[:, up_cols], sems.at[c, 1]),
            pltpu.make_async_copy(w2_hbm.at[l, cols, :], w2_ref.at[cols, :], sems.at[c, 2]))


def _ffn_kernel(*refs, l, sub, split_in, final):
    n_x = 2 if split_in else 1
    x_refs, (mod_ref, g_ref, w13_hbm, w2_hbm) = refs[:n_x], refs[n_x:n_x + 4]
    *rest, w13_ref, w2_ref, acc_ref, sems = refs[n_x + 4:]
    i = pl.program_id(0)
    is_ctx = i < N_CTX // FFN_TILE
    row = _stacked_mod_row(i, FFN_TILE)
    n_chunks = D_FF // FF_CHUNK

    def normed():
        x = jnp.where(is_ctx, x_refs[0][...], x_refs[1][...]) if split_in else x_refs[0][...]
        return x, _norm_mod(x, g_ref[sub:sub + 1, :], _mod_chunk(mod_ref, row, 3 * sub),
                            _mod_chunk(mod_ref, row, 3 * sub + 1))

    def swiglu(gate, up, w2):
        return _dot(gate * jax.nn.sigmoid(gate) * up, w2)

    def finish(x, acc):
        y = x + 0.5 * _mod_chunk(mod_ref, row, 3 * sub + 2) * acc
        if not final:
            rest[0][...] = y
            return
        gf_ref, ctx_ref, lat_ref = rest
        y = _rms(y) * gf_ref[...]

        @pl.when(is_ctx)
        def _():
            ctx_ref[...] = y

        @pl.when(jnp.logical_not(is_ctx))
        def _():
            lat_ref[...] = y

    def first_step():
        def start(c, _):
            for cp in _ffn_weight_copies(w13_hbm, w2_hbm, w13_ref, w2_ref, sems, l, c):
                cp.start()
            return 0

        lax.fori_loop(0, n_chunks, start, 0)
        x, h = normed()
        acc_ref[...] = jnp.zeros_like(acc_ref)

        def chunk(c, _):
            for cp in _ffn_weight_copies(w13_hbm, w2_hbm, w13_ref, w2_ref, sems, l, c):
                cp.wait()
            lo = pl.multiple_of(c * FF_CHUNK, FF_CHUNK)
            gate = _dot(h, w13_ref[:, pl.ds(lo, FF_CHUNK)])
            up = _dot(h, w13_ref[:, pl.ds(D_FF + lo, FF_CHUNK)])
            acc_ref[...] += swiglu(gate, up, w2_ref[pl.ds(lo, FF_CHUNK), :])
            return 0

        lax.fori_loop(0, n_chunks, chunk, 0)
        finish(x, acc_ref[...])

    def later_step():
        x, h = normed()

        def gate_up(lo):
            return (_dot(h, w13_ref[:, lo:lo + FF_CHUNK]),
                    _dot(h, w13_ref[:, D_FF + lo:D_FF + lo + FF_CHUNK]))

        acc = jnp.zeros((FFN_TILE, D_MODEL), F32)
        nxt = gate_up(0)
        for lo in range(0, D_FF, FF_CHUNK):
            gate, up = nxt
            if lo + FF_CHUNK < D_FF:
                nxt = gate_up(lo + FF_CHUNK)
            acc = acc + swiglu(gate, up, w2_ref[lo:lo + FF_CHUNK, :])
        finish(x, acc)

    lax.cond(i == 0, first_step, later_step)


def _ffn(xs, mod, g_norm, w13, w2, g_final, l, sub):
    split_in, final = isinstance(xs, tuple), g_final is not None
    n_ctx_tiles = N_CTX // FFN_TILE
    tile = pl.BlockSpec((FFN_TILE, D_MODEL), lambda i: (i, 0))
    ctx_tile = pl.BlockSpec((FFN_TILE, D_MODEL), lambda i: (jnp.minimum(i, n_ctx_tiles - 1), 0))
    lat_tile = pl.BlockSpec((FFN_TILE, D_MODEL), lambda i: (jnp.maximum(i - n_ctx_tiles, 0), 0))
    in_specs = ([ctx_tile, lat_tile] if split_in else [tile]) + [
        _resident((None, COND_ROWS, N_MOD * D_MODEL), lambda i: (l, 0, 0)),
        _resident((None, 3, D_MODEL), lambda i: (l, 0, 0)),
        pl.BlockSpec(memory_space=pl.ANY),
        pl.BlockSpec(memory_space=pl.ANY),
    ]
    args = (list(xs) if split_in else [xs]) + [mod, g_norm, w13, w2]
    if final:
        in_specs.append(_resident((1, D_MODEL), lambda i: (0, 0)))
        args.append(g_final.reshape(1, D_MODEL))
        out_specs = [ctx_tile, lat_tile]
        out_shape = [jax.ShapeDtypeStruct((N_CTX, D_MODEL), F32), jax.ShapeDtypeStruct((N_LAT, D_MODEL), F32)]
    else:
        out_specs, out_shape = tile, jax.ShapeDtypeStruct((N_TOK, D_MODEL), F32)
    return pl.pallas_call(
        functools.partial(_ffn_kernel, l=l, sub=sub, split_in=split_in, final=final),
        grid=(N_TOK // FFN_TILE,),
        in_specs=in_specs,
        out_specs=out_specs,
        out_shape=out_shape,
        scratch_shapes=[pltpu.VMEM((D_MODEL, 2 * D_FF), F32), pltpu.VMEM((D_FF, D_MODEL), F32),
                        pltpu.VMEM((FFN_TILE, D_MODEL), F32),
                        pltpu.SemaphoreType.DMA((D_FF // FF_CHUNK, 3))],
        input_output_aliases={} if split_in or final else {0: 0},
        compiler_params=_params(),
        name="ffn",
    )(*args)


def _gelu_tanh(x):
    return x * (0.5 * (1.0 + jnp.tanh(math.sqrt(2.0 / math.pi) * (x + 0.044715 * (x * x * x)))))


def _rope_tables():
    pos = np.arange(DEC_SEQ)
    lane = np.arange(V_DIM)
    half = HEAD_DIM // 2
    within = lane % half
    coord = np.where((lane % HEAD_DIM) < half, pos[:, None] // GRID_W, pos[:, None] % GRID_W)
    freqs = ROPE_THETA ** (-np.arange(0, half, 2, dtype=np.float64) / half)
    ang = coord * freqs[within % (half // 2)][None, :]
    first = (within < half // 2)[None, :]
    identity = np.zeros((ROW_TILE, V_DIM))
    cos = np.concatenate([np.cos(ang), identity + 1.0])
    sin_lo = np.concatenate([np.where(first, -np.sin(ang), 0.0), identity])
    sin_hi = np.concatenate([np.where(first, 0.0, np.sin(ang)), identity])
    return tuple(jnp.asarray(t, dtype=F32) for t in (cos, sin_lo, sin_hi))


def _proj_kernel(x_ref, mod_ref, g_ref, wz_ref, wq_ref, wk_ref, wv_ref, gain_ref, ws_ref, bias_ref,
                 cos_ref, lo_ref, hi_ref, *rest, l):
    a_ref, qkv_ref, kc_ref, vc_ref = rest[-4:]
    i = pl.program_id(0)
    row = _stacked_mod_row(i)

    def project(r0):
        h = _norm_mod(x_ref[r0:r0 + SUB_TILE, :], g_ref[1:2, :], _mod_chunk(mod_ref, row, 3),
                      _mod_chunk(mod_ref, row, 4))
        z = _gelu_tanh(_dot(h, wz_ref[...]))
        return z, _dot(h, wq_ref[...]) * (HEAD_DIM ** -0.5), _dot(h, wk_ref[...]), _dot(h, wv_ref[...])

    def gate(r0, z):
        u = z[:, :D_MODEL]
        v = _rms(z[:, D_MODEL:]) * gain_ref[l:l + 1, :]
        for c in range(0, SUB_TILE, CHUNK):
            for g in range(N_GROUPS_A):
                c0 = g * GROUP_W
                mixed = _dot(ws_ref[l, g], v[c:c + CHUNK, c0:c0 + GROUP_W]) + bias_ref[:, c0:c0 + GROUP_W]
                a_ref[r0 + c:r0 + c + CHUNK, c0:c0 + GROUP_W] = (
                    u[c:c + CHUNK, c0:c0 + GROUP_W] * mixed).astype(BF16)

    def rotate(r0, q, k):
        rows = slice(r0, r0 + SUB_TILE)
        cos, sin_lo, sin_hi = cos_ref[rows, :], lo_ref[rows, :], hi_ref[rows, :]
        for c0 in range(0, D_MODEL, V_DIM):
            for src, base in ((q, 0), (k, D_MODEL)):
                t = src[:, c0:c0 + V_DIM]
                t = (t * cos + pltpu.roll(t, V_DIM - HEAD_DIM // 4, 1) * sin_lo
                     + pltpu.roll(t, HEAD_DIM // 4, 1) * sin_hi)
                qkv_ref[rows, base + c0:base + c0 + V_DIM] = t.astype(BF16)

    starts = list(range(0, ROW_TILE, SUB_TILE))
    kv = []
    nxt = project(starts[0])
    for n, r0 in enumerate(starts):
        z, q, k, v = nxt
        if n + 1 < len(starts):
            nxt = project(starts[n + 1])
        gate(r0, z)
        rotate(r0, q, k)
        qkv_ref[r0:r0 + SUB_TILE, 2 * D_MODEL:] = v.astype(BF16)
        kv.append((k, v))

    @pl.when(i < N_CTX // ROW_TILE)
    def _():
        for n, (k, v) in enumerate(kv):
            kc_ref[n] = k.reshape(SEQ, N_HEADS, V_DIM)
            vc_ref[n] = v.reshape(SEQ, N_HEADS, V_DIM)


def _proj(x, mod, g_norm, w_in, sgu_gain, w_spatial, bias_tile, rope, caches, l):
    assert SUB_TILE == SEQ
    n_ctx_tiles, lat_tiles = N_CTX // ROW_TILE, DEC_SEQ // ROW_TILE
    tile = lambda w: pl.BlockSpec((ROW_TILE, w), lambda i: (i, 0))
    w_block = lambda c: _resident((None, D_MODEL, D_MODEL), lambda i: (l, 0, c))
    table = pl.BlockSpec((ROW_TILE, V_DIM),
                         lambda i: (jnp.where(i < n_ctx_tiles, lat_tiles, (i - n_ctx_tiles) % lat_tiles), 0))
    cache_block = pl.BlockSpec((ROW_TILE // SEQ, None, SEQ, N_HEADS, V_DIM),
                               lambda i: (jnp.minimum(i, n_ctx_tiles - 1), l, 0, 0, 0))
    cache = jax.ShapeDtypeStruct((BATCH, DEPTH, SEQ, N_HEADS, V_DIM), F32)
    in_specs = [
        tile(D_MODEL),
        _resident((None, COND_ROWS, N_MOD * D_MODEL), lambda i: (l, 0, 0)),
        _resident((None, 3, D_MODEL), lambda i: (l, 0, 0)),
        _resident((None, D_MODEL, 2 * D_MODEL), lambda i: (l, 0, 0)),
        w_block(2), w_block(3), w_block(4),
        _resident((DEPTH, D_MODEL), lambda i: (0, 0)),
        _resident((DEPTH, N_GROUPS_A, CHUNK, CHUNK), lambda i: (0, 0, 0, 0)),
        _resident((None, CHUNK, D_MODEL), lambda i: (l, 0, 0)),
        table, table, table,
    ]
    args = [x, mod, g_norm, w_in, w_in, w_in, w_in, sgu_gain, w_spatial, bias_tile, *rope]
    aliases = {}
    if caches is not None:
        in_specs += [pl.BlockSpec(memory_space=pl.ANY)] * 2
        aliases = {len(args): 2, len(args) + 1: 3}
        args += list(caches)
    return pl.pallas_call(
        functools.partial(_proj_kernel, l=l),
        grid=(N_TOK // ROW_TILE,),
        in_specs=in_specs,
        out_specs=[tile(D_MODEL), tile(3 * D_MODEL), cache_block, cache_block],
        out_shape=[jax.ShapeDtypeStruct((N_TOK, D_MODEL), BF16),
                   jax.ShapeDtypeStruct((N_TOK, 3 * D_MODEL), BF16), cache, cache],
        input_output_aliases=aliases,
        compiler_params=_params(),
        name="proj",
    )(*args)


def _lam_full(lam_ref, l):
    lp = lam_ref[l]
    s01 = jnp.sum(lp[0:1, :] * lp[1:2, :], axis=-1, keepdims=True)
    s23 = jnp.sum(lp[2:3, :] * lp[3:4, :], axis=-1, keepdims=True)
    return jnp.exp(s01) - jnp.exp(s23) + _lam_init(l)


def _lam_init(l):
    return 0.8 - 0.6 * math.exp(-0.3 * l)


def _attn_scores(q, ks):
    lane = lax.broadcasted_iota(jnp.int32, (1, V_DIM), 1)
    lo = (lane < HEAD_DIM).astype(BF16)
    qs = jnp.concatenate([q * lo, q * (1 - lo)], axis=0)
    return [_dot_nt(qs, k) for k in ks]


def _attn_combine(s, vs, lam_full, gain, l):
    n_q = s[0].shape[0] // 2
    m = functools.reduce(jnp.maximum, [jnp.max(t, axis=-1, keepdims=True) for t in s])
    pv = functools.reduce(jnp.add, [
        _dot(jnp.exp(t - m).astype(BF16), jnp.concatenate([v, jnp.ones_like(v)], axis=1))
        for t, v in zip(s, vs)])
    pv = pv[:, :V_DIM] / pv[:, V_DIM:]
    o = pv[:n_q] - lam_full * pv[n_q:]
    return _rms(o) * gain * (1.0 - _lam_init(l))


def _attn_heads(q_ref, k_refs, v_refs, lam_ref, gain_ref, o_ref, l, n_seqs=1):
    lam_full = _lam_full(lam_ref, l)
    gain = gain_ref[l:l + 1, :]
    units = [(sq, hd) for hd in range(N_HEADS) for sq in range(n_seqs)]

    def part(ref, sq, hd):
        n = ref.shape[0] // n_seqs
        return ref[sq * n:(sq + 1) * n, hd * V_DIM:(hd + 1) * V_DIM].astype(BF16)

    def scores(sq, hd):
        return _attn_scores(part(q_ref, sq, hd), [part(r, sq, hd) for r in k_refs])

    nxt = scores(*units[0])
    for n, (sq, hd) in enumerate(units):
        s = nxt
        if n + 1 < len(units):
            nxt = scores(*units[n + 1])
        o = _attn_combine(s, [part(r, sq, hd) for r in v_refs], lam_full, gain, l)
        rows = q_ref.shape[0] // n_seqs
        o_ref[sq * rows:(sq + 1) * rows, hd * V_DIM:(hd + 1) * V_DIM] = o.astype(BF16)


def _attn_lat_kernel(q_ref, k_ref, v_ref, past_k, past_v, lam_ref, gain_ref, o_ref, pk_ref, pv_ref, *, l):
    pk_ref[...] = past_k[...].reshape(PAST_LEN, D_MODEL).astype(BF16)
    pv_ref[...] = past_v[...].reshape(PAST_LEN, D_MODEL).astype(BF16)
    _attn_heads(q_ref, [pk_ref, k_ref], [pv_ref, v_ref], lam_ref, gain_ref, o_ref, l)


def _attn_lat(qkv, past_k, past_v, lam, subln_gain, l):
    tiles = DEC_SEQ // Q_TILE
    qspec = pl.BlockSpec((Q_TILE, D_MODEL), lambda b, j: (N_CTX // Q_TILE + b * tiles + j, 0))
    kspec = pl.BlockSpec((DEC_SEQ, D_MODEL), lambda b, j: (N_CTX // DEC_SEQ + b, 1))
    vspec = pl.BlockSpec((DEC_SEQ, D_MODEL), lambda b, j: (N_CTX // DEC_SEQ + b, 2))
    past = pl.BlockSpec((None, None, PAST_LEN, N_HEADS, V_DIM), lambda b, j: (b, l, 0, 0, 0))
    return pl.pallas_call(
        functools.partial(_attn_lat_kernel, l=l),
        grid=(DEC_BATCH, tiles),
        in_specs=[qspec, kspec, vspec, past, past,
                  _resident((DEPTH, 4, HEAD_DIM), lambda b, j: (0, 0, 0)),
                  _resident((DEPTH, V_DIM), lambda b, j: (0, 0))],
        out_specs=pl.BlockSpec((Q_TILE, D_MODEL), lambda b, j: (b * tiles + j, 0)),
        out_shape=jax.ShapeDtypeStruct((N_LAT, D_MODEL), BF16),
        scratch_shapes=[pltpu.VMEM((PAST_LEN, D_MODEL), BF16)] * 2,
        compiler_params=_params(),
        name="attn_latent",
    )(qkv, qkv, qkv, past_k, past_v, lam, subln_gain)


def _merge_weight_copies(w_in_hbm, wa_hbm, wb_hbm, wo_hbm, w_ref, sems, l):
    sources = [w_in_hbm.at[l, :, pl.ds(5 * D_MODEL, D_MODEL)], w_in_hbm.at[l, :, pl.ds(6 * D_MODEL, D_MODEL)],
               wa_hbm.at[l], wb_hbm.at[l], wo_hbm.at[l]]
    return [pltpu.make_async_copy(src, w_ref.at[k], sems.at[k]) for k, src in enumerate(sources)]


def _merge_kernel(x_ref, a_ref, b_lat_ref, q_ref, k_ref, v_ref, lam_ref, subln_ref, mod_ref, g_ref,
                  w_in_hbm, wa_hbm, wb_hbm, wo_hbm, o_ref, b_ref, w_ref, sems, *, l):
    i = pl.program_id(0)
    row = _stacked_mod_row(i)
    is_ctx = i < N_CTX // ROW_TILE

    @pl.when(is_ctx)
    def _():
        _attn_heads(q_ref, [k_ref], [v_ref], lam_ref, subln_ref, b_ref, l, n_seqs=ROW_TILE // SEQ)

    @pl.when(jnp.logical_not(is_ctx))
    def _():
        b_ref[...] = b_lat_ref[...]

    def body(copies):
        def weight(k):
            if copies is not None:
                copies[k].wait()
            return w_ref[k]

        x = x_ref[...]
        h = _norm_mod(x, g_ref[1:2, :], _mod_chunk(mod_ref, row, 3), _mod_chunk(mod_ref, row, 4))
        gate_a = jax.nn.sigmoid(_dot(h, weight(0)))
        gate_b = jax.nn.sigmoid(_dot(h, weight(1)))
        merged = (gate_a * _dot(a_ref[...].astype(F32), weight(2))
                  + gate_b * _dot(b_ref[...].astype(F32), weight(3)))
        o_ref[...] = x + _mod_chunk(mod_ref, row, 5) * _dot(merged, weight(4))

    def first_step():
        copies = _merge_weight_copies(w_in_hbm, wa_hbm, wb_hbm, wo_hbm, w_ref, sems, l)
        for cp in copies:
            cp.start()
        body(copies)

    lax.cond(i == 0, first_step, lambda: body(None))


def _merge(x, a, b_lat, qkv, lam, subln_gain, mod, g_norm, w_in, w_branch_a, w_branch_b, w_out, l):
    n_ctx_tiles = N_CTX // ROW_TILE
    tile = pl.BlockSpec((ROW_TILE, D_MODEL), lambda i: (i, 0))
    lat_tile = pl.BlockSpec((ROW_TILE, D_MODEL), lambda i: (jnp.maximum(i - n_ctx_tiles, 0), 0))
    ctx_part = [pl.BlockSpec((ROW_TILE, D_MODEL),
                             functools.partial(lambda i, c: (jnp.minimum(i, n_ctx_tiles - 1), c), c=c))
                for c in range(3)]
    hbm = pl.BlockSpec(memory_space=pl.ANY)
    return pl.pallas_call(
        functools.partial(_merge_kernel, l=l),
        grid=(N_TOK // ROW_TILE,),
        in_specs=[tile, tile, lat_tile] + ctx_part + [
                  _resident((DEPTH, 4, HEAD_DIM), lambda i: (0, 0, 0)),
                  _resident((DEPTH, V_DIM), lambda i: (0, 0)),
                  _resident((None, COND_ROWS, N_MOD * D_MODEL), lambda i: (l, 0, 0)),
                  _resident((None, 3, D_MODEL), lambda i: (l, 0, 0)),
                  hbm, hbm, hbm, hbm],
        out_specs=tile,
        out_shape=jax.ShapeDtypeStruct((N_TOK, D_MODEL), F32),
        scratch_shapes=[pltpu.VMEM((ROW_TILE, D_MODEL), BF16), pltpu.VMEM((5, D_MODEL, D_MODEL), F32),
                        pltpu.SemaphoreType.DMA((5,))],
        input_output_aliases={0: 0},
        compiler_params=_params(),
        name="merge",
    )(x, a, b_lat, qkv, qkv, qkv, lam, subln_gain, mod, g_norm, w_in, w_branch_a, w_branch_b, w_out)


def kernel(x_prompt, x_sample, cache_k, cache_v, c, c_ctx, w_mod, b_mod, g_norm, ffn1_w13, ffn1_w2, w_in,
           sgu_gain, w_spatial, b_spatial, lam, subln_gain, w_branch_a, w_branch_b, w_out, ffn2_w13,
           ffn2_w2, g_final):
    cond = jnp.concatenate(
        [c_ctx[None, :], c, jnp.zeros((COND_ROWS - 1 - DEC_BATCH, D_MODEL), F32)], axis=0)
    mod = _modulation(cond, w_mod, b_mod)
    bias_tile = jnp.repeat(jnp.swapaxes(b_spatial, 1, 2), GROUP_W, axis=2)
    rope = _rope_tables()

    x = (x_prompt.reshape(N_CTX, D_MODEL), x_sample.reshape(N_LAT, D_MODEL))
    caches = None
    for l in range(DEPTH):
        x = _ffn(x, mod, g_norm, ffn1_w13, ffn1_w2, None, l, 0)
        a, qkv, *caches = _proj(x, mod, g_norm, w_in, sgu_gain, w_spatial, bias_tile, rope, caches, l)
        b_lat = _attn_lat(qkv, cache_k, cache_v, lam, subln_gain, l)
        x = _merge(x, a, b_lat, qkv, lam, subln_gain, mod, g_norm, w_in, w_branch_a, w_branch_b, w_out, l)
        x = _ffn(x, mod, g_norm, ffn2_w13, ffn2_w2, g_final if l == DEPTH - 1 else None, l, 2)

    y_ctx, y_lat = x
    new_k, new_v = caches
    return y_ctx.reshape(BATCH, SEQ, D_MODEL), y_lat.reshape(DEC_BATCH, DEC_SEQ, D_MODEL), new_k, new_v
```

```python
import functools
import math

import numpy as np
import jax
import jax.numpy as jnp
from jax import lax
from jax.experimental import pallas as pl
from jax.experimental.pallas import tpu as pltpu

D_MODEL = 1024
BATCH = 16
SEQ = 256
DEPTH = 4
DEC_BATCH = 2
DEC_SEQ = 1024
PAST_LEN = 256
GRID_W = 64
CHUNK = 128
N_GROUPS_A = 4
GROUP_W = D_MODEL // N_GROUPS_A
N_HEADS = 8
HEAD_DIM = D_MODEL // (2 * N_HEADS)
V_DIM = 2 * HEAD_DIM
D_FF = ((8 * D_MODEL // 3 + 127) // 128) * 128
IN_W = 7 * D_MODEL
N_MOD = 9
ROPE_THETA = 10000.0
EPS = 1e-6

N_CTX = BATCH * SEQ
N_LAT = DEC_BATCH * DEC_SEQ
N_TOK = N_CTX + N_LAT
COND_ROWS = 8

ROW_TILE = 512
SUB_TILE = 256
FFN_TILE = 512
FF_CHUNK = 256
MOD_TILE = 2304
Q_TILE = 512
V7X_VMEM_BYTES = 64 * 1024 * 1024
VMEM_LIMIT_BYTES = V7X_VMEM_BYTES - 4 * 1024 * 1024

F32 = jnp.float32
BF16 = jnp.bfloat16


def _params():
    return pltpu.CompilerParams(vmem_limit_bytes=VMEM_LIMIT_BYTES)


def _resident(block_shape, index_map):
    return pl.BlockSpec(block_shape, index_map, pipeline_mode=pl.Buffered(1))


def _dot(a, b):
    return jnp.dot(a, b, preferred_element_type=F32)


def _dot_nt(a, b):
    return lax.dot_general(a, b, (((1,), (1,)), ((), ())), preferred_element_type=F32)


def _rms(x):
    return x * lax.rsqrt(jnp.mean(x * x, axis=-1, keepdims=True) + EPS)


def _norm_mod(x, g, shift, scale):
    return _rms(x) * g * (1.0 + scale) + shift


def _stacked_mod_row(i, tile=ROW_TILE):
    n_ctx_tiles = N_CTX // tile
    return jnp.where(i < n_ctx_tiles, 0, 1 + (i - n_ctx_tiles) // (DEC_SEQ // tile))


def _mod_chunk(mod_ref, row, k):
    return mod_ref[pl.ds(row, 1), k * D_MODEL:(k + 1) * D_MODEL]


def _mod_kernel(c_ref, w_ref, b_ref, o_ref):
    c = c_ref[...]
    o_ref[...] = _dot(c * jax.nn.sigmoid(c), w_ref[...]) + b_ref[...]


def _modulation(cond, w_mod, b_mod):
    n = N_MOD * D_MODEL
    return pl.pallas_call(
        _mod_kernel,
        grid=(DEPTH, n // MOD_TILE),
        in_specs=[
            pl.BlockSpec((COND_ROWS, D_MODEL), lambda l, j: (0, 0)),
            pl.BlockSpec((None, D_MODEL, MOD_TILE), lambda l, j: (l, 0, j)),
            pl.BlockSpec((None, 1, MOD_TILE), lambda l, j: (l, 0, j)),
        ],
        out_specs=pl.BlockSpec((None, COND_ROWS, MOD_TILE), lambda l, j: (l, 0, j)),
        out_shape=jax.ShapeDtypeStruct((DEPTH, COND_ROWS, n), F32),
        compiler_params=_params(),
        name="modulation",
    )(cond, w_mod, b_mod.reshape(DEPTH, 1, n))


def _ffn_weight_copies(w13_hbm, w2_hbm, w13_ref, w2_ref, sems, l, c):
    lo = pl.multiple_of(c * FF_CHUNK, FF_CHUNK)
    cols, up_cols = pl.ds(lo, FF_CHUNK), pl.ds(D_FF + lo, FF_CHUNK)
    return (pltpu.make_async_copy(w13_hbm.at[l, :, cols], w13_ref.at[:, cols], sems.at[c, 0]),
            pltpu.make_async_copy(w13_hbm.at[l, :, up_cols], w13_ref.at[:, up_cols], sems.at[c, 1]),
            pltpu.make_async_copy(w2_hbm.at[l, cols, :], w2_ref.at[cols, :], sems.at[c, 2]))


def _ffn_kernel(*refs, l, sub, split_in, final):
    n_x = 2 if split_in else 1
    x_refs, (mod_ref, g_ref, w13_hbm, w2_hbm) = refs[:n_x], refs[n_x:n_x + 4]
    *rest, w13_ref, w2_ref, acc_ref, sems = refs[n_x + 4:]
    i = pl.program_id(0)
    is_ctx = i < N_CTX // FFN_TILE
    row = _stacked_mod_row(i, FFN_TILE)
    n_chunks = D_FF // FF_CHUNK

    def normed():
        x = jnp.where(is_ctx, x_refs[0][...], x_refs[1][...]) if split_in else x_refs[0][...]
        return x, _norm_mod(x, g_ref[sub:sub + 1, :], _mod_chunk(mod_ref, row, 3 * sub),
                            _mod_chunk(mod_ref, row, 3 * sub + 1))

    def swiglu(gate, up, w2):
        return _dot(gate * jax.nn.sigmoid(gate) * up, w2)

    def finish(x, acc):
        y = x + 0.5 * _mod_chunk(mod_ref, row, 3 * sub + 2) * acc
        if not final:
            rest[0][...] = y
            return
        gf_ref, ctx_ref, lat_ref = rest
        y = _rms(y) * gf_ref[...]

        @pl.when(is_ctx)
        def _():
            ctx_ref[...] = y

        @pl.when(jnp.logical_not(is_ctx))
        def _():
            lat_ref[...] = y

    def first_step():
        def start(c, _):
            for cp in _ffn_weight_copies(w13_hbm, w2_hbm, w13_ref, w2_ref, sems, l, c):
                cp.start()
            return 0

        lax.fori_loop(0, n_chunks, start, 0)
        x, h = normed()
        acc_ref[...] = jnp.zeros_like(acc_ref)

        def chunk(c, _):
            for cp in _ffn_weight_copies(w13_hbm, w2_hbm, w13_ref, w2_ref, sems, l, c):
                cp.wait()
            lo = pl.multiple_of(c * FF_CHUNK, FF_CHUNK)
            gate = _dot(h, w13_ref[:, pl.ds(lo, FF_CHUNK)])
            up = _dot(h, w13_ref[:, pl.ds(D_FF + lo, FF_CHUNK)])
            acc_ref[...] += swiglu(gate, up, w2_ref[pl.ds(lo, FF_CHUNK), :])
            return 0

        lax.fori_loop(0, n_chunks, chunk, 0)
        finish(x, acc_ref[...])

    def later_step():
        x, h = normed()

        def gate_up(lo):
            return (_dot(h, w13_ref[:, lo:lo + FF_CHUNK]),
                    _dot(h, w13_ref[:, D_FF + lo:D_FF + lo + FF_CHUNK]))

        acc = jnp.zeros((FFN_TILE, D_MODEL), F32)
        nxt = gate_up(0)
        for lo in range(0, D_FF, FF_CHUNK):
            gate, up = nxt
            if lo + FF_CHUNK < D_FF:
                nxt = gate_up(lo + FF_CHUNK)
            acc = acc + swiglu(gate, up, w2_ref[lo:lo + FF_CHUNK, :])
        finish(x, acc)

    lax.cond(i == 0, first_step, later_step)


def _ffn_loop_kernel(x_hbm, mod_ref, g_ref, w13_hbm, w2_hbm, o_hbm, w13_ref, w2_ref, acc_ref, x_buf, o_buf,
                     w_sems, x_sems, o_sems, *, l, sub):
    n_tiles, n_chunks = N_TOK // FFN_TILE, D_FF // FF_CHUNK

    def x_copy(t, slot):
        return pltpu.make_async_copy(x_hbm.at[pl.ds(t * FFN_TILE, FFN_TILE), :], x_buf.at[slot], x_sems.at[slot])

    def o_copy(t, slot):
        return pltpu.make_async_copy(o_buf.at[slot], o_hbm.at[pl.ds(t * FFN_TILE, FFN_TILE), :], o_sems.at[slot])

    def normed(t, slot):
        row = _stacked_mod_row(t, FFN_TILE)
        x = x_buf[slot]
        h = _norm_mod(x, g_ref[l, sub:sub + 1, :], _mod_chunk(mod_ref.at[l], row, 3 * sub),
                      _mod_chunk(mod_ref.at[l], row, 3 * sub + 1))
        return x, h, _mod_chunk(mod_ref.at[l], row, 3 * sub + 2)

    def swiglu(gate, up, w2):
        return _dot(gate * jax.nn.sigmoid(gate) * up, w2)

    def start_weights(c, _):
        for cp in _ffn_weight_copies(w13_hbm, w2_hbm, w13_ref, w2_ref, w_sems, l, c):
            cp.start()
        return 0

    lax.fori_loop(0, n_chunks, start_weights, 0)
    x_copy(0, 0).start()
    x_copy(1, 1).start()

    x_copy(0, 0).wait()
    x, h, gate_mod = normed(0, 0)
    acc_ref[...] = jnp.zeros_like(acc_ref)

    def chunk(c, _):
        for cp in _ffn_weight_copies(w13_hbm, w2_hbm, w13_ref, w2_ref, w_sems, l, c):
            cp.wait()
        lo = pl.multiple_of(c * FF_CHUNK, FF_CHUNK)
        gate = _dot(h, w13_ref[:, pl.ds(lo, FF_CHUNK)])
        up = _dot(h, w13_ref[:, pl.ds(D_FF + lo, FF_CHUNK)])
        acc_ref[...] += swiglu(gate, up, w2_ref[pl.ds(lo, FF_CHUNK), :])
        return 0

    lax.fori_loop(0, n_chunks, chunk, 0)
    o_buf[0] = x + 0.5 * gate_mod * acc_ref[...]
    o_copy(0, 0).start()

    def tile(t, _):
        slot = t % 2
        x_copy(t, slot).wait()

        @pl.when(t + 1 < n_tiles)
        def _():
            x_copy(t + 1, 1 - slot).start()

        x, h, gate_mod = normed(t, slot)

        def gate_up(lo):
            return (_dot(h, w13_ref[:, lo:lo + FF_CHUNK]),
                    _dot(h, w13_ref[:, D_FF + lo:D_FF + lo + FF_CHUNK]))

        acc = jnp.zeros((FFN_TILE, D_MODEL), F32)
        nxt = gate_up(0)
        for lo in range(0, D_FF, FF_CHUNK):
            gate, up = nxt
            if lo + FF_CHUNK < D_FF:
                nxt = gate_up(lo + FF_CHUNK)
            acc = acc + swiglu(gate, up, w2_ref[lo:lo + FF_CHUNK, :])
        o_buf[slot] = x + 0.5 * gate_mod * acc
        o_copy(t, slot).start()
        o_copy(t - 1, 1 - slot).wait()
        return 0

    lax.fori_loop(1, n_tiles, tile, 0)
    o_copy(n_tiles - 1, (n_tiles - 1) % 2).wait()


def _ffn_loop(x, mod, g_norm, w13, w2, l, sub):
    hbm = pl.BlockSpec(memory_space=pl.ANY)
    vmem = pl.BlockSpec(memory_space=pltpu.VMEM)
    tile_buf = pltpu.VMEM((2, FFN_TILE, D_MODEL), F32)
    return pl.pallas_call(
        functools.partial(_ffn_loop_kernel, l=l, sub=sub),
        in_specs=[hbm, vmem, vmem, hbm, hbm],
        out_specs=hbm,
        out_shape=jax.ShapeDtypeStruct((N_TOK, D_MODEL), F32),
        scratch_shapes=[pltpu.VMEM((D_MODEL, 2 * D_FF), F32), pltpu.VMEM((D_FF, D_MODEL), F32),
                        pltpu.VMEM((FFN_TILE, D_MODEL), F32), tile_buf, tile_buf,
                        pltpu.SemaphoreType.DMA((D_FF // FF_CHUNK, 3)), pltpu.SemaphoreType.DMA((2,)),
                        pltpu.SemaphoreType.DMA((2,))],
        input_output_aliases={0: 0},
        compiler_params=_params(),
        name="ffn_loop",
    )(x, mod, g_norm, w13, w2)


def _ffn(xs, mod, g_norm, w13, w2, g_final, l, sub):
    split_in, final = isinstance(xs, tuple), g_final is not None
    n_ctx_tiles = N_CTX // FFN_TILE
    tile = pl.BlockSpec((FFN_TILE, D_MODEL), lambda i: (i, 0))
    ctx_tile = pl.BlockSpec((FFN_TILE, D_MODEL), lambda i: (jnp.minimum(i, n_ctx_tiles - 1), 0))
    lat_tile = pl.BlockSpec((FFN_TILE, D_MODEL), lambda i: (jnp.maximum(i - n_ctx_tiles, 0), 0))
    in_specs = ([ctx_tile, lat_tile] if split_in else [tile]) + [
        _resident((None, COND_ROWS, N_MOD * D_MODEL), lambda i: (l, 0, 0)),
        _resident((None, 3, D_MODEL), lambda i: (l, 0, 0)),
        pl.BlockSpec(memory_space=pl.ANY),
        pl.BlockSpec(memory_space=pl.ANY),
    ]
    args = (list(xs) if split_in else [xs]) + [mod, g_norm, w13, w2]
    if final:
        in_specs.append(_resident((1, D_MODEL), lambda i: (0, 0)))
        args.append(g_final.reshape(1, D_MODEL))
        out_specs = [ctx_tile, lat_tile]
        out_shape = [jax.ShapeDtypeStruct((N_CTX, D_MODEL), F32), jax.ShapeDtypeStruct((N_LAT, D_MODEL), F32)]
    else:
        out_specs, out_shape = tile, jax.ShapeDtypeStruct((N_TOK, D_MODEL), F32)
    return pl.pallas_call(
        functools.partial(_ffn_kernel, l=l, sub=sub, split_in=split_in, final=final),
        grid=(N_TOK // FFN_TILE,),
        in_specs=in_specs,
        out_specs=out_specs,
        out_shape=out_shape,
        scratch_shapes=[pltpu.VMEM((D_MODEL, 2 * D_FF), F32), pltpu.VMEM((D_FF, D_MODEL), F32),
                        pltpu.VMEM((FFN_TILE, D_MODEL), F32),
                        pltpu.SemaphoreType.DMA((D_FF // FF_CHUNK, 3))],
        input_output_aliases={} if split_in or final else {0: 0},
        compiler_params=_params(),
        name="ffn",
    )(*args)


def _gelu_tanh(x):
    return x * (0.5 * (1.0 + jnp.tanh(math.sqrt(2.0 / math.pi) * (x + 0.044715 * (x * x * x)))))


def _rope_tables():
    pos = np.arange(DEC_SEQ)
    lane = np.arange(V_DIM)
    half = HEAD_DIM // 2
    within = lane % half
    coord = np.where((lane % HEAD_DIM) < half, pos[:, None] // GRID_W, pos[:, None] % GRID_W)
    freqs = ROPE_THETA ** (-np.arange(0, half, 2, dtype=np.float64) / half)
    ang = coord * freqs[within % (half // 2)][None, :]
    first = (within < half // 2)[None, :]
    identity = np.zeros((ROW_TILE, V_DIM))
    cos = np.concatenate([np.cos(ang), identity + 1.0])
    sin_lo = np.concatenate([np.where(first, -np.sin(ang), 0.0), identity])
    sin_hi = np.concatenate([np.where(first, 0.0, np.sin(ang)), identity])
    return tuple(jnp.asarray(t, dtype=F32) for t in (cos, sin_lo, sin_hi))


def _proj_kernel(x_ref, mod_ref, g_ref, wz_ref, wq_ref, wk_ref, wv_ref, gain_ref, ws_ref, bias_ref,
                 cos_ref, lo_ref, hi_ref, *rest, l):
    a_ref, qkv_ref, kc_ref, vc_ref = rest[-4:]
    i = pl.program_id(0)
    row = _stacked_mod_row(i)

    def project(r0):
        h = _norm_mod(x_ref[r0:r0 + SUB_TILE, :], g_ref[1:2, :], _mod_chunk(mod_ref, row, 3),
                      _mod_chunk(mod_ref, row, 4))
        z = _gelu_tanh(_dot(h, wz_ref[...]))
        return z, _dot(h, wq_ref[...]) * (HEAD_DIM ** -0.5), _dot(h, wk_ref[...]), _dot(h, wv_ref[...])

    def gate(r0, z):
        u = z[:, :D_MODEL]
        v = _rms(z[:, D_MODEL:]) * gain_ref[l:l + 1, :]
        for c in range(0, SUB_TILE, CHUNK):
            for g in range(N_GROUPS_A):
                c0 = g * GROUP_W
                mixed = _dot(ws_ref[l, g], v[c:c + CHUNK, c0:c0 + GROUP_W]) + bias_ref[:, c0:c0 + GROUP_W]
                a_ref[r0 + c:r0 + c + CHUNK, c0:c0 + GROUP_W] = (
                    u[c:c + CHUNK, c0:c0 + GROUP_W] * mixed).astype(BF16)

    def rotate(r0, q, k):
        rows = slice(r0, r0 + SUB_TILE)
        cos, sin_lo, sin_hi = cos_ref[rows, :], lo_ref[rows, :], hi_ref[rows, :]
        for c0 in range(0, D_MODEL, V_DIM):
            for src, base in ((q, 0), (k, D_MODEL)):
                t = src[:, c0:c0 + V_DIM]
                t = (t * cos + pltpu.roll(t, V_DIM - HEAD_DIM // 4, 1) * sin_lo
                     + pltpu.roll(t, HEAD_DIM // 4, 1) * sin_hi)
                qkv_ref[rows, base + c0:base + c0 + V_DIM] = t.astype(BF16)

    starts = list(range(0, ROW_TILE, SUB_TILE))
    kv = []
    nxt = project(starts[0])
    for n, r0 in enumerate(starts):
        z, q, k, v = nxt
        if n + 1 < len(starts):
            nxt = project(starts[n + 1])
        gate(r0, z)
        rotate(r0, q, k)
        qkv_ref[r0:r0 + SUB_TILE, 2 * D_MODEL:] = v.astype(BF16)
        kv.append((k, v))

    @pl.when(i < N_CTX // ROW_TILE)
    def _():
        for n, (k, v) in enumerate(kv):
            kc_ref[n] = k.reshape(SEQ, N_HEADS, V_DIM)
            vc_ref[n] = v.reshape(SEQ, N_HEADS, V_DIM)


def _proj(x, mod, g_norm, w_in, sgu_gain, w_spatial, bias_tile, rope, caches, l):
    assert SUB_TILE == SEQ
    n_ctx_tiles, lat_tiles = N_CTX // ROW_TILE, DEC_SEQ // ROW_TILE
    tile = lambda w: pl.BlockSpec((ROW_TILE, w), lambda i: (i, 0))
    w_block = lambda c: _resident((None, D_MODEL, D_MODEL), lambda i: (l, 0, c))
    table = pl.BlockSpec((ROW_TILE, V_DIM),
                         lambda i: (jnp.where(i < n_ctx_tiles, lat_tiles, (i - n_ctx_tiles) % lat_tiles), 0))
    cache_block = pl.BlockSpec((ROW_TILE // SEQ, None, SEQ, N_HEADS, V_DIM),
                               lambda i: (jnp.minimum(i, n_ctx_tiles - 1), l, 0, 0, 0))
    cache = jax.ShapeDtypeStruct((BATCH, DEPTH, SEQ, N_HEADS, V_DIM), F32)
    in_specs = [
        tile(D_MODEL),
        _resident((None, COND_ROWS, N_MOD * D_MODEL), lambda i: (l, 0, 0)),
        _resident((None, 3, D_MODEL), lambda i: (l, 0, 0)),
        _resident((None, D_MODEL, 2 * D_MODEL), lambda i: (l, 0, 0)),
        w_block(2), w_block(3), w_block(4),
        _resident((DEPTH, D_MODEL), lambda i: (0, 0)),
        _resident((DEPTH, N_GROUPS_A, CHUNK, CHUNK), lambda i: (0, 0, 0, 0)),
        _resident((None, CHUNK, D_MODEL), lambda i: (l, 0, 0)),
        table, table, table,
    ]
    args = [x, mod, g_norm, w_in, w_in, w_in, w_in, sgu_gain, w_spatial, bias_tile, *rope]
    aliases = {}
    if caches is not None:
        in_specs += [pl.BlockSpec(memory_space=pl.ANY)] * 2
        aliases = {len(args): 2, len(args) + 1: 3}
        args += list(caches)
    return pl.pallas_call(
        functools.partial(_proj_kernel, l=l),
        grid=(N_TOK // ROW_TILE,),
        in_specs=in_specs,
        out_specs=[tile(D_MODEL), tile(3 * D_MODEL), cache_block, cache_block],
        out_shape=[jax.ShapeDtypeStruct((N_TOK, D_MODEL), BF16),
                   jax.ShapeDtypeStruct((N_TOK, 3 * D_MODEL), BF16), cache, cache],
        input_output_aliases=aliases,
        compiler_params=_params(),
        name="proj",
    )(*args)


def _lam_full(lam_ref, l):
    lp = lam_ref[l]
    s01 = jnp.sum(lp[0:1, :] * lp[1:2, :], axis=-1, keepdims=True)
    s23 = jnp.sum(lp[2:3, :] * lp[3:4, :], axis=-1, keepdims=True)
    return jnp.exp(s01) - jnp.exp(s23) + _lam_init(l)


def _lam_init(l):
    return 0.8 - 0.6 * math.exp(-0.3 * l)


def _attn_scores(q, ks):
    lane = lax.broadcasted_iota(jnp.int32, (1, V_DIM), 1)
    lo = (lane < HEAD_DIM).astype(BF16)
    qs = jnp.concatenate([q * lo, q * (1 - lo)], axis=0)
    return [_dot_nt(qs, k) for k in ks]


def _attn_combine(s, vs, lam_full, gain, l):
    n_q = s[0].shape[0] // 2
    m = functools.reduce(jnp.maximum, [jnp.max(t, axis=-1, keepdims=True) for t in s])
    pv = functools.reduce(jnp.add, [
        _dot(jnp.exp(t - m).astype(BF16), jnp.concatenate([v, jnp.ones_like(v)], axis=1))
        for t, v in zip(s, vs)])
    pv = pv[:, :V_DIM] / pv[:, V_DIM:]
    o = pv[:n_q] - lam_full * pv[n_q:]
    return _rms(o) * gain * (1.0 - _lam_init(l))


def _attn_heads(q_ref, k_refs, v_refs, lam_ref, gain_ref, o_ref, l, n_seqs=1):
    lam_full = _lam_full(lam_ref, l)
    gain = gain_ref[l:l + 1, :]
    units = [(sq, hd) for hd in range(N_HEADS) for sq in range(n_seqs)]

    def part(ref, sq, hd):
        n = ref.shape[0] // n_seqs
        return ref[sq * n:(sq + 1) * n, hd * V_DIM:(hd + 1) * V_DIM].astype(BF16)

    def scores(sq, hd):
        return _attn_scores(part(q_ref, sq, hd), [part(r, sq, hd) for r in k_refs])

    nxt = scores(*units[0])
    for n, (sq, hd) in enumerate(units):
        s = nxt
        if n + 1 < len(units):
            nxt = scores(*units[n + 1])
        o = _attn_combine(s, [part(r, sq, hd) for r in v_refs], lam_full, gain, l)
        rows = q_ref.shape[0] // n_seqs
        o_ref[sq * rows:(sq + 1) * rows, hd * V_DIM:(hd + 1) * V_DIM] = o.astype(BF16)


def _attn_lat_kernel(q_ref, k_ref, v_ref, past_k, past_v, lam_ref, gain_ref, o_ref, pk_ref, pv_ref, *, l):
    pk_ref[...] = past_k[...].reshape(PAST_LEN, D_MODEL).astype(BF16)
    pv_ref[...] = past_v[...].reshape(PAST_LEN, D_MODEL).astype(BF16)
    _attn_heads(q_ref, [pk_ref, k_ref], [pv_ref, v_ref], lam_ref, gain_ref, o_ref, l)


def _attn_lat(qkv, past_k, past_v, lam, subln_gain, l):
    tiles = DEC_SEQ // Q_TILE
    qspec = pl.BlockSpec((Q_TILE, D_MODEL), lambda b, j: (N_CTX // Q_TILE + b * tiles + j, 0))
    kspec = pl.BlockSpec((DEC_SEQ, D_MODEL), lambda b, j: (N_CTX // DEC_SEQ + b, 1))
    vspec = pl.BlockSpec((DEC_SEQ, D_MODEL), lambda b, j: (N_CTX // DEC_SEQ + b, 2))
    past = pl.BlockSpec((None, None, PAST_LEN, N_HEADS, V_DIM), lambda b, j: (b, l, 0, 0, 0))
    return pl.pallas_call(
        functools.partial(_attn_lat_kernel, l=l),
        grid=(DEC_BATCH, tiles),
        in_specs=[qspec, kspec, vspec, past, past,
                  _resident((DEPTH, 4, HEAD_DIM), lambda b, j: (0, 0, 0)),
                  _resident((DEPTH, V_DIM), lambda b, j: (0, 0))],
        out_specs=pl.BlockSpec((Q_TILE, D_MODEL), lambda b, j: (b * tiles + j, 0)),
        out_shape=jax.ShapeDtypeStruct((N_LAT, D_MODEL), BF16),
        scratch_shapes=[pltpu.VMEM((PAST_LEN, D_MODEL), BF16)] * 2,
        compiler_params=_params(),
        name="attn_latent",
    )(qkv, qkv, qkv, past_k, past_v, lam, subln_gain)


def _merge_weight_copies(w_in_hbm, wa_hbm, wb_hbm, wo_hbm, w_ref, sems, l):
    sources = [w_in_hbm.at[l, :, pl.ds(5 * D_MODEL, D_MODEL)], w_in_hbm.at[l, :, pl.ds(6 * D_MODEL, D_MODEL)],
               wa_hbm.at[l], wb_hbm.at[l], wo_hbm.at[l]]
    return [pltpu.make_async_copy(src, w_ref.at[k], sems.at[k]) for k, src in enumerate(sources)]


def _merge_kernel(x_ref, a_ref, b_lat_ref, q_ref, k_ref, v_ref, lam_ref, subln_ref, mod_ref, g_ref,
                  w_in_hbm, wa_hbm, wb_hbm, wo_hbm, o_ref, b_ref, w_ref, sems, *, l):
    i = pl.program_id(0)
    row = _stacked_mod_row(i)
    is_ctx = i < N_CTX // ROW_TILE

    @pl.when(is_ctx)
    def _():
        _attn_heads(q_ref, [k_ref], [v_ref], lam_ref, subln_ref, b_ref, l, n_seqs=ROW_TILE // SEQ)

    @pl.when(jnp.logical_not(is_ctx))
    def _():
        b_ref[...] = b_lat_ref[...]

    def body(copies):
        def weight(k):
            if copies is not None:
                copies[k].wait()
            return w_ref[k]

        x = x_ref[...]
        h = _norm_mod(x, g_ref[1:2, :], _mod_chunk(mod_ref, row, 3), _mod_chunk(mod_ref, row, 4))
        gate_a = jax.nn.sigmoid(_dot(h, weight(0)))
        gate_b = jax.nn.sigmoid(_dot(h, weight(1)))
        merged = (gate_a * _dot(a_ref[...].astype(F32), weight(2))
                  + gate_b * _dot(b_ref[...].astype(F32), weight(3)))
        o_ref[...] = x + _mod_chunk(mod_ref, row, 5) * _dot(merged, weight(4))

    def first_step():
        copies = _merge_weight_copies(w_in_hbm, wa_hbm, wb_hbm, wo_hbm, w_ref, sems, l)
        for cp in copies:
            cp.start()
        body(copies)

    lax.cond(i == 0, first_step, lambda: body(None))


def _merge(x, a, b_lat, qkv, lam, subln_gain, mod, g_norm, w_in, w_branch_a, w_branch_b, w_out, l):
    n_ctx_tiles = N_CTX // ROW_TILE
    tile = pl.BlockSpec((ROW_TILE, D_MODEL), lambda i: (i, 0))
    lat_tile = pl.BlockSpec((ROW_TILE, D_MODEL), lambda i: (jnp.maximum(i - n_ctx_tiles, 0), 0))
    ctx_part = [pl.BlockSpec((ROW_TILE, D_MODEL),
                             functools.partial(lambda i, c: (jnp.minimum(i, n_ctx_tiles - 1), c), c=c))
                for c in range(3)]
    hbm = pl.BlockSpec(memory_space=pl.ANY)
    return pl.pallas_call(
        functools.partial(_merge_kernel, l=l),
        grid=(N_TOK // ROW_TILE,),
        in_specs=[tile, tile, lat_tile] + ctx_part + [
                  _resident((DEPTH, 4, HEAD_DIM), lambda i: (0, 0, 0)),
                  _resident((DEPTH, V_DIM), lambda i: (0, 0)),
                  _resident((None, COND_ROWS, N_MOD * D_MODEL), lambda i: (l, 0, 0)),
                  _resident((None, 3, D_MODEL), lambda i: (l, 0, 0)),
                  hbm, hbm, hbm, hbm],
        out_specs=tile,
        out_shape=jax.ShapeDtypeStruct((N_TOK, D_MODEL), F32),
        scratch_shapes=[pltpu.VMEM((ROW_TILE, D_MODEL), BF16), pltpu.VMEM((5, D_MODEL, D_MODEL), F32),
                        pltpu.SemaphoreType.DMA((5,))],
        input_output_aliases={0: 0},
        compiler_params=_params(),
        name="merge",
    )(x, a, b_lat, qkv, qkv, qkv, lam, subln_gain, mod, g_norm, w_in, w_branch_a, w_branch_b, w_out)


def kernel(x_prompt, x_sample, cache_k, cache_v, c, c_ctx, w_mod, b_mod, g_norm, ffn1_w13, ffn1_w2, w_in,
           sgu_gain, w_spatial, b_spatial, lam, subln_gain, w_branch_a, w_branch_b, w_out, ffn2_w13,
           ffn2_w2, g_final):
    cond = jnp.concatenate(
        [c_ctx[None, :], c, jnp.zeros((COND_ROWS - 1 - DEC_BATCH, D_MODEL), F32)], axis=0)
    mod = _modulation(cond, w_mod, b_mod)
    bias_tile = jnp.repeat(jnp.swapaxes(b_spatial, 1, 2), GROUP_W, axis=2)
    rope = _rope_tables()

    x = (x_prompt.reshape(N_CTX, D_MODEL), x_sample.reshape(N_LAT, D_MODEL))
    caches = None
    for l in range(DEPTH):
        x = _ffn(x, mod, g_norm, ffn1_w13, ffn1_w2, None, l, 0) if l == 0 else _ffn_loop(
            x, mod, g_norm, ffn1_w13, ffn1_w2, l, 0)
        a, qkv, *caches = _proj(x, mod, g_norm, w_in, sgu_gain, w_spatial, bias_tile, rope, caches, l)
        b_lat = _attn_lat(qkv, cache_k, cache_v, lam, subln_gain, l)
        x = _merge(x, a, b_lat, qkv, lam, subln_gain, mod, g_norm, w_in, w_branch_a, w_branch_b, w_out, l)
        x = _ffn(x, mod, g_norm, ffn2_w13, ffn2_w2, g_final, l, 2) if l == DEPTH - 1 else _ffn_loop(
            x, mod, g_norm, ffn2_w13, ffn2_w2, l, 2)

    y_ctx, y_lat = x
    new_k, new_v = caches
    return y_ctx.reshape(BATCH, SEQ, D_MODEL), y_lat.reshape(DEC_BATCH, DEC_SEQ, D_MODEL), new_k, new_v
```

```python
import functools
import math

import numpy as np
import jax
import jax.numpy as jnp
from jax import lax
from jax.experimental import pallas as pl
from jax.experimental.pallas import tpu as pltpu

D_MODEL = 1024
BATCH = 16
SEQ = 256
DEPTH = 4
DEC_BATCH = 2
DEC_SEQ = 1024
PAST_LEN = 256
GRID_W = 64
CHUNK = 128
N_GROUPS_A = 4
GROUP_W = D_MODEL // N_GROUPS_A
N_HEADS = 8
HEAD_DIM = D_MODEL // (2 * N_HEADS)
V_DIM = 2 * HEAD_DIM
D_FF = ((8 * D_MODEL // 3 + 127) // 128) * 128
IN_W = 7 * D_MODEL
N_MOD = 9
ROPE_THETA = 10000.0
EPS = 1e-6

N_CTX = BATCH * SEQ
N_LAT = DEC_BATCH * DEC_SEQ
N_TOK = N_CTX + N_LAT
COND_ROWS = 8

ROW_TILE = 512
SUB_TILE = 256
FFN_TILE = 512
FF_CHUNK = 256
MOD_TILE = 2304
Q_TILE = 512
V7X_VMEM_BYTES = 64 * 1024 * 1024
VMEM_LIMIT_BYTES = V7X_VMEM_BYTES - 4 * 1024 * 1024

F32 = jnp.float32
BF16 = jnp.bfloat16


def _params():
    return pltpu.CompilerParams(vmem_limit_bytes=VMEM_LIMIT_BYTES)


def _resident(block_shape, index_map):
    return pl.BlockSpec(block_shape, index_map, pipeline_mode=pl.Buffered(1))


def _dot(a, b):
    return jnp.dot(a, b, preferred_element_type=F32)


def _dot_nt(a, b):
    return lax.dot_general(a, b, (((1,), (1,)), ((), ())), preferred_element_type=F32)


def _rms(x):
    return x * lax.rsqrt(jnp.mean(x * x, axis=-1, keepdims=True) + EPS)


def _norm_mod(x, g, shift, scale):
    return _rms(x) * g * (1.0 + scale) + shift


def _stacked_mod_row(i, tile=ROW_TILE):
    n_ctx_tiles = N_CTX // tile
    return jnp.where(i < n_ctx_tiles, 0, 1 + (i - n_ctx_tiles) // (DEC_SEQ // tile))


def _mod_chunk(mod_ref, row, k):
    return mod_ref[pl.ds(row, 1), k * D_MODEL:(k + 1) * D_MODEL]


def _mod_kernel(c_ref, w_ref, b_ref, o_ref):
    c = c_ref[...]
    o_ref[...] = _dot(c * jax.nn.sigmoid(c), w_ref[...]) + b_ref[...]


def _modulation(cond, w_mod, b_mod):
    n = N_MOD * D_MODEL
    return pl.pallas_call(
        _mod_kernel,
        grid=(DEPTH, n // MOD_TILE),
        in_specs=[
            pl.BlockSpec((COND_ROWS, D_MODEL), lambda l, j: (0, 0)),
            pl.BlockSpec((None, D_MODEL, MOD_TILE), lambda l, j: (l, 0, j)),
            pl.BlockSpec((None, 1, MOD_TILE), lambda l, j: (l, 0, j)),
        ],
        out_specs=pl.BlockSpec((None, COND_ROWS, MOD_TILE), lambda l, j: (l, 0, j)),
        out_shape=jax.ShapeDtypeStruct((DEPTH, COND_ROWS, n), F32),
        compiler_params=_params(),
        name="modulation",
    )(cond, w_mod, b_mod.reshape(DEPTH, 1, n))


def _ffn_weight_copies(w13_hbm, w2_hbm, w13_ref, w2_ref, sems, l, c):
    lo = pl.multiple_of(c * FF_CHUNK, FF_CHUNK)
    cols, up_cols = pl.ds(lo, FF_CHUNK), pl.ds(D_FF + lo, FF_CHUNK)
    return (pltpu.make_async_copy(w13_hbm.at[l, :, cols], w13_ref.at[:, cols], sems.at[c, 0]),
            pltpu.make_async_copy(w13_hbm.at[l, :, up_cols], w13_ref.at[:, up_cols], sems.at[c, 1]),
            pltpu.make_async_copy(w2_hbm.at[l, cols, :], w2_ref.at[cols, :], sems.at[c, 2]))


def _ffn_kernel(*refs, l, sub, split_in, final):
    n_x = 2 if split_in else 1
    x_refs, (mod_ref, g_ref, w13_hbm, w2_hbm) = refs[:n_x], refs[n_x:n_x + 4]
    *rest, w13_ref, w2_ref, acc_ref, sems = refs[n_x + 4:]
    i = pl.program_id(0)
    is_ctx = i < N_CTX // FFN_TILE
    row = _stacked_mod_row(i, FFN_TILE)
    n_chunks = D_FF // FF_CHUNK

    def normed():
        x = jnp.where(is_ctx, x_refs[0][...], x_refs[1][...]) if split_in else x_refs[0][...]
        return x, _norm_mod(x, g_ref[sub:sub + 1, :], _mod_chunk(mod_ref, row, 3 * sub),
                            _mod_chunk(mod_ref, row, 3 * sub + 1))

    def swiglu(gate, up, w2):
        return _dot(gate * jax.nn.sigmoid(gate) * up, w2)

    def finish(x, acc):
        y = x + 0.5 * _mod_chunk(mod_ref, row, 3 * sub + 2) * acc
        if not final:
            rest[0][...] = y
            return
        gf_ref, ctx_ref, lat_ref = rest
        y = _rms(y) * gf_ref[...]

        @pl.when(is_ctx)
        def _():
            ctx_ref[...] = y

        @pl.when(jnp.logical_not(is_ctx))
        def _():
            lat_ref[...] = y

    def first_step():
        def start(c, _):
            for cp in _ffn_weight_copies(w13_hbm, w2_hbm, w13_ref, w2_ref, sems, l, c):
                cp.start()
            return 0

        lax.fori_loop(0, n_chunks, start, 0)
        x, h = normed()
        acc_ref[...] = jnp.zeros_like(acc_ref)

        def chunk(c, _):
            for cp in _ffn_weight_copies(w13_hbm, w2_hbm, w13_ref, w2_ref, sems, l, c):
                cp.wait()
            lo = pl.multiple_of(c * FF_CHUNK, FF_CHUNK)
            gate = _dot(h, w13_ref[:, pl.ds(lo, FF_CHUNK)])
            up = _dot(h, w13_ref[:, pl.ds(D_FF + lo, FF_CHUNK)])
            acc_ref[...] += swiglu(gate, up, w2_ref[pl.ds(lo, FF_CHUNK), :])
            return 0

        lax.fori_loop(0, n_chunks, chunk, 0)
        finish(x, acc_ref[...])

    def later_step():
        x, h = normed()

        def gate_up(lo):
            return (_dot(h, w13_ref[:, lo:lo + FF_CHUNK]),
                    _dot(h, w13_ref[:, D_FF + lo:D_FF + lo + FF_CHUNK]))

        acc = jnp.zeros((FFN_TILE, D_MODEL), F32)
        nxt = gate_up(0)
        for lo in range(0, D_FF, FF_CHUNK):
            gate, up = nxt
            if lo + FF_CHUNK < D_FF:
                nxt = gate_up(lo + FF_CHUNK)
            acc = acc + swiglu(gate, up, w2_ref[lo:lo + FF_CHUNK, :])
        finish(x, acc)

    lax.cond(i == 0, first_step, later_step)


def _ffn(xs, mod, g_norm, w13, w2, g_final, l, sub):
    split_in, final = isinstance(xs, tuple), g_final is not None
    n_ctx_tiles = N_CTX // FFN_TILE
    tile = pl.BlockSpec((FFN_TILE, D_MODEL), lambda i: (i, 0))
    ctx_tile = pl.BlockSpec((FFN_TILE, D_MODEL), lambda i: (jnp.minimum(i, n_ctx_tiles - 1), 0))
    lat_tile = pl.BlockSpec((FFN_TILE, D_MODEL), lambda i: (jnp.maximum(i - n_ctx_tiles, 0), 0))
    in_specs = ([ctx_tile, lat_tile] if split_in else [tile]) + [
        _resident((None, COND_ROWS, N_MOD * D_MODEL), lambda i: (l, 0, 0)),
        _resident((None, 3, D_MODEL), lambda i: (l, 0, 0)),
        pl.BlockSpec(memory_space=pl.ANY),
        pl.BlockSpec(memory_space=pl.ANY),
    ]
    args = (list(xs) if split_in else [xs]) + [mod, g_norm, w13, w2]
    if final:
        in_specs.append(_resident((1, D_MODEL), lambda i: (0, 0)))
        args.append(g_final.reshape(1, D_MODEL))
        out_specs = [ctx_tile, lat_tile]
        out_shape = [jax.ShapeDtypeStruct((N_CTX, D_MODEL), F32), jax.ShapeDtypeStruct((N_LAT, D_MODEL), F32)]
    else:
        out_specs, out_shape = tile, jax.ShapeDtypeStruct((N_TOK, D_MODEL), F32)
    return pl.pallas_call(
        functools.partial(_ffn_kernel, l=l, sub=sub, split_in=split_in, final=final),
        grid=(N_TOK // FFN_TILE,),
        in_specs=in_specs,
        out_specs=out_specs,
        out_shape=out_shape,
        scratch_shapes=[pltpu.VMEM((D_MODEL, 2 * D_FF), F32), pltpu.VMEM((D_FF, D_MODEL), F32),
                        pltpu.VMEM((FFN_TILE, D_MODEL), F32),
                        pltpu.SemaphoreType.DMA((D_FF // FF_CHUNK, 3))],
        input_output_aliases={} if split_in or final else {0: 0},
        compiler_params=_params(),
        name="ffn",
    )(*args)


def _gelu_tanh(x):
    return x * (0.5 * (1.0 + jnp.tanh(math.sqrt(2.0 / math.pi) * (x + 0.044715 * (x * x * x)))))


def _rope_tables():
    pos = np.arange(DEC_SEQ)
    lane = np.arange(V_DIM)
    half = HEAD_DIM // 2
    within = lane % half
    coord = np.where((lane % HEAD_DIM) < half, pos[:, None] // GRID_W, pos[:, None] % GRID_W)
    freqs = ROPE_THETA ** (-np.arange(0, half, 2, dtype=np.float64) / half)
    ang = coord * freqs[within % (half // 2)][None, :]
    first = (within < half // 2)[None, :]
    identity = np.zeros((ROW_TILE, V_DIM))
    cos = np.concatenate([np.cos(ang), identity + 1.0])
    sin_lo = np.concatenate([np.where(first, -np.sin(ang), 0.0), identity])
    sin_hi = np.concatenate([np.where(first, 0.0, np.sin(ang)), identity])
    return tuple(jnp.asarray(t, dtype=F32) for t in (cos, sin_lo, sin_hi))


def _proj_kernel(x_ref, mod_ref, g_ref, wz_ref, wq_ref, wk_ref, wv_ref, gain_ref, ws_ref, bias_ref,
                 cos_ref, lo_ref, hi_ref, *rest, l):
    a_ref, qkv_ref, kc_hbm, vc_hbm, k_buf, v_buf, sems = rest[-7:]
    i = pl.program_id(0)
    row = _stacked_mod_row(i)
    n_ctx_tiles, seqs = N_CTX // ROW_TILE, ROW_TILE // SEQ
    slot = i % 2

    def cache_copies(step, slot):
        rows = pl.ds(step * seqs, seqs)
        return (pltpu.make_async_copy(k_buf.at[slot], kc_hbm.at[rows, l], sems.at[slot, 0]),
                pltpu.make_async_copy(v_buf.at[slot], vc_hbm.at[rows, l], sems.at[slot, 1]))

    @pl.when(jnp.logical_and(i >= 2, i < n_ctx_tiles + 2))
    def _():
        for cp in cache_copies(i - 2, slot):
            cp.wait()

    def project(r0):
        h = _norm_mod(x_ref[r0:r0 + SUB_TILE, :], g_ref[1:2, :], _mod_chunk(mod_ref, row, 3),
                      _mod_chunk(mod_ref, row, 4))
        z = _gelu_tanh(_dot(h, wz_ref[...]))
        return z, _dot(h, wq_ref[...]) * (HEAD_DIM ** -0.5), _dot(h, wk_ref[...]), _dot(h, wv_ref[...])

    def gate(r0, z):
        u = z[:, :D_MODEL]
        v = _rms(z[:, D_MODEL:]) * gain_ref[l:l + 1, :]
        for c in range(0, SUB_TILE, CHUNK):
            for g in range(N_GROUPS_A):
                c0 = g * GROUP_W
                mixed = _dot(ws_ref[l, g], v[c:c + CHUNK, c0:c0 + GROUP_W]) + bias_ref[:, c0:c0 + GROUP_W]
                a_ref[r0 + c:r0 + c + CHUNK, c0:c0 + GROUP_W] = (
                    u[c:c + CHUNK, c0:c0 + GROUP_W] * mixed).astype(BF16)

    def rotate(r0, q, k):
        rows = slice(r0, r0 + SUB_TILE)
        cos, sin_lo, sin_hi = cos_ref[rows, :], lo_ref[rows, :], hi_ref[rows, :]
        for c0 in range(0, D_MODEL, V_DIM):
            for src, base in ((q, 0), (k, D_MODEL)):
                t = src[:, c0:c0 + V_DIM]
                t = (t * cos + pltpu.roll(t, V_DIM - HEAD_DIM // 4, 1) * sin_lo
                     + pltpu.roll(t, HEAD_DIM // 4, 1) * sin_hi)
                qkv_ref[rows, base + c0:base + c0 + V_DIM] = t.astype(BF16)

    starts = list(range(0, ROW_TILE, SUB_TILE))
    nxt = project(starts[0])
    for n, r0 in enumerate(starts):
        z, q, k, v = nxt
        if n + 1 < len(starts):
            nxt = project(starts[n + 1])
        gate(r0, z)
        rotate(r0, q, k)
        qkv_ref[r0:r0 + SUB_TILE, 2 * D_MODEL:] = v.astype(BF16)
        k_buf[slot, n] = k.reshape(SEQ, N_HEADS, V_DIM)
        v_buf[slot, n] = v.reshape(SEQ, N_HEADS, V_DIM)

    @pl.when(i < n_ctx_tiles)
    def _():
        for cp in cache_copies(i, slot):
            cp.start()


def _proj(x, mod, g_norm, w_in, sgu_gain, w_spatial, bias_tile, rope, caches, l):
    assert SUB_TILE == SEQ
    n_ctx_tiles, lat_tiles = N_CTX // ROW_TILE, DEC_SEQ // ROW_TILE
    tile = lambda w: pl.BlockSpec((ROW_TILE, w), lambda i: (i, 0))
    w_block = lambda c: _resident((None, D_MODEL, D_MODEL), lambda i: (l, 0, c))
    table = pl.BlockSpec((ROW_TILE, V_DIM),
                         lambda i: (jnp.where(i < n_ctx_tiles, lat_tiles, (i - n_ctx_tiles) % lat_tiles), 0))
    hbm = pl.BlockSpec(memory_space=pl.ANY)
    stage = pltpu.VMEM((2, ROW_TILE // SEQ, SEQ, N_HEADS, V_DIM), F32)
    cache = jax.ShapeDtypeStruct((BATCH, DEPTH, SEQ, N_HEADS, V_DIM), F32)
    in_specs = [
        tile(D_MODEL),
        _resident((None, COND_ROWS, N_MOD * D_MODEL), lambda i: (l, 0, 0)),
        _resident((None, 3, D_MODEL), lambda i: (l, 0, 0)),
        _resident((None, D_MODEL, 2 * D_MODEL), lambda i: (l, 0, 0)),
        w_block(2), w_block(3), w_block(4),
        _resident((DEPTH, D_MODEL), lambda i: (0, 0)),
        _resident((DEPTH, N_GROUPS_A, CHUNK, CHUNK), lambda i: (0, 0, 0, 0)),
        _resident((None, CHUNK, D_MODEL), lambda i: (l, 0, 0)),
        table, table, table,
    ]
    args = [x, mod, g_norm, w_in, w_in, w_in, w_in, sgu_gain, w_spatial, bias_tile, *rope]
    aliases = {}
    if caches is not None:
        in_specs += [hbm] * 2
        aliases = {len(args): 2, len(args) + 1: 3}
        args += list(caches)
    return pl.pallas_call(
        functools.partial(_proj_kernel, l=l),
        grid=(N_TOK // ROW_TILE,),
        in_specs=in_specs,
        out_specs=[tile(D_MODEL), tile(3 * D_MODEL), hbm, hbm],
        out_shape=[jax.ShapeDtypeStruct((N_TOK, D_MODEL), BF16),
                   jax.ShapeDtypeStruct((N_TOK, 3 * D_MODEL), BF16), cache, cache],
        scratch_shapes=[stage, stage, pltpu.SemaphoreType.DMA((2, 2))],
        input_output_aliases=aliases,
        compiler_params=_params(),
        name="proj",
    )(*args)


def _lam_full(lam_ref, l):
    lp = lam_ref[l]
    s01 = jnp.sum(lp[0:1, :] * lp[1:2, :], axis=-1, keepdims=True)
    s23 = jnp.sum(lp[2:3, :] * lp[3:4, :], axis=-1, keepdims=True)
    return jnp.exp(s01) - jnp.exp(s23) + _lam_init(l)


def _lam_init(l):
    return 0.8 - 0.6 * math.exp(-0.3 * l)


def _attn_scores(q, ks):
    lane = lax.broadcasted_iota(jnp.int32, (1, V_DIM), 1)
    lo = (lane < HEAD_DIM).astype(BF16)
    qs = jnp.concatenate([q * lo, q * (1 - lo)], axis=0)
    return [_dot_nt(qs, k) for k in ks]


def _attn_combine(s, vs, lam_full, gain, l):
    n_q = s[0].shape[0] // 2
    m = functools.reduce(jnp.maximum, [jnp.max(t, axis=-1, keepdims=True) for t in s])
    pv = functools.reduce(jnp.add, [
        _dot(jnp.exp(t - m).astype(BF16), jnp.concatenate([v, jnp.ones_like(v)], axis=1))
        for t, v in zip(s, vs)])
    pv = pv[:, :V_DIM] / pv[:, V_DIM:]
    o = pv[:n_q] - lam_full * pv[n_q:]
    return _rms(o) * gain * (1.0 - _lam_init(l))


def _attn_heads(q_ref, k_refs, v_refs, lam_ref, gain_ref, o_ref, l, n_seqs=1):
    lam_full = _lam_full(lam_ref, l)
    gain = gain_ref[l:l + 1, :]
    units = [(sq, hd) for hd in range(N_HEADS) for sq in range(n_seqs)]

    def part(ref, sq, hd):
        n = ref.shape[0] // n_seqs
        return ref[sq * n:(sq + 1) * n, hd * V_DIM:(hd + 1) * V_DIM].astype(BF16)

    def scores(sq, hd):
        return _attn_scores(part(q_ref, sq, hd), [part(r, sq, hd) for r in k_refs])

    nxt = scores(*units[0])
    for n, (sq, hd) in enumerate(units):
        s = nxt
        if n + 1 < len(units):
            nxt = scores(*units[n + 1])
        o = _attn_combine(s, [part(r, sq, hd) for r in v_refs], lam_full, gain, l)
        rows = q_ref.shape[0] // n_seqs
        o_ref[sq * rows:(sq + 1) * rows, hd * V_DIM:(hd + 1) * V_DIM] = o.astype(BF16)


def _attn_lat_kernel(q_ref, k_ref, v_ref, past_k, past_v, lam_ref, gain_ref, o_ref, pk_ref, pv_ref, *, l):
    pk_ref[...] = past_k[...].reshape(PAST_LEN, D_MODEL).astype(BF16)
    pv_ref[...] = past_v[...].reshape(PAST_LEN, D_MODEL).astype(BF16)
    _attn_heads(q_ref, [pk_ref, k_ref], [pv_ref, v_ref], lam_ref, gain_ref, o_ref, l)


def _attn_lat(qkv, past_k, past_v, lam, subln_gain, l):
    tiles = DEC_SEQ // Q_TILE
    qspec = pl.BlockSpec((Q_TILE, D_MODEL), lambda b, j: (N_CTX // Q_TILE + b * tiles + j, 0))
    kspec = pl.BlockSpec((DEC_SEQ, D_MODEL), lambda b, j: (N_CTX // DEC_SEQ + b, 1))
    vspec = pl.BlockSpec((DEC_SEQ, D_MODEL), lambda b, j: (N_CTX // DEC_SEQ + b, 2))
    past = pl.BlockSpec((None, None, PAST_LEN, N_HEADS, V_DIM), lambda b, j: (b, l, 0, 0, 0))
    return pl.pallas_call(
        functools.partial(_attn_lat_kernel, l=l),
        grid=(DEC_BATCH, tiles),
        in_specs=[qspec, kspec, vspec, past, past,
                  _resident((DEPTH, 4, HEAD_DIM), lambda b, j: (0, 0, 0)),
                  _resident((DEPTH, V_DIM), lambda b, j: (0, 0))],
        out_specs=pl.BlockSpec((Q_TILE, D_MODEL), lambda b, j: (b * tiles + j, 0)),
        out_shape=jax.ShapeDtypeStruct((N_LAT, D_MODEL), BF16),
        scratch_shapes=[pltpu.VMEM((PAST_LEN, D_MODEL), BF16)] * 2,
        compiler_params=_params(),
        name="attn_latent",
    )(qkv, qkv, qkv, past_k, past_v, lam, subln_gain)


def _merge_weight_copies(w_in_hbm, wa_hbm, wb_hbm, wo_hbm, w_ref, sems, l):
    sources = [w_in_hbm.at[l, :, pl.ds(5 * D_MODEL, D_MODEL)], w_in_hbm.at[l, :, pl.ds(6 * D_MODEL, D_MODEL)],
               wa_hbm.at[l], wb_hbm.at[l], wo_hbm.at[l]]
    return [pltpu.make_async_copy(src, w_ref.at[k], sems.at[k]) for k, src in enumerate(sources)]


def _merge_kernel(x_ref, a_ref, b_lat_ref, q_ref, k_ref, v_ref, lam_ref, subln_ref, mod_ref, g_ref,
                  w_in_hbm, wa_hbm, wb_hbm, wo_hbm, o_ref, b_ref, w_ref, sems, *, l):
    i = pl.program_id(0)
    row = _stacked_mod_row(i)
    is_ctx = i < N_CTX // ROW_TILE

    @pl.when(is_ctx)
    def _():
        _attn_heads(q_ref, [k_ref], [v_ref], lam_ref, subln_ref, b_ref, l, n_seqs=ROW_TILE // SEQ)

    @pl.when(jnp.logical_not(is_ctx))
    def _():
        b_ref[...] = b_lat_ref[...]

    def body(copies):
        def weight(k):
            if copies is not None:
                copies[k].wait()
            return w_ref[k]

        x = x_ref[...]
        h = _norm_mod(x, g_ref[1:2, :], _mod_chunk(mod_ref, row, 3), _mod_chunk(mod_ref, row, 4))
        gate_a = jax.nn.sigmoid(_dot(h, weight(0)))
        gate_b = jax.nn.sigmoid(_dot(h, weight(1)))
        merged = (gate_a * _dot(a_ref[...].astype(F32), weight(2))
                  + gate_b * _dot(b_ref[...].astype(F32), weight(3)))
        o_ref[...] = x + _mod_chunk(mod_ref, row, 5) * _dot(merged, weight(4))

    def first_step():
        copies = _merge_weight_copies(w_in_hbm, wa_hbm, wb_hbm, wo_hbm, w_ref, sems, l)
        for cp in copies:
            cp.start()
        body(copies)

    lax.cond(i == 0, first_step, lambda: body(None))


def _merge(x, a, b_lat, qkv, lam, subln_gain, mod, g_norm, w_in, w_branch_a, w_branch_b, w_out, l):
    n_ctx_tiles = N_CTX // ROW_TILE
    tile = pl.BlockSpec((ROW_TILE, D_MODEL), lambda i: (i, 0))
    lat_tile = pl.BlockSpec((ROW_TILE, D_MODEL), lambda i: (jnp.maximum(i - n_ctx_tiles, 0), 0))
    ctx_part = [pl.BlockSpec((ROW_TILE, D_MODEL),
                             functools.partial(lambda i, c: (jnp.minimum(i, n_ctx_tiles - 1), c), c=c))
                for c in range(3)]
    hbm = pl.BlockSpec(memory_space=pl.ANY)
    return pl.pallas_call(
        functools.partial(_merge_kernel, l=l),
        grid=(N_TOK // ROW_TILE,),
        in_specs=[tile, tile, lat_tile] + ctx_part + [
                  _resident((DEPTH, 4, HEAD_DIM), lambda i: (0, 0, 0)),
                  _resident((DEPTH, V_DIM), lambda i: (0, 0)),
                  _resident((None, COND_ROWS, N_MOD * D_MODEL), lambda i: (l, 0, 0)),
                  _resident((None, 3, D_MODEL), lambda i: (l, 0, 0)),
                  hbm, hbm, hbm, hbm],
        out_specs=tile,
        out_shape=jax.ShapeDtypeStruct((N_TOK, D_MODEL), F32),
        scratch_shapes=[pltpu.VMEM((ROW_TILE, D_MODEL), BF16), pltpu.VMEM((5, D_MODEL, D_MODEL), F32),
                        pltpu.SemaphoreType.DMA((5,))],
        input_output_aliases={0: 0},
        compiler_params=_params(),
        name="merge",
    )(x, a, b_lat, qkv, qkv, qkv, lam, subln_gain, mod, g_norm, w_in, w_branch_a, w_branch_b, w_out)


def kernel(x_prompt, x_sample, cache_k, cache_v, c, c_ctx, w_mod, b_mod, g_norm, ffn1_w13, ffn1_w2, w_in,
           sgu_gain, w_spatial, b_spatial, lam, subln_gain, w_branch_a, w_branch_b, w_out, ffn2_w13,
           ffn2_w2, g_final):
    cond = jnp.concatenate(
        [c_ctx[None, :], c, jnp.zeros((COND_ROWS - 1 - DEC_BATCH, D_MODEL), F32)], axis=0)
    mod = _modulation(cond, w_mod, b_mod)
    bias_tile = jnp.repeat(jnp.swapaxes(b_spatial, 1, 2), GROUP_W, axis=2)
    rope = _rope_tables()

    x = (x_prompt.reshape(N_CTX, D_MODEL), x_sample.reshape(N_LAT, D_MODEL))
    caches = None
    for l in range(DEPTH):
        x = _ffn(x, mod, g_norm, ffn1_w13, ffn1_w2, None, l, 0)
        a, qkv, *caches = _proj(x, mod, g_norm, w_in, sgu_gain, w_spatial, bias_tile, rope, caches, l)
        b_lat = _attn_lat(qkv, cache_k, cache_v, lam, subln_gain, l)
        x = _merge(x, a, b_lat, qkv, lam, subln_gain, mod, g_norm, w_in, w_branch_a, w_branch_b, w_out, l)
        x = _ffn(x, mod, g_norm, ffn2_w13, ffn2_w2, g_final if l == DEPTH - 1 else None, l, 2)

    y_ctx, y_lat = x
    new_k, new_v = caches
    return y_ctx.reshape(BATCH, SEQ, D_MODEL), y_lat.reshape(DEC_BATCH, DEC_SEQ, D_MODEL), new_k, new_v
```
